```python
import math
import jax, jax.numpy as jnp
from jax import lax
import numpy as np

D_MODEL = 1024
BATCH = 4
SEQ = 8192
DEPTH = 2
DEC_BATCH = 32
DEC_SEQ = 64
PAST_LEN = 1024

CHUNK = 64
N_AB = (DEPTH + 1) // 2
N_CD = DEPTH // 2
MIX_WIDTH = D_MODEL
CONV_CH = MIX_WIDTH // 2
CONV_K = 31
DA_HEADS = 4
DA_HEAD_DIM = MIX_WIDTH // 2 // DA_HEADS // 2
DA_QK = DA_HEADS * 2 * DA_HEAD_DIM
Q_BLOCK = 128
GM_WIDTH = MIX_WIDTH // 2
GM_GROUPS = 4
GM_GROUP_CH = GM_WIDTH // GM_GROUPS
GM_CHUNK = 128
SSM_INNER = MIX_WIDTH // 2
SSM_HEAD_DIM = 64
SSM_HEADS = SSM_INNER // SSM_HEAD_DIM
SSM_GROUPS = 2
SSM_STATE = 128
SSM_CONV_K = 4
SSM_CONV_CH = SSM_INNER + 2 * SSM_GROUPS * SSM_STATE
SSD_CHUNK = CHUNK
N_MEM = 256
X_HEADS = 4
X_HEAD_DIM = D_MODEL // X_HEADS
PEER_HEADS = 8
PEER_KEYS = 128
PEER_EXPERTS = PEER_KEYS * PEER_KEYS
PEER_QDIM = 256
PEER_TOPK = 16
PEER_BLOCK = 128

AB_IN = 2 * CONV_CH + 3 * DA_QK
CD_IN = 2 * GM_WIDTH + SSM_INNER + SSM_CONV_CH + SSM_HEADS

kernel_name = 'hybrid_stream_encoder_step'


def _rms(x, g=None, eps=1e-6):
    xf = x.astype(jnp.float32)
    y = xf * lax.rsqrt(jnp.mean(xf * xf, axis=-1, keepdims=True) + eps)
    if g is not None:
        y = y * g.astype(jnp.float32)
    return y.astype(x.dtype)


def _layernorm(x, g, b, eps=1e-5):
    xf = x.astype(jnp.float32)
    xc = xf - jnp.mean(xf, axis=-1, keepdims=True)
    y = xc * lax.rsqrt(jnp.mean(xc * xc, axis=-1, keepdims=True) + eps)
    return (y * g.astype(jnp.float32) + b.astype(jnp.float32)).astype(x.dtype)


def _causal_dwconv(xpad, w, b):
    y = lax.conv_general_dilated(xpad, w[:, None, :].astype(xpad.dtype), window_strides=(1,), padding='VALID',
                                 dimension_numbers=('NWC', 'WIO', 'NWC'), feature_group_count=xpad.shape[-1])
    return y + b


def _diff_core(q, k, v, q_pos, k_pos, lam):
    s = jnp.einsum('bqhtd,bkhtd->bhtqk', q, k).astype(jnp.float32) * (DA_HEAD_DIM ** -0.5)
    allowed = (k_pos[None, :] // CHUNK) <= (q_pos[:, None] // CHUNK)
    p = jax.nn.softmax(jnp.where(allowed, s, -1e30), axis=-1)
    a = p[:, :, 0] - lam * p[:, :, 1]
    return jnp.einsum('bhqk,bkhe->bqhe', a.astype(v.dtype), v)


def _ab_mixer(h, conv_ctx, k_past, v_past, past_len, lam_init, w_in, conv_w, conv_b, ln_g, ln_b,
              lq1, lk1, lq2, lk2, subln_g, w_out):
    b, L, _ = h.shape
    z = h @ w_in
    a_val, a_gate, q, k, v = jnp.split(z, [CONV_CH, 2 * CONV_CH, 2 * CONV_CH + DA_QK, 2 * CONV_CH + 2 * DA_QK], axis=-1)
    glu = a_val * jax.nn.sigmoid(a_gate)
    if conv_ctx is None:
        conv_ctx = jnp.zeros((b, CONV_K - 1, CONV_CH), glu.dtype)
    xpad = jnp.concatenate([conv_ctx.astype(glu.dtype), glu], axis=1)
    ca = jax.nn.silu(_layernorm(_causal_dwconv(xpad, conv_w, conv_b), ln_g, ln_b))
    new_conv = xpad[:, -(CONV_K - 1):]
    q = q.reshape(b, L, DA_HEADS, 2, DA_HEAD_DIM)
    k = k.reshape(b, L, DA_HEADS, 2 * DA_HEAD_DIM)
    v = v.reshape(b, L, DA_HEADS, 2 * DA_HEAD_DIM)
    f32 = jnp.float32
    lam = (jnp.exp(jnp.sum(lq1.astype(f32) * lk1.astype(f32)))
           - jnp.exp(jnp.sum(lq2.astype(f32) * lk2.astype(f32))) + lam_init)
    q_pos = past_len + jnp.arange(L)
    if k_past is None:
        k_pos = jnp.arange(L)
        k5 = k.reshape(b, L, DA_HEADS, 2, DA_HEAD_DIM)
        nb = L // Q_BLOCK
        qb = jnp.swapaxes(q.reshape(b, nb, Q_BLOCK, DA_HEADS, 2, DA_HEAD_DIM), 0, 1)
        pb = q_pos.reshape(nb, Q_BLOCK)
        o = lax.map(lambda qp: _diff_core(qp[0], k5, v, qp[1], k_pos, lam), (qb, pb))
        o = jnp.swapaxes(o, 0, 1).reshape(b, L, DA_HEADS, 2 * DA_HEAD_DIM)
    else:
        kk = jnp.concatenate([k_past.astype(k.dtype), k], axis=1)
        vv = jnp.concatenate([v_past.astype(v.dtype), v], axis=1)
        k_pos = jnp.arange(kk.shape[1])
        o = _diff_core(q, kk.reshape(b, -1, DA_HEADS, 2, DA_HEAD_DIM), vv, q_pos, k_pos, lam)
    o = _rms(o, subln_g) * (1.0 - lam_init)
    out = jnp.concatenate([ca, o.reshape(b, L, DA_QK)], axis=-1) @ w_out
    return out, new_conv, k, v


def _ssd(x, dt, A, Bm, Cm, h0, chunk):
    b, L, H, P = x.shape
    rep = H // Bm.shape[2]
    nc = L // chunk
    f32 = jnp.float32
    x = x.astype(f32).reshape(b, nc, chunk, H, P)
    dt = dt.reshape(b, nc, chunk, H)
    Bh = jnp.repeat(Bm.astype(f32), rep, axis=2).reshape(b, nc, chunk, H, -1)
    Ch = jnp.repeat(Cm.astype(f32), rep, axis=2).reshape(b, nc, chunk, H, -1)
    acs = jnp.cumsum(dt * A, axis=2)
    seg = acs[:, :, :, None, :] - acs[:, :, None, :, :]
    tri = jnp.tril(jnp.ones((chunk, chunk), bool))[None, None, :, :, None]
    lmat = jnp.exp(jnp.where(tri, seg, -jnp.inf))
    xdt = x * dt[..., None]
    scores = jnp.einsum('bcihn,bcjhn->bcijh', Ch, Bh) * lmat
    y_diag = jnp.einsum('bcijh,bcjhp->bcihp', scores, xdt)
    decay = jnp.exp(acs[:, :, -1:, :] - acs)
    states = jnp.einsum('bcjhn,bcjh,bcjhp->bchpn', Bh, decay, xdt)
    chunk_decay = jnp.exp(acs[:, :, -1, :])

    def step(s, inp):
        st, dec = inp
        return s * dec[:, :, None, None] + st, s

    final, prev = lax.scan(step, h0.astype(f32), (jnp.moveaxis(states, 1, 0), jnp.moveaxis(chunk_decay, 1, 0)))
    prev = jnp.moveaxis(prev, 0, 1)
    y_off = jnp.einsum('bcihn,bchpn,bcih->bcihp', Ch, prev, jnp.exp(acs))
    return (y_diag + y_off).reshape(b, L, H, P), final


def _cd_mixer(h, conv_ctx, ssd_state, gm_len, ssd_chunk, w_in, ln_g, ln_b, w_s, b_s, conv_w, conv_b,
              dt_bias, a_log, d_skip, norm_g, w_out):
    b, L, _ = h.shape
    f32 = jnp.float32
    z = h @ w_in
    c_proj, gate, xbc, dt = jnp.split(z, [2 * GM_WIDTH, 2 * GM_WIDTH + SSM_INNER, 2 * GM_WIDTH + SSM_INNER + SSM_CONV_CH], axis=-1)
    u, vv = jnp.split(jax.nn.gelu(c_proj), 2, axis=-1)
    vv = _layernorm(vv, ln_g, ln_b).reshape(b, L // gm_len, gm_len, GM_GROUPS, GM_GROUP_CH)
    w = w_s[:, :gm_len, :gm_len] * jnp.tril(jnp.ones((gm_len, gm_len), w_s.dtype))
    mixed = jnp.einsum('gij,bcjgd->bcigd', w, vv) + jnp.swapaxes(b_s[:, :gm_len], 0, 1)[:, :, None]
    c_out = u * mixed.reshape(b, L, GM_WIDTH)
    new_v = vv.reshape(b, L, GM_GROUPS, GM_GROUP_CH)
    if conv_ctx is None:
        conv_ctx = jnp.zeros((b, SSM_CONV_K - 1, SSM_CONV_CH), xbc.dtype)
    xpad = jnp.concatenate([conv_ctx.astype(xbc.dtype), xbc], axis=1)
    xbc_c = jax.nn.silu(_causal_dwconv(xpad, conv_w, conv_b))
    new_conv = xpad[:, -(SSM_CONV_K - 1):]
    xs, Bm, Cm = jnp.split(xbc_c, [SSM_INNER, SSM_INNER + SSM_GROUPS * SSM_STATE], axis=-1)
    xs = xs.reshape(b, L, SSM_HEADS, SSM_HEAD_DIM)
    Bm = Bm.reshape(b, L, SSM_GROUPS, SSM_STATE)
    Cm = Cm.reshape(b, L, SSM_GROUPS, SSM_STATE)
    dtv = jax.nn.softplus(dt.astype(f32) + dt_bias.astype(f32))
    A = -jnp.exp(a_log.astype(f32))
    if ssd_state is None:
        ssd_state = jnp.zeros((b, SSM_HEADS, SSM_HEAD_DIM, SSM_STATE), f32)
    y, final = _ssd(xs, dtv, A, Bm, Cm, ssd_state, ssd_chunk)
    y = y + d_skip.astype(f32)[:, None] * xs.astype(f32)
    y = y.reshape(b, L, SSM_INNER) * jax.nn.silu(gate.astype(f32))
    y = _rms(y.reshape(b, L, SSM_GROUPS, SSM_INNER // SSM_GROUPS)).reshape(b, L, SSM_INNER) * norm_g.astype(f32)
    out = jnp.concatenate([c_out, y.astype(h.dtype)], axis=-1) @ w_out
    return out, new_v, new_conv, final.astype(h.dtype)


def _mem_kv(mem, w_k, w_v):
    b = mem.shape[0]
    return ((mem @ w_k).reshape(b, N_MEM, X_HEADS, X_HEAD_DIM),
            (mem @ w_v).reshape(b, N_MEM, X_HEADS, X_HEAD_DIM))


def _cross_attn(h, mk, mv, w_q, w_o):
    b, L, _ = h.shape
    q = (h @ w_q).reshape(b, L, X_HEADS, X_HEAD_DIM)
    s = jnp.einsum('bqhd,bkhd->bhqk', q, mk.astype(q.dtype)).astype(jnp.float32) * (X_HEAD_DIM ** -0.5)
    p = jax.nn.softmax(s, axis=-1).astype(q.dtype)
    o = jnp.einsum('bhqk,bkhd->bqhd', p, mv.astype(q.dtype)).reshape(b, L, D_MODEL)
    return o @ w_o


def _peer(h, w_q, sub_keys, eu, ev):
    b, L, D = h.shape
    blk = PEER_BLOCK if L % PEER_BLOCK == 0 else L
    xb = h.reshape(b * L // blk, blk, D)

    def one(x):
        t = x.shape[0]
        q = (x @ w_q).reshape(t, PEER_HEADS, 2, PEER_QDIM // 2)
        s = jnp.einsum('thcd,ckd->thck', q, sub_keys).astype(jnp.float32)
        ts, ti = lax.top_k(s, PEER_TOPK)
        cand_s = (ts[:, :, 0, :, None] + ts[:, :, 1, None, :]).reshape(t, PEER_HEADS, PEER_TOPK * PEER_TOPK)
        cand_i = (ti[:, :, 0, :, None] * PEER_KEYS + ti[:, :, 1, None, :]).reshape(t, PEER_HEADS, PEER_TOPK * PEER_TOPK)
        best_s, best_j = lax.top_k(cand_s, PEER_TOPK)
        idx = jnp.take_along_axis(cand_i, best_j, axis=-1)
        g = jax.nn.softmax(best_s, axis=-1)
        act = jax.nn.gelu(jnp.einsum('thkd,td->thk', eu[idx], x).astype(jnp.float32))
        return jnp.einsum('thk,thkd->td', (g * act).astype(x.dtype), ev[idx])

    return lax.map(one, xb).reshape(b, L, D)


def _trunk(x, mem_k, mem_v, attn_k, attn_v, conv_a, ssd_st, conv_ssm, past_len, gm_len, ssd_chunk, p):
    new_k, new_v, new_ca, new_gv, new_ssd, new_cs = [], [], [], [], [], []
    for l in range(DEPTH):
        i = l // 2
        h = _rms(x, p['norm_mix_g'][l])
        if l % 2 == 0:
            out, c_new, k_new, v_new = _ab_mixer(
                h, None if conv_a is None else conv_a[i], None if attn_k is None else attn_k[i],
                None if attn_v is None else attn_v[i], past_len, 0.8 - 0.6 * math.exp(-0.3 * l),
                p['w_in_ab'][i], p['conv_a_w'][i], p['conv_a_b'][i], p['ln_a_g'][i], p['ln_a_b'][i],
                p['lam_q1'][i], p['lam_k1'][i], p['lam_q2'][i], p['lam_k2'][i], p['subln_g'][i], p['w_out_ab'][i])
            new_k.append(k_new)
            new_v.append(v_new)
            new_ca.append(c_new)
        else:
            out, gv_new, cs_new, st_new = _cd_mixer(
                h, None if conv_ssm is None else conv_ssm[i], None if ssd_st is None else ssd_st[i], gm_len, ssd_chunk,
                p['w_in_cd'][i], p['ln_c_g'][i], p['ln_c_b'][i], p['gm_w_s'][i], p['gm_b_s'][i],
                p['conv_d_w'][i], p['conv_d_b'][i], p['dt_bias'][i], p['a_log'][i], p['d_skip'][i],
                p['norm_d_g'][i], p['w_out_cd'][i])
            new_gv.append(gv_new)
            new_cs.append(cs_new)
            new_ssd.append(st_new)
        x = x + out
        x = x + _cross_attn(_rms(x, p['norm_cross_g'][l]), mem_k[l], mem_v[l], p['w_xq'][l], p['w_xo'][l])
        x = x + _peer(_rms(x, p['norm_ffn_g'][l]), p['w_pq'][l], p['sub_keys'][l], p['expert_u'][l], p['expert_v'][l])
    return _rms(x, p['norm_final_g']), new_k, new_v, new_ca, new_gv, new_ssd, new_cs


def setup_inputs(seed: int = 0) -> dict:
    key = jax.random.key(seed)
    ks = iter(jax.random.split(key, 64))
    f32 = jnp.float32

    def nrm(shape, scale):
        return jax.random.normal(next(ks), shape, f32) * scale

    def gain(shape):
        return 1.0 + nrm(shape, 0.01)

    dt0 = jnp.exp(jax.random.uniform(next(ks), (N_CD, SSM_HEADS), f32, math.log(1e-3), math.log(1e-1)))
    a_log = jnp.log(jax.random.uniform(next(ks), (N_CD, SSM_HEADS), f32, 1.0, 16.0))
    return {
        'x_prompt': nrm((BATCH, SEQ, D_MODEL), 1.0),
        'x_sample': nrm((DEC_BATCH, DEC_SEQ, D_MODEL), 1.0),
        'cache_attn_k': nrm((N_AB, DEC_BATCH, PAST_LEN, DA_HEADS, 2 * DA_HEAD_DIM), 1.0),
        'cache_attn_v': nrm((N_AB, DEC_BATCH, PAST_LEN, DA_HEADS, 2 * DA_HEAD_DIM), 1.0),
        'state_conv_a': nrm((N_AB, DEC_BATCH, CONV_K - 1, CONV_CH), 0.5),
        'state_ssd': nrm((N_CD, DEC_BATCH, SSM_HEADS, SSM_HEAD_DIM, SSM_STATE), 0.1),
        'state_conv_ssm': nrm((N_CD, DEC_BATCH, SSM_CONV_K - 1, SSM_CONV_CH), 1.0),
        'cache_mem_k': nrm((DEPTH, DEC_BATCH, N_MEM, X_HEADS, X_HEAD_DIM), 1.0),
        'cache_mem_v': nrm((DEPTH, DEC_BATCH, N_MEM, X_HEADS, X_HEAD_DIM), 1.0),
        'mem_prompt': nrm((BATCH, N_MEM, D_MODEL), 1.0),
        'norm_mix_g': gain((DEPTH, D_MODEL)),
        'norm_cross_g': gain((DEPTH, D_MODEL)),
        'norm_ffn_g': gain((DEPTH, D_MODEL)),
        'norm_final_g': gain((D_MODEL,)),
        'w_in_ab': nrm((N_AB, D_MODEL, AB_IN), D_MODEL ** -0.5),
        'conv_a_w': nrm((N_AB, CONV_K, CONV_CH), CONV_K ** -0.5),
        'conv_a_b': nrm((N_AB, CONV_CH), 0.01),
        'ln_a_g': gain((N_AB, CONV_CH)),
        'ln_a_b': nrm((N_AB, CONV_CH), 0.01),
        'lam_q1': nrm((N_AB, DA_HEAD_DIM), 0.1),
        'lam_k1': nrm((N_AB, DA_HEAD_DIM), 0.1),
        'lam_q2': nrm((N_AB, DA_HEAD_DIM), 0.1),
        'lam_k2': nrm((N_AB, DA_HEAD_DIM), 0.1),
        'subln_g': gain((N_AB, 2 * DA_HEAD_DIM)),
        'w_out_ab': nrm((N_AB, MIX_WIDTH, D_MODEL), MIX_WIDTH ** -0.5),
        'w_in_cd': nrm((N_CD, D_MODEL, CD_IN), D_MODEL ** -0.5),
        'ln_c_g': gain((N_CD, GM_WIDTH)),
        'ln_c_b': nrm((N_CD, GM_WIDTH), 0.01),
        'gm_w_s': nrm((N_CD, GM_GROUPS, GM_CHUNK, GM_CHUNK), GM_CHUNK ** -0.5),
        'gm_b_s': gain((N_CD, GM_GROUPS, GM_CHUNK)),
        'conv_d_w': nrm((N_CD, SSM_CONV_K, SSM_CONV_CH), SSM_CONV_K ** -0.5),
        'conv_d_b': nrm((N_CD, SSM_CONV_CH), 0.01),
        'dt_bias': dt0 + jnp.log(-jnp.expm1(-dt0)),
        'a_log': a_log,
        'd_skip': gain((N_CD, SSM_HEADS)),
        'norm_d_g': gain((N_CD, SSM_INNER)),
        'w_out_cd': nrm((N_CD, MIX_WIDTH, D_MODEL), MIX_WIDTH ** -0.5),
        'w_xq': nrm((DEPTH, D_MODEL, D_MODEL), D_MODEL ** -0.5),
        'w_xk': nrm((DEPTH, D_MODEL, D_MODEL), D_MODEL ** -0.5),
        'w_xv': nrm((DEPTH, D_MODEL, D_MODEL), D_MODEL ** -0.5),
        'w_xo': nrm((DEPTH, D_MODEL, D_MODEL), D_MODEL ** -0.5),
        'w_pq': nrm((DEPTH, D_MODEL, PEER_HEADS * PEER_QDIM), D_MODEL ** -0.5),
        'sub_keys': nrm((DEPTH, 2, PEER_KEYS, PEER_QDIM // 2), (PEER_QDIM // 2) ** -0.5),
        'expert_u': nrm((DEPTH, PEER_EXPERTS, D_MODEL), D_MODEL ** -0.5),
        'expert_v': nrm((DEPTH, PEER_EXPERTS, D_MODEL), 0.1),
    }


def reference(x_prompt, x_sample, cache_attn_k, cache_attn_v, state_conv_a, state_ssd, state_conv_ssm,
              cache_mem_k, cache_mem_v, mem_prompt,
              norm_mix_g, norm_cross_g, norm_ffn_g, norm_final_g,
              w_in_ab, conv_a_w, conv_a_b, ln_a_g, ln_a_b, lam_q1, lam_k1, lam_q2, lam_k2, subln_g, w_out_ab,
              w_in_cd, ln_c_g, ln_c_b, gm_w_s, gm_b_s, conv_d_w, conv_d_b, dt_bias, a_log, d_skip, norm_d_g, w_out_cd,
              w_xq, w_xk, w_xv, w_xo, w_pq, sub_keys, expert_u, expert_v):
    p = {
        'norm_mix_g': norm_mix_g, 'norm_cross_g': norm_cross_g, 'norm_ffn_g': norm_ffn_g, 'norm_final_g': norm_final_g,
        'w_in_ab': w_in_ab, 'conv_a_w': conv_a_w, 'conv_a_b': conv_a_b, 'ln_a_g': ln_a_g, 'ln_a_b': ln_a_b,
        'lam_q1': lam_q1, 'lam_k1': lam_k1, 'lam_q2': lam_q2, 'lam_k2': lam_k2, 'subln_g': subln_g, 'w_out_ab': w_out_ab,
        'w_in_cd': w_in_cd, 'ln_c_g': ln_c_g, 'ln_c_b': ln_c_b, 'gm_w_s': gm_w_s, 'gm_b_s': gm_b_s,
        'conv_d_w': conv_d_w, 'conv_d_b': conv_d_b, 'dt_bias': dt_bias, 'a_log': a_log, 'd_skip': d_skip,
        'norm_d_g': norm_d_g, 'w_out_cd': w_out_cd, 'w_xq': w_xq, 'w_xo': w_xo,
        'w_pq': w_pq, 'sub_keys': sub_keys, 'expert_u': expert_u, 'expert_v': expert_v,
    }
    mem_kv = [_mem_kv(mem_prompt, w_xk[l], w_xv[l]) for l in range(DEPTH)]
    mem_k_p = jnp.stack([kv[0] for kv in mem_kv])
    mem_v_p = jnp.stack([kv[1] for kv in mem_kv])
    y_prompt, kp, vp, cap, _, ssdp, csp = _trunk(
        x_prompt, mem_k_p, mem_v_p, None, None, None, None, None, 0, GM_CHUNK, SSD_CHUNK, p)
    ds = x_sample.shape[1]
    y_sample, ks, vs, cas, gvs, ssds, css = _trunk(
        x_sample, cache_mem_k, cache_mem_v, cache_attn_k, cache_attn_v, state_conv_a, state_ssd, state_conv_ssm,
        cache_attn_k.shape[2], ds, ds, p)
    return (y_prompt, y_sample,
            jnp.stack(kp), jnp.stack(vp), jnp.stack(cap), jnp.stack(ssdp), jnp.stack(csp), mem_k_p, mem_v_p,
            jnp.stack(ks), jnp.stack(vs), jnp.stack(cas), jnp.stack(gvs), jnp.stack(ssds), jnp.stack(css))
```

```python
import functools
import math

import jax
import jax.numpy as jnp
from jax import lax
from jax.experimental import pallas as pl
from jax.experimental.pallas import tpu as pltpu

F32 = jnp.float32
BF16 = jnp.bfloat16
NEG_INF = float("-inf")

D_MODEL = 1024
CHUNK = 64
CONV_CH = 512
CONV_K = 31
DA_HEADS = 4
DA_HEAD_DIM = 64
DA_QK = 512
GM_WIDTH = 512
GM_GROUPS = 4
SSM_INNER = 512
SSM_HEADS = 8
SSM_HEAD_DIM = 64
SSM_STATE = 128
SSM_CONV_K = 4
SSM_CONV_CH = 1024
N_MEM = 256
X_HEADS = 4
X_HEAD_DIM = 256
PEER_HEADS = 8
PEER_KEYS = 128
PEER_EXPERTS = PEER_KEYS * PEER_KEYS
PEER_TOPK = 16
LANES = 128
VMEM_LIMIT = 56 * 1024 * 1024


def _cparams(sem):
    return pltpu.CompilerParams(dimension_semantics=sem, vmem_limit_bytes=VMEM_LIMIT)


def _dot(a, b):
    return jnp.dot(a, b, preferred_element_type=F32)


def _dot_nt(a, b):
    return lax.dot_general(a, b, (((1,), (1,)), ((), ())), preferred_element_type=F32)


def _dot_f32(a, b):
    return jnp.dot(a, b, preferred_element_type=F32, precision=lax.Precision.HIGHEST)


def _rms(x, g, eps=1e-6):
    return x * lax.rsqrt(jnp.mean(x * x, axis=-1, keepdims=True) + eps) * g


def _layernorm(x, g, b, eps=1e-5):
    xc = x - jnp.mean(x, axis=-1, keepdims=True)
    return xc * lax.rsqrt(jnp.mean(xc * xc, axis=-1, keepdims=True) + eps) * g + b


def _silu(x):
    return x * jax.nn.sigmoid(x)


def _softplus(x):
    return jnp.maximum(x, 0.0) + jnp.log(1.0 + jnp.exp(-jnp.abs(x)))


def _row_tile(n, pref=512):
    return pref if n % pref == 0 else n


def _ab_in_kernel(x_ref, g_ref, w_ref, glu_ref, q_ref, k32_ref, v32_ref, k16_ref, v16_ref):
    h = _rms(x_ref[...], g_ref[...]).astype(BF16)
    a_val = _dot(h, w_ref[:, 0:512])
    a_gate = _dot(h, w_ref[:, 512:1024])
    glu_ref[...] = a_val * jax.nn.sigmoid(a_gate)
    q = _dot(h, w_ref[:, 1024:1536])
    q_ref[...] = (q * (DA_HEAD_DIM ** -0.5)).astype(BF16)
    k = _dot(h, w_ref[:, 1536:2048])
    k32_ref[...] = k
    k16_ref[...] = k.astype(BF16)
    v = _dot(h, w_ref[:, 2048:2560])
    v32_ref[...] = v
    v16_ref[...] = v.astype(BF16)


def _ab_in(x, g, w):
    n = x.shape[0]
    tm = _row_tile(n)
    row = lambda c: pl.BlockSpec((tm, c), lambda i: (i, 0))
    full = lambda a: pl.BlockSpec(a.shape, lambda i: (0,) * a.ndim)
    return pl.pallas_call(
        _ab_in_kernel,
        grid=(n // tm,),
        in_specs=[row(D_MODEL), full(g), full(w)],
        out_specs=[row(512)] * 6,
        out_shape=[jax.ShapeDtypeStruct((n, 512), F32), jax.ShapeDtypeStruct((n, 512), BF16),
                   jax.ShapeDtypeStruct((n, 512), F32), jax.ShapeDtypeStruct((n, 512), F32),
                   jax.ShapeDtypeStruct((n, 512), BF16), jax.ShapeDtypeStruct((n, 512), BF16)],
        compiler_params=_cparams(("parallel",)),
        name="ab_in",
    )(x, g, w)


CONV_HALO = 32
CONV_RB = 64


def _conv_a_kernel(glu_ref, halo_ref, ctx_ref, w_ref, b_ref, lg_ref, lb_ref, o_ref, xp_scr, *, tl):
    i = pl.program_id(1)
    pad = CONV_HALO - (CONV_K - 1)

    @pl.when(i == 0)
    def _():
        xp_scr[pad:CONV_HALO, :] = ctx_ref[0]

    @pl.when(i > 0)
    def _():
        xp_scr[0:CONV_HALO, :] = halo_ref[0]

    xp_scr[CONV_HALO:CONV_HALO + tl, :] = glu_ref[0]
    rb = min(CONV_RB, tl)
    for r0 in range(0, tl, rb):
        acc = jnp.broadcast_to(b_ref[...], (rb, CONV_CH))
        for k in range(CONV_K):
            acc = acc + w_ref[k:k + 1, :] * xp_scr[r0 + pad + k:r0 + pad + k + rb, :]
        y = _layernorm(acc, lg_ref[...], lb_ref[...])
        o_ref[0, r0:r0 + rb, :] = _silu(y).astype(BF16)


def _conv_a(glu, ctx, w, b, lg, lb):
    bsz, l, _ = glu.shape
    tl = _row_tile(l)
    hb = tl // CONV_HALO
    full = lambda a: pl.BlockSpec(a.shape, lambda bi, i: (0,) * a.ndim)
    return pl.pallas_call(
        functools.partial(_conv_a_kernel, tl=tl),
        grid=(bsz, l // tl),
        in_specs=[pl.BlockSpec((1, tl, CONV_CH), lambda bi, i: (bi, i, 0)),
                  pl.BlockSpec((1, CONV_HALO, CONV_CH), lambda bi, i: (bi, jnp.maximum(i * hb - 1, 0), 0)),
                  pl.BlockSpec((1, CONV_K - 1, CONV_CH), lambda bi, i: (bi, 0, 0)),
                  full(w), full(b), full(lg), full(lb)],
        out_specs=pl.BlockSpec((1, tl, CONV_CH), lambda bi, i: (bi, i, 0)),
        out_shape=jax.ShapeDtypeStruct((bsz, l, CONV_CH), BF16),
        scratch_shapes=[pltpu.VMEM((CONV_HALO + tl, CONV_CH), F32)],
        compiler_params=_cparams(("parallel", "arbitrary")),
        name="conv_a",
    )(glu, glu, ctx, w, b, lg, lb)


def _lambda(lq1, lk1, lq2, lk2, lam_init):
    return (jnp.exp(jnp.sum(lq1[...] * lk1[...], axis=-1, keepdims=True))
            - jnp.exp(jnp.sum(lq2[...] * lk2[...], axis=-1, keepdims=True)) + lam_init)


def _split_q(q):
    lane = lax.broadcasted_iota(jnp.int32, q.shape, 1)
    z = jnp.zeros_like(q)
    return jnp.concatenate([jnp.where(lane < DA_HEAD_DIM, q, z), jnp.where(lane >= DA_HEAD_DIM, q, z)], axis=0)


def _diff_finish(acc, l, tq, lam, sg, lam_init):
    o = acc[0:tq] / l[0:tq] - lam * (acc[tq:2 * tq] / l[tq:2 * tq])
    return _rms(o, sg) * (1.0 - lam_init)


def _attn_prompt_kernel(qi_ref, ki_ref, q_ref, k_ref, v_ref, lq1, lk1, lq2, lk2, sg_ref, o_ref,
                        qs_scr, m_scr, l_scr, acc_scr, *, tq, lam_init):
    p = pl.program_id(2)
    qi = qi_ref[p]
    ki = ki_ref[p]

    @pl.when(ki == 0)
    def _():
        qs_scr[...] = _split_q(q_ref[0])
        m_scr[...] = jnp.full(m_scr.shape, NEG_INF, F32)
        l_scr[...] = jnp.zeros(l_scr.shape, F32)
        acc_scr[...] = jnp.zeros(acc_scr.shape, F32)

    def step(masked):
        s = _dot_nt(qs_scr[...], k_ref[0])
        if masked:
            r = lax.broadcasted_iota(jnp.int32, s.shape, 0)
            c = lax.broadcasted_iota(jnp.int32, s.shape, 1)
            qrow = jnp.where(r >= tq, r - tq, r)
            s = jnp.where((c >> 6) <= (qrow >> 6), s, -1e30)
        m_old = m_scr[...]
        m_new = jnp.maximum(m_old, jnp.max(s, axis=-1, keepdims=True))
        alpha = jnp.exp(m_old - m_new)
        pr = jnp.exp(s - m_new)
        l_scr[...] = alpha * l_scr[...] + jnp.sum(pr, axis=-1, keepdims=True)
        acc_scr[...] = alpha * acc_scr[...] + _dot(pr.astype(BF16), v_ref[0])
        m_scr[...] = m_new

    @pl.when(ki < qi)
    def _():
        step(False)

    @pl.when(ki == qi)
    def _():
        step(True)
        lam = _lambda(lq1, lk1, lq2, lk2, lam_init)
        o_ref[0] = _diff_finish(acc_scr[...], l_scr[...], tq, lam, sg_ref[...], lam_init).astype(BF16)


def _attn_prompt(q, k, v, lq1, lk1, lq2, lk2, sg, lam_init):
    bsz, l, _ = q.shape
    tq = _row_tile(l)
    nq = l // tq
    pairs = [(a, b) for a in range(nq) for b in range(a + 1)]
    qi_tab = jnp.asarray([a for a, _ in pairs], jnp.int32)
    ki_tab = jnp.asarray([b for _, b in pairs], jnp.int32)
    small = lambda a: pl.BlockSpec(a.shape, lambda bi, h, p, qt, kt: (0,) * a.ndim)
    grid_spec = pltpu.PrefetchScalarGridSpec(
        num_scalar_prefetch=2,
        grid=(bsz, DA_HEADS, len(pairs)),
        in_specs=[pl.BlockSpec((1, tq, LANES), lambda bi, h, p, qt, kt: (bi, qt[p], h)),
                  pl.BlockSpec((1, tq, LANES), lambda bi, h, p, qt, kt: (bi, kt[p], h)),
                  pl.BlockSpec((1, tq, LANES), lambda bi, h, p, qt, kt: (bi, kt[p], h)),
                  small(lq1), small(lk1), small(lq2), small(lk2), small(sg)],
        out_specs=pl.BlockSpec((1, tq, LANES), lambda bi, h, p, qt, kt: (bi, qt[p], h)),
        scratch_shapes=[pltpu.VMEM((2 * tq, LANES), BF16), pltpu.VMEM((2 * tq, 1), F32),
                        pltpu.VMEM((2 * tq, 1), F32), pltpu.VMEM((2 * tq, LANES), F32)],
    )
    return pl.pallas_call(
        functools.partial(_attn_prompt_kernel, tq=tq, lam_init=lam_init),
        grid_spec=grid_spec,
        out_shape=jax.ShapeDtypeStruct((bsz, l, DA_QK), BF16),
        compiler_params=_cparams(("parallel", "parallel", "arbitrary")),
        name="attn_prompt",
    )(qi_tab, ki_tab, q, k, v, lq1, lk1, lq2, lk2, sg)


def _attn_sample_kernel(q_ref, kc_ref, vc_ref, kn_ref, vn_ref, lq1, lk1, lq2, lk2, sg_ref, o_ref,
                        *, tq, past_len, lam_init):
    qs = _split_q(q_ref[0])
    s_c = _dot_nt(qs, kc_ref[0].astype(BF16))
    s_n = _dot_nt(qs, kn_ref[0])
    r = lax.broadcasted_iota(jnp.int32, s_n.shape, 0)
    c = lax.broadcasted_iota(jnp.int32, s_n.shape, 1)
    qrow = jnp.where(r >= tq, r - tq, r)
    s_n = jnp.where(((past_len + c) >> 6) <= ((past_len + qrow) >> 6), s_n, -1e30)
    m = jnp.maximum(jnp.max(s_c, axis=-1, keepdims=True), jnp.max(s_n, axis=-1, keepdims=True))
    p_c = jnp.exp(s_c - m)
    p_n = jnp.exp(s_n - m)
    l = jnp.sum(p_c, axis=-1, keepdims=True) + jnp.sum(p_n, axis=-1, keepdims=True)
    acc = _dot(p_c.astype(BF16), vc_ref[0].astype(BF16)) + _dot(p_n.astype(BF16), vn_ref[0])
    lam = _lambda(lq1, lk1, lq2, lk2, lam_init)
    o_ref[0] = _diff_finish(acc, l, tq, lam, sg_ref[...], lam_init).astype(BF16)


def _attn_sample(q, k, v, k_past, v_past, lq1, lk1, lq2, lk2, sg, lam_init):
    bsz, l, _ = q.shape
    past_len = k_past.shape[1]
    small = lambda a: pl.BlockSpec(a.shape, lambda bi, h: (0,) * a.ndim)
    new = pl.BlockSpec((1, l, LANES), lambda bi, h: (bi, 0, h))
    old = pl.BlockSpec((1, past_len, LANES), lambda bi, h: (bi, 0, h))
    return pl.pallas_call(
        functools.partial(_attn_sample_kernel, tq=l, past_len=past_len, lam_init=lam_init),
        grid=(bsz, DA_HEADS),
        in_specs=[new, old, old, new, new, small(lq1), small(lk1), small(lq2), small(lk2), small(sg)],
        out_specs=new,
        out_shape=jax.ShapeDtypeStruct((bsz, l, DA_QK), BF16),
        compiler_params=_cparams(("parallel", "parallel")),
        name="attn_sample",
    )(q, k_past, v_past, k, v, lq1, lk1, lq2, lk2, sg)


def _out_proj_kernel(x_ref, a_ref, b_ref, w_ref, o_ref):
    o_ref[...] = x_ref[...] + _dot(a_ref[...], w_ref[0:512, :]) + _dot(b_ref[...], w_ref[512:1024, :])


def _out_proj(x, a, b, w):
    n = x.shape[0]
    tm = _row_tile(n)
    row = lambda c: pl.BlockSpec((tm, c), lambda i: (i, 0))
    return pl.pallas_call(
        _out_proj_kernel,
        grid=(n // tm,),
        in_specs=[row(D_MODEL), row(512), row(512), pl.BlockSpec(w.shape, lambda i: (0, 0))],
        out_specs=row(D_MODEL),
        out_shape=jax.ShapeDtypeStruct((n, D_MODEL), F32),
        compiler_params=_cparams(("parallel",)),
        name="out_proj",
    )(x, a, b, w)


def _mem_kv_kernel(m_ref, wk_ref, wv_ref, k32_ref, v32_ref, k16_ref, v16_ref):
    m = m_ref[...].astype(BF16)
    k = _dot(m, wk_ref[0])
    v = _dot(m, wv_ref[0])
    k32_ref[0] = k
    v32_ref[0] = v
    k16_ref[0] = k.astype(BF16)
    v16_ref[0] = v.astype(BF16)


def _mem_kv(mem, wk, wv):
    n = mem.shape[0]
    depth = wk.shape[0]
    tm = _row_tile(n)
    wspec = pl.BlockSpec((1, D_MODEL, D_MODEL), lambda l, i: (l, 0, 0))
    ospec = pl.BlockSpec((1, tm, D_MODEL), lambda l, i: (l, i, 0))
    return pl.pallas_call(
        _mem_kv_kernel,
        grid=(depth, n // tm),
        in_specs=[pl.BlockSpec((tm, D_MODEL), lambda l, i: (i, 0)), wspec, wspec],
        out_specs=[ospec] * 4,
        out_shape=[jax.ShapeDtypeStruct((depth, n, D_MODEL), F32)] * 2
        + [jax.ShapeDtypeStruct((depth, n, D_MODEL), BF16)] * 2,
        compiler_params=_cparams(("parallel", "parallel")),
        name="mem_kv",
    )(mem, wk, wv)


def _cross_kernel(x_ref, g_ref, wq_ref, wo_ref, mk_ref, mv_ref, o_ref):
    x = x_ref[0]
    h = _rms(x, g_ref[...]).astype(BF16)
    q = _dot(h, wq_ref[...]).astype(BF16)
    outs = []
    for hd in range(X_HEADS):
        sl = slice(hd * X_HEAD_DIM, (hd + 1) * X_HEAD_DIM)
        s = _dot_nt(q[:, sl], mk_ref[0, :, sl].astype(BF16)) * (X_HEAD_DIM ** -0.5)
        s = s - jnp.max(s, axis=-1, keepdims=True)
        e = jnp.exp(s)
        p = (e / jnp.sum(e, axis=-1, keepdims=True)).astype(BF16)
        outs.append(_dot(p, mv_ref[0, :, sl].astype(BF16)).astype(BF16))
    o = jnp.concatenate(outs, axis=-1)
    o_ref[0] = x + _dot(o, wo_ref[...])


def _cross(x, g, wq, wo, mk, mv):
    bsz, l, _ = x.shape
    tm = _row_tile(l)
    full = lambda a: pl.BlockSpec(a.shape, lambda bi, i: (0,) * a.ndim)
    xs = pl.BlockSpec((1, tm, D_MODEL), lambda bi, i: (bi, i, 0))
    ms = pl.BlockSpec((1, N_MEM, D_MODEL), lambda bi, i: (bi, 0, 0))
    return pl.pallas_call(
        _cross_kernel,
        grid=(bsz, l // tm),
        in_specs=[xs, full(g), full(wq), full(wo), ms, ms],
        out_specs=xs,
        out_shape=jax.ShapeDtypeStruct((bsz, l, D_MODEL), F32),
        compiler_params=_cparams(("parallel", "parallel")),
        name="cross_attn",
    )(x, g, wq, wo, mk, mv)


PEER_T = 512
PEER_EC = 512
N_CAND = 56


def _top_values(s, k):
    out = []
    work = s
    for _ in range(k):
        m = jnp.max(work, axis=0, keepdims=True)
        out.append(m)
        work = jnp.where(work >= m, NEG_INF, work)
    return out


def _peer_route_kernel(x_ref, g_ref, wq_ref, sk_ref, ht_ref, cut_ref, e0_ref, s1_ref, e1_ref):
    h = _rms(x_ref[...], g_ref[...])
    ht_ref[...] = h.T.astype(BF16)
    q = _dot(h.astype(BF16), wq_ref[...]).astype(BF16)
    t = h.shape[0]
    for hd in range(PEER_HEADS):
        base = hd * 2 * PEER_KEYS
        s0 = _dot_nt(sk_ref[0], q[:, base:base + PEER_KEYS])
        s1 = _dot_nt(sk_ref[1], q[:, base + PEER_KEYS:base + 2 * PEER_KEYS])
        u0 = _top_values(s0, PEER_TOPK + 1)
        u1 = _top_values(s1, PEER_TOPK + 1)
        cands = [u0[a] + u1[b] for a in range(PEER_TOPK + 1) for b in range(PEER_TOPK + 1)
                 if (a + 1) * (b + 1) <= PEER_TOPK + 1]
        cands += [jnp.full((1, t), NEG_INF, F32)] * (N_CAND - len(cands))
        best = _top_values(jnp.concatenate(cands, axis=0), PEER_TOPK + 1)
        thr = 0.5 * (best[PEER_TOPK - 1] + best[PEER_TOPK])
        z = jnp.ones_like(best[0])
        for b in best[1:PEER_TOPK]:
            z = z + jnp.exp(b - best[0])
        cut_ref[hd] = thr - s0
        e0_ref[hd] = jnp.exp(s0 - u0[0]) / z
        s1_ref[hd] = s1
        e1_ref[hd] = jnp.exp(s1 - u1[0])


def _peer_route(x, g, wq, sk):
    n = x.shape[0]
    t = _row_tile(n, PEER_T)
    full = lambda a: pl.BlockSpec(a.shape, lambda i: (0,) * a.ndim)
    rspec = pl.BlockSpec((PEER_HEADS, PEER_KEYS, t), lambda i: (0, 0, i))
    rshape = jax.ShapeDtypeStruct((PEER_HEADS, PEER_KEYS, n), F32)
    return pl.pallas_call(
        _peer_route_kernel,
        grid=(n // t,),
        in_specs=[pl.BlockSpec((t, D_MODEL), lambda i: (i, 0)), full(g), full(wq), full(sk)],
        out_specs=[pl.BlockSpec((D_MODEL, t), lambda i: (0, i)), rspec, rspec, rspec, rspec],
        out_shape=[jax.ShapeDtypeStruct((D_MODEL, n), BF16), rshape, rshape, rshape, rshape],
        compiler_params=_cparams(("parallel",)),
        name="peer_route",
    )(x, g, wq, sk)


def _peer_dense_kernel(ht_ref, cut_ref, e0_ref, s1_ref, e1_ref, eu_ref, evt_ref, x_ref, o_ref,
                       acc_scr, p_scr):
    j = pl.program_id(1)
    rows = PEER_EC // PEER_KEYS

    @pl.when(j == 0)
    def _():
        acc_scr[...] = jnp.zeros(acc_scr.shape, F32)

    at = _dot(eu_ref[...], ht_ref[...])
    for ii in range(rows):
        i0 = j * rows + ii
        act = jax.nn.gelu(at[ii * PEER_KEYS:(ii + 1) * PEER_KEYS])
        w = jnp.zeros(act.shape, F32)
        for hd in range(PEER_HEADS):
            cut = cut_ref[hd, pl.ds(i0, 1), :]
            e0 = e0_ref[hd, pl.ds(i0, 1), :]
            w = w + jnp.where(s1_ref[hd] >= cut, e1_ref[hd] * e0, 0.0)
        p_scr[ii * PEER_KEYS:(ii + 1) * PEER_KEYS, :] = (w * act).astype(BF16)
    acc_scr[...] += _dot(evt_ref[...], p_scr[...])

    @pl.when(j == pl.num_programs(1) - 1)
    def _():
        o_ref[...] = x_ref[...] + acc_scr[...].T


def _peer_dense(x, ht, cut, e0, s1, e1, eu, evt):
    n = x.shape[0]
    t = _row_tile(n, PEER_T)
    rspec = pl.BlockSpec((PEER_HEADS, PEER_KEYS, t), lambda i, j: (0, 0, i))
    xspec = pl.BlockSpec((t, D_MODEL), lambda i, j: (i, 0))
    return pl.pallas_call(
        _peer_dense_kernel,
        grid=(n // t, PEER_EXPERTS // PEER_EC),
        in_specs=[pl.BlockSpec((D_MODEL, t), lambda i, j: (0, i)), rspec, rspec, rspec, rspec,
                  pl.BlockSpec((PEER_EC, D_MODEL), lambda i, j: (j, 0)),
                  pl.BlockSpec((D_MODEL, PEER_EC), lambda i, j: (0, j)), xspec],
        out_specs=xspec,
        out_shape=jax.ShapeDtypeStruct((n, D_MODEL), F32),
        scratch_shapes=[pltpu.VMEM((D_MODEL, t), F32), pltpu.VMEM((PEER_EC, t), BF16)],
        compiler_params=_cparams(("parallel", "arbitrary")),
        name="peer_dense",
    )(ht, cut, e0, s1, e1, eu, evt, x)


def _cd_in_kernel(x_ref, g_ref, wc_ref, wg_ref, wx_ref, wd_ref, lg_ref, lb_ref,
                  u_ref, v_ref, gate_ref, xbc_ref, dt_ref):
    h = _rms(x_ref[...], g_ref[...]).astype(BF16)
    u_ref[...] = jax.nn.gelu(_dot(h, wc_ref[:, 0:512]))
    v_ref[...] = _layernorm(jax.nn.gelu(_dot(h, wc_ref[:, 512:1024])), lg_ref[...], lb_ref[...])
    gate_ref[...] = _dot(h, wg_ref[...])
    xbc_ref[...] = _dot(h, wx_ref[...])
    dt_ref[...] = _dot(h, wd_ref[...])


def _cd_in(x, g, wc, wg, wx, wd, lg, lb):
    n = x.shape[0]
    tm = _row_tile(n)
    row = lambda c: pl.BlockSpec((tm, c), lambda i: (i, 0))
    full = lambda a: pl.BlockSpec(a.shape, lambda i: (0,) * a.ndim)
    widths = (512, 512, 512, SSM_CONV_CH, LANES)
    return pl.pallas_call(
        _cd_in_kernel,
        grid=(n // tm,),
        in_specs=[row(D_MODEL), full(g), full(wc), full(wg), full(wx), full(wd), full(lg), full(lb)],
        out_specs=[row(c) for c in widths],
        out_shape=[jax.ShapeDtypeStruct((n, c), F32) for c in widths],
        compiler_params=_cparams(("parallel",)),
        name="cd_in",
    )(x, g, wc, wg, wx, wd, lg, lb)


def _gmlp_kernel(u_ref, v_ref, ws_ref, bs_ref, o_ref, *, gm_len, n_chunks):
    r = lax.broadcasted_iota(jnp.int32, (gm_len, gm_len), 0)
    c = lax.broadcasted_iota(jnp.int32, (gm_len, gm_len), 1)
    for g in range(GM_GROUPS):
        w = jnp.where(r >= c, ws_ref[g], 0.0).astype(BF16)
        ch = slice(g * LANES, (g + 1) * LANES)
        for ci in range(n_chunks):
            rows = slice(ci * gm_len, (ci + 1) * gm_len)
            mixed = _dot(w, v_ref[0, rows, ch].astype(BF16)) + bs_ref[:, g:g + 1]
            o_ref[0, rows, ch] = (u_ref[0, rows, ch] * mixed).astype(BF16)


def _gmlp(u, v, ws, bs_t, gm_len):
    bsz, l, _ = u.shape
    tl = _row_tile(l)
    full = lambda a: pl.BlockSpec(a.shape, lambda bi, i: (0,) * a.ndim)
    spec = pl.BlockSpec((1, tl, GM_WIDTH), lambda bi, i: (bi, i, 0))
    return pl.pallas_call(
        functools.partial(_gmlp_kernel, gm_len=gm_len, n_chunks=tl // gm_len),
        grid=(bsz, l // tl),
        in_specs=[spec, spec, full(ws), full(bs_t)],
        out_specs=spec,
        out_shape=jax.ShapeDtypeStruct((bsz, l, GM_WIDTH), BF16),
        compiler_params=_cparams(("parallel", "parallel")),
        name="gmlp",
    )(u, v, ws, bs_t)


SSD_HALO = 8
SSD_PAIRS = SSM_HEADS // 2
SSD_GROUP_W = SSM_INNER // 2


def _ssd_kernel(xbc_ref, halo_ref, ctx_ref, gate_ref, dt_ref, h0_ref, cw_ref, cb_ref, dtb_ref, alog_ref,
                dsk_ref, ng_ref, y_ref, fin_ref, xp_scr, st_scr, *, tl):
    i = pl.program_id(1)
    pad = SSD_HALO - (SSM_CONV_K - 1)
    q = CHUNK

    @pl.when(i == 0)
    def _():
        xp_scr[pad:SSD_HALO, :] = ctx_ref[0]
        for k in range(SSD_PAIRS):
            st_scr[k] = jnp.concatenate([h0_ref[0, 2 * k], h0_ref[0, 2 * k + 1]], axis=0).T

    @pl.when(i > 0)
    def _():
        xp_scr[0:SSD_HALO, :] = halo_ref[0]

    xp_scr[SSD_HALO:SSD_HALO + tl, :] = xbc_ref[0]

    lane128 = lax.broadcasted_iota(jnp.int32, (q, LANES), 1)
    row128 = lax.broadcasted_iota(jnp.int32, (q, LANES), 0)
    er = lax.broadcasted_iota(jnp.int32, (LANES, SSM_INNER), 0)
    ec = lax.broadcasted_iota(jnp.int32, (LANES, SSM_INNER), 1)
    expand = jnp.where((ec >> 6) == er, 1.0, 0.0).astype(F32)
    tr = lax.broadcasted_iota(jnp.int32, (q, q), 0)
    tc = lax.broadcasted_iota(jnp.int32, (q, q), 1)
    ltri = jnp.where(tr >= tc, 1.0, 0.0).astype(F32)
    a_neg = -jnp.exp(alog_ref[...])
    zeros_q = jnp.zeros((q, LANES), F32)

    for ci in range(tl // q):
        r0 = ci * q
        acc = jnp.broadcast_to(cb_ref[...], (q, SSM_CONV_CH))
        for k in range(SSM_CONV_K):
            acc = acc + cw_ref[k:k + 1, :] * xp_scr[r0 + pad + k:r0 + pad + k + q, :]
        xc = _silu(acc)
        xs = xc[:, 0:SSM_INNER]
        dt = jnp.where(lane128 < SSM_HEADS, _softplus(dt_ref[0, r0:r0 + q, :] + dtb_ref[...]), 0.0)
        acs = _dot_f32(ltri, dt * a_neg)
        acs_e = _dot_f32(acs, expand)
        dt_e = _dot_f32(dt, expand)
        tot_e = acs_e[q - 1:q, :]
        xdt = xs * dt_e
        xd = (xdt * jnp.exp(tot_e - acs_e)).astype(BF16)
        eacs = jnp.exp(acs_e)
        cdec = jnp.exp(tot_e)
        ys = []
        for k in range(SSD_PAIRS):
            g = k // 2
            blk = slice(k * LANES, (k + 1) * LANES)
            bm = xc[:, SSM_INNER + g * SSM_STATE:SSM_INNER + (g + 1) * SSM_STATE]
            cm = xc[:, SSM_INNER + 2 * SSM_STATE + g * SSM_STATE:
                    SSM_INNER + 2 * SSM_STATE + (g + 1) * SSM_STATE].astype(BF16)
            cb2 = _dot_nt(cm, jnp.concatenate([bm, bm], axis=0).astype(BF16))
            a_blk = acs_e[:, blk]
            a_row = jnp.sum(jnp.where(row128 == (lane128 & (q - 1)), a_blk, 0.0), axis=0, keepdims=True)
            lmat = jnp.exp(jnp.where((lane128 & (q - 1)) <= row128, a_blk - a_row, NEG_INF))
            sc = (cb2 * lmat).astype(BF16)
            x_blk = xdt[:, blk]
            rhs = jnp.concatenate([jnp.where(lane128 < q, x_blk, 0.0), jnp.where(lane128 >= q, x_blk, 0.0)],
                                  axis=0).astype(BF16)
            prev = st_scr[k]
            y_pair = _dot(sc, rhs) + _dot(cm, prev.astype(BF16)) * eacs[:, blk]
            ys.append(y_pair)
            bt = jnp.concatenate([bm, zeros_q], axis=0).T.astype(BF16)
            xd_pad = jnp.concatenate([xd[:, blk], zeros_q.astype(BF16)], axis=0)
            st_scr[k] = prev * cdec[:, blk] + _dot(bt, xd_pad)
        y = jnp.concatenate(ys, axis=-1) + dsk_ref[...] * xs
        y = y * _silu(gate_ref[0, r0:r0 + q, :])
        outs = []
        for g in range(2):
            seg = y[:, g * SSD_GROUP_W:(g + 1) * SSD_GROUP_W]
            outs.append(seg * lax.rsqrt(jnp.mean(seg * seg, axis=-1, keepdims=True) + 1e-6))
        y_ref[0, r0:r0 + q, :] = (jnp.concatenate(outs, axis=-1) * ng_ref[...]).astype(BF16)

    @pl.when(i == pl.num_programs(1) - 1)
    def _():
        for k in range(SSD_PAIRS):
            st = st_scr[k].T
            fin_ref[0, 2 * k] = st[0:SSM_HEAD_DIM]
            fin_ref[0, 2 * k + 1] = st[SSM_HEAD_DIM:2 * SSM_HEAD_DIM]


def _ssd(xbc, ctx, gate, dt, h0, cw, cb, dtb, alog, dsk, ng):
    bsz, l, _ = xbc.shape
    tl = 256 if l % 256 == 0 else l
    hb = tl // SSD_HALO
    full = lambda a: pl.BlockSpec(a.shape, lambda bi, i: (0,) * a.ndim)
    tile = lambda c: pl.BlockSpec((1, tl, c), lambda bi, i: (bi, i, 0))
    stspec = pl.BlockSpec((1, SSM_HEADS, SSM_HEAD_DIM, SSM_STATE), lambda bi, i: (bi, 0, 0, 0))
    return pl.pallas_call(
        functools.partial(_ssd_kernel, tl=tl),
        grid=(bsz, l // tl),
        in_specs=[tile(SSM_CONV_CH),
                  pl.BlockSpec((1, SSD_HALO, SSM_CONV_CH), lambda bi, i: (bi, jnp.maximum(i * hb - 1, 0), 0)),
                  pl.BlockSpec((1, SSM_CONV_K - 1, SSM_CONV_CH), lambda bi, i: (bi, 0, 0)),
                  tile(SSM_INNER), tile(LANES), stspec,
                  full(cw), full(cb), full(dtb), full(alog), full(dsk), full(ng)],
        out_specs=[tile(SSM_INNER), stspec],
        out_shape=[jax.ShapeDtypeStruct((bsz, l, SSM_INNER), BF16),
                   jax.ShapeDtypeStruct((bsz, SSM_HEADS, SSM_HEAD_DIM, SSM_STATE), F32)],
        scratch_shapes=[pltpu.VMEM((SSD_HALO + tl, SSM_CONV_CH), F32),
                        pltpu.VMEM((SSD_PAIRS, SSM_STATE, LANES), F32)],
        compiler_params=_cparams(("parallel", "arbitrary")),
        name="ssd",
    )(xbc, xbc, ctx, gate, dt, h0, cw, cb, dtb, alog, dsk, ng)


def _final_rms_kernel(x_ref, g_ref, o_ref):
    o_ref[...] = _rms(x_ref[...], g_ref[...])


def _final_rms(x, g):
    n = x.shape[0]
    tm = _row_tile(n)
    row = pl.BlockSpec((tm, D_MODEL), lambda i: (i, 0))
    return pl.pallas_call(
        _final_rms_kernel,
        grid=(n // tm,),
        in_specs=[row, pl.BlockSpec(g.shape, lambda i: (0, 0))],
        out_specs=row,
        out_shape=jax.ShapeDtypeStruct((n, D_MODEL), F32),
        compiler_params=_cparams(("parallel",)),
        name="final_rms",
    )(x, g)


def _pad_lanes(a, width=LANES):
    return jnp.pad(a, ((0, 0), (0, width - a.shape[-1])))


def _trunk(x, mem_k, mem_v, attn_k, attn_v, conv_a, ssd_st, conv_ssm, gm_len, p):
    bsz, l, _ = x.shape
    n = bsz * l
    row = lambda a: a.reshape(1, -1)
    x2 = x.reshape(n, D_MODEL)

    lam_init = 0.8 - 0.6 * math.exp(-0.3 * 0)
    glu, q16, k32, v32, k16, v16 = _ab_in(x2, row(p["norm_mix_g"][0]), p["w_in_ab"])
    glu3 = glu.reshape(bsz, l, CONV_CH)
    ctx_a = jnp.zeros((bsz, CONV_K - 1, CONV_CH), F32) if conv_a is None else conv_a[0]
    ca = _conv_a(glu3, ctx_a, p["conv_a_w"], row(p["conv_a_b"]), row(p["ln_a_g"]), row(p["ln_a_b"]))
    lam_args = (row(p["lam_q1"]), row(p["lam_k1"]), row(p["lam_q2"]), row(p["lam_k2"]), row(p["subln_g"]))
    shp3 = lambda a: a.reshape(bsz, l, DA_QK)
    if attn_k is None:
        o = _attn_prompt(shp3(q16), shp3(k16), shp3(v16), *lam_args, lam_init)
    else:
        past = attn_k.shape[2]
        o = _attn_sample(shp3(q16), shp3(k16), shp3(v16), attn_k[0].reshape(bsz, past, DA_QK),
                         attn_v[0].reshape(bsz, past, DA_QK), *lam_args, lam_init)
    x2 = _out_proj(x2, ca.reshape(n, CONV_CH), o.reshape(n, DA_QK), p["w_out_ab"])
    new_k = k32.reshape(1, bsz, l, DA_HEADS, 2 * DA_HEAD_DIM)
    new_v = v32.reshape(1, bsz, l, DA_HEADS, 2 * DA_HEAD_DIM)
    new_ca = glu3[:, l - (CONV_K - 1):][None]

    def tail(x2, layer):
        x3 = _cross(x2.reshape(bsz, l, D_MODEL), row(p["norm_cross_g"][layer]), p["w_xq"][layer],
                    p["w_xo"][layer], mem_k[layer], mem_v[layer]).reshape(n, D_MODEL)
        routed = _peer_route(x3, row(p["norm_ffn_g"][layer]), p["w_pq"][layer], p["sub_keys"][layer])
        return _peer_dense(x3, *routed, p["expert_u"][layer], p["expert_vt"][layer])

    x2 = tail(x2, 0)

    u, vln, gate, xbc, dt = _cd_in(x2, row(p["norm_mix_g"][1]), p["w_cd_c"], p["w_cd_gate"], p["w_cd_xbc"],
                                   p["w_cd_dt"], row(p["ln_c_g"]), row(p["ln_c_b"]))
    shp = lambda a: a.reshape(bsz, l, a.shape[-1])
    c_out = _gmlp(shp(u), shp(vln), p["gm_w_s"][:, :gm_len, :gm_len], p["gm_b_s"][:, :gm_len].T, gm_len)
    xbc3 = shp(xbc)
    ctx_d = jnp.zeros((bsz, SSM_CONV_K - 1, SSM_CONV_CH), F32) if conv_ssm is None else conv_ssm[0]
    h0 = jnp.zeros((bsz, SSM_HEADS, SSM_HEAD_DIM, SSM_STATE), F32) if ssd_st is None else ssd_st[0]
    y, fin = _ssd(xbc3, ctx_d, shp(gate), shp(dt), h0, p["conv_d_w"], row(p["conv_d_b"]),
                  _pad_lanes(row(p["dt_bias"])), _pad_lanes(row(p["a_log"])),
                  row(jnp.repeat(p["d_skip"], SSM_HEAD_DIM)), row(p["norm_d_g"]))
    x2 = _out_proj(x2, c_out.reshape(n, GM_WIDTH), y.reshape(n, SSM_INNER), p["w_out_cd"])
    x2 = tail(x2, 1)
    y_out = _final_rms(x2, row(p["norm_final_g"])).reshape(bsz, l, D_MODEL)
    new_gv = vln.reshape(1, bsz, l, GM_GROUPS, GM_WIDTH // GM_GROUPS)
    new_cs = xbc3[:, l - (SSM_CONV_K - 1):][None]
    return y_out, new_k, new_v, new_ca, new_gv, fin[None], new_cs


def kernel(x_prompt, x_sample, cache_attn_k, cache_attn_v, state_conv_a, state_ssd, state_conv_ssm, cache_mem_k, cache_mem_v, mem_prompt, norm_mix_g, norm_cross_g, norm_ffn_g, norm_final_g, w_in_ab, conv_a_w, conv_a_b, ln_a_g, ln_a_b, lam_q1, lam_k1, lam_q2, lam_k2, subln_g, w_out_ab, w_in_cd, ln_c_g, ln_c_b, gm_w_s, gm_b_s, conv_d_w, conv_d_b, dt_bias, a_log, d_skip, norm_d_g, w_out_cd, w_xq, w_xk, w_xv, w_xo, w_pq, sub_keys, expert_u, expert_v):
    bf = lambda a: a.astype(BF16)
    w_cd = w_in_cd[0]
    p = {
        "norm_mix_g": norm_mix_g, "norm_cross_g": norm_cross_g, "norm_ffn_g": norm_ffn_g,
        "norm_final_g": norm_final_g,
        "w_in_ab": bf(w_in_ab[0]), "conv_a_w": conv_a_w[0], "conv_a_b": conv_a_b[0],
        "ln_a_g": ln_a_g[0], "ln_a_b": ln_a_b[0],
        "lam_q1": lam_q1[0], "lam_k1": lam_k1[0], "lam_q2": lam_q2[0], "lam_k2": lam_k2[0],
        "subln_g": subln_g[0], "w_out_ab": bf(w_out_ab[0]),
        "w_cd_c": bf(w_cd[:, 0:1024]), "w_cd_gate": bf(w_cd[:, 1024:1536]), "w_cd_xbc": bf(w_cd[:, 1536:2560]),
        "w_cd_dt": bf(_pad_lanes(w_cd[:, 2560:2568])),
        "ln_c_g": ln_c_g[0], "ln_c_b": ln_c_b[0], "gm_w_s": gm_w_s[0], "gm_b_s": gm_b_s[0],
        "conv_d_w": conv_d_w[0], "conv_d_b": conv_d_b[0], "dt_bias": dt_bias[0], "a_log": a_log[0],
        "d_skip": d_skip[0], "norm_d_g": norm_d_g[0], "w_out_cd": bf(w_out_cd[0]),
        "w_xq": bf(w_xq), "w_xo": bf(w_xo), "w_pq": bf(w_pq), "sub_keys": bf(sub_keys),
        "expert_u": bf(expert_u), "expert_vt": jnp.swapaxes(bf(expert_v), 1, 2),
    }
    bsz, seq, _ = x_prompt.shape
    dec_b, dec_l, _ = x_sample.shape
    depth = w_xk.shape[0]

    mk32, mv32, mk16, mv16 = _mem_kv(mem_prompt.reshape(bsz * N_MEM, D_MODEL), bf(w_xk), bf(w_xv))
    mem_k_p = mk32.reshape(depth, bsz, N_MEM, X_HEADS, X_HEAD_DIM)
    mem_v_p = mv32.reshape(depth, bsz, N_MEM, X_HEADS, X_HEAD_DIM)
    y_prompt, kp, vp, cap, _, ssdp, csp = _trunk(
        x_prompt, mk16.reshape(depth, bsz, N_MEM, D_MODEL), mv16.reshape(depth, bsz, N_MEM, D_MODEL),
        None, None, None, None, None, 2 * CHUNK, p)

    y_sample, ks, vs, cas, gvs, ssds, css = _trunk(
        x_sample, cache_mem_k.reshape(depth, dec_b, N_MEM, D_MODEL), cache_mem_v.reshape(depth, dec_b, N_MEM, D_MODEL),
        cache_attn_k, cache_attn_v, state_conv_a, state_ssd, state_conv_ssm, dec_l, p)
    return (y_prompt, y_sample, kp, vp, cap, ssdp, csp, mem_k_p, mem_v_p, ks, vs, cas, gvs, ssds, css)
```

```python
import functools
import math

import jax
import jax.numpy as jnp
from jax import lax
from jax.experimental import pallas as pl
from jax.experimental.pallas import tpu as pltpu

F32 = jnp.float32
BF16 = jnp.bfloat16
NEG_INF = float("-inf")

D_MODEL = 1024
CHUNK = 64
CONV_CH = 512
CONV_K = 31
DA_HEADS = 4
DA_HEAD_DIM = 64
DA_QK = 512
GM_WIDTH = 512
GM_GROUPS = 4
SSM_INNER = 512
SSM_HEADS = 8
SSM_HEAD_DIM = 64
SSM_STATE = 128
SSM_CONV_K = 4
SSM_CONV_CH = 1024
N_MEM = 256
X_HEADS = 4
X_HEAD_DIM = 256
PEER_HEADS = 8
PEER_KEYS = 128
PEER_EXPERTS = PEER_KEYS * PEER_KEYS
PEER_TOPK = 16
LANES = 128
SUBLANES = 8
BF16_ROWS = 16
VMEM_LIMIT = 56 * 1024 * 1024


def _cparams(sem):
    return pltpu.CompilerParams(dimension_semantics=sem, vmem_limit_bytes=VMEM_LIMIT)


def _dot(a, b):
    return jnp.dot(a, b, preferred_element_type=F32)


def _dot_nt(a, b):
    return lax.dot_general(a, b, (((1,), (1,)), ((), ())), preferred_element_type=F32)


def _dot_f32(a, b):
    return jnp.dot(a, b, preferred_element_type=F32, precision=lax.Precision.HIGHEST)


def _rms(x, g, eps=1e-6):
    return x * lax.rsqrt(jnp.mean(x * x, axis=-1, keepdims=True) + eps) * g


def _layernorm(x, g, b, eps=1e-5):
    xc = x - jnp.mean(x, axis=-1, keepdims=True)
    return xc * lax.rsqrt(jnp.mean(xc * xc, axis=-1, keepdims=True) + eps) * g + b


def _silu(x):
    return x * jax.nn.sigmoid(x)


def _softplus(x):
    return jnp.maximum(x, 0.0) + jnp.log(1.0 + jnp.exp(-jnp.abs(x)))


def _row_tile(n, pref=512):
    return pref if n % pref == 0 else n


def _ab_in_kernel(x_ref, g_ref, w_ref, glu_ref, q_ref, k32_ref, v32_ref, k16_ref, v16_ref):
    h = _rms(x_ref[...], g_ref[...]).astype(BF16)
    a_val = _dot(h, w_ref[:, 0:512])
    a_gate = _dot(h, w_ref[:, 512:1024])
    glu_ref[...] = a_val * jax.nn.sigmoid(a_gate)
    q = _dot(h, w_ref[:, 1024:1536])
    q_ref[...] = (q * (DA_HEAD_DIM ** -0.5)).astype(BF16)
    k = _dot(h, w_ref[:, 1536:2048])
    k32_ref[...] = k
    k16_ref[...] = k.astype(BF16)
    v = _dot(h, w_ref[:, 2048:2560])
    v32_ref[...] = v
    v16_ref[...] = v.astype(BF16)


def _ab_in(x, g, w):
    n = x.shape[0]
    tm = _row_tile(n)
    row = lambda c: pl.BlockSpec((tm, c), lambda i: (i, 0))
    full = lambda a: pl.BlockSpec(a.shape, lambda i: (0,) * a.ndim)
    return pl.pallas_call(
        _ab_in_kernel,
        grid=(n // tm,),
        in_specs=[row(D_MODEL), full(g), full(w)],
        out_specs=[row(512)] * 6,
        out_shape=[jax.ShapeDtypeStruct((n, 512), F32), jax.ShapeDtypeStruct((n, 512), BF16),
                   jax.ShapeDtypeStruct((n, 512), F32), jax.ShapeDtypeStruct((n, 512), F32),
                   jax.ShapeDtypeStruct((n, 512), BF16), jax.ShapeDtypeStruct((n, 512), BF16)],
        compiler_params=_cparams(("parallel",)),
        name="ab_in",
    )(x, g, w)


CONV_HALO = 32
CONV_RB = 64


def _conv_a_kernel(glu_ref, halo_ref, ctx_ref, w_ref, b_ref, lg_ref, lb_ref, o_ref, xp_scr, *, tl):
    i = pl.program_id(1)
    pad = CONV_HALO - (CONV_K - 1)

    @pl.when(i == 0)
    def _():
        xp_scr[pad:CONV_HALO, :] = ctx_ref[0]

    @pl.when(i > 0)
    def _():
        xp_scr[0:CONV_HALO, :] = halo_ref[0]

    xp_scr[CONV_HALO:CONV_HALO + tl, :] = glu_ref[0]
    rb = min(CONV_RB, tl)
    for r0 in range(0, tl, rb):
        acc = jnp.broadcast_to(b_ref[...], (rb, CONV_CH))
        for k in range(CONV_K):
            acc = acc + w_ref[k:k + 1, :] * xp_scr[r0 + pad + k:r0 + pad + k + rb, :]
        y = _layernorm(acc, lg_ref[...], lb_ref[...])
        o_ref[0, r0:r0 + rb, :] = _silu(y).astype(BF16)


def _conv_a(glu, ctx, w, b, lg, lb):
    bsz, l, _ = glu.shape
    tl = _row_tile(l)
    hb = tl // CONV_HALO
    full = lambda a: pl.BlockSpec(a.shape, lambda bi, i: (0,) * a.ndim)
    return pl.pallas_call(
        functools.partial(_conv_a_kernel, tl=tl),
        grid=(bsz, l // tl),
        in_specs=[pl.BlockSpec((1, tl, CONV_CH), lambda bi, i: (bi, i, 0)),
                  pl.BlockSpec((1, CONV_HALO, CONV_CH), lambda bi, i: (bi, jnp.maximum(i * hb - 1, 0), 0)),
                  pl.BlockSpec((1, CONV_K - 1, CONV_CH), lambda bi, i: (bi, 0, 0)),
                  full(w), full(b), full(lg), full(lb)],
        out_specs=pl.BlockSpec((1, tl, CONV_CH), lambda bi, i: (bi, i, 0)),
        out_shape=jax.ShapeDtypeStruct((bsz, l, CONV_CH), BF16),
        scratch_shapes=[pltpu.VMEM((CONV_HALO + tl, CONV_CH), F32)],
        compiler_params=_cparams(("parallel", "arbitrary")),
        name="conv_a",
    )(glu, glu, ctx, w, b, lg, lb)


def _lambda(lq1, lk1, lq2, lk2, lam_init):
    return (jnp.exp(jnp.sum(lq1[...] * lk1[...], axis=-1, keepdims=True))
            - jnp.exp(jnp.sum(lq2[...] * lk2[...], axis=-1, keepdims=True)) + lam_init)


def _split_q(q):
    lane = lax.broadcasted_iota(jnp.int32, q.shape, 1)
    z = jnp.zeros_like(q)
    return jnp.concatenate([jnp.where(lane < DA_HEAD_DIM, q, z), jnp.where(lane >= DA_HEAD_DIM, q, z)], axis=0)


def _diff_finish(acc, l, tq, lam, sg, lam_init):
    o = acc[0:tq] / l[0:tq] - lam * (acc[tq:2 * tq] / l[tq:2 * tq])
    return _rms(o, sg) * (1.0 - lam_init)


ATTN_TQ = 512
ATTN_TK = 512


def _attn_prompt_kernel(qi_ref, ki_ref, fl_ref, q_ref, k_ref, v_ref, lq1, lk1, lq2, lk2, sg_ref, o_ref,
                        qs_scr, m_scr, acc_scr, *, tq, tk, lam_init):
    p = pl.program_id(1)
    qi = qi_ref[p]
    ki = ki_ref[p]
    flags = fl_ref[p]

    @pl.when(ki == 0)
    def _():
        for h in range(DA_HEADS):
            qs_scr[h] = _split_q(q_ref[0, :, h * LANES:(h + 1) * LANES])
        m_scr[...] = jnp.full(m_scr.shape, NEG_INF, F32)
        acc_scr[...] = jnp.zeros(acc_scr.shape, F32)

    ones_col = jnp.where(lax.broadcasted_iota(jnp.int32, (tk, LANES), 1) == 0, 1.0, 0.0).astype(BF16)

    def step(masked):
        for h in range(DA_HEADS):
            sl = slice(h * LANES, (h + 1) * LANES)
            s = _dot_nt(qs_scr[h], k_ref[0, :, sl])
            if masked:
                r = lax.broadcasted_iota(jnp.int32, s.shape, 0)
                c = lax.broadcasted_iota(jnp.int32, s.shape, 1)
                q_pos = qi * tq + jnp.where(r >= tq, r - tq, r)
                s = jnp.where(((ki * tk + c) >> 6) <= (q_pos >> 6), s, -1e30)
            m_old = m_scr[h]
            m_new = jnp.maximum(m_old, jnp.max(s, axis=-1, keepdims=True))
            pr = jnp.exp(s - m_new).astype(BF16)
            v_aug = jnp.concatenate([v_ref[0, :, sl], ones_col], axis=1)
            acc_scr[h] = jnp.exp(m_old - m_new) * acc_scr[h] + _dot(pr, v_aug)
            m_scr[h] = m_new

    @pl.when((flags & 1) == 0)
    def _():
        step(False)

    @pl.when((flags & 1) == 1)
    def _():
        step(True)

    @pl.when((flags & 2) != 0)
    def _():
        lam = _lambda(lq1, lk1, lq2, lk2, lam_init)
        for h in range(DA_HEADS):
            acc = acc_scr[h]
            o = _diff_finish(acc[:, 0:LANES], acc[:, LANES:LANES + 1], tq, lam, sg_ref[...], lam_init)
            o_ref[0, :, h * LANES:(h + 1) * LANES] = o.astype(BF16)


def _attn_prompt(q, k, v, lq1, lk1, lq2, lk2, sg, lam_init):
    bsz, l, _ = q.shape
    tq = _row_tile(l, ATTN_TQ)
    tk = _row_tile(l, ATTN_TK)
    pairs = []
    for a in range(l // tq):
        last = ((a + 1) * tq - 1) // tk
        for b in range(last + 1):
            crosses = (b + 1) * tk > a * tq + CHUNK
            pairs.append((a, b, int(crosses) + 2 * int(b == last)))
    tabs = [jnp.asarray([pr[i] for pr in pairs], jnp.int32) for i in range(3)]
    small = lambda a: pl.BlockSpec(a.shape, lambda bi, p, qt, kt, fl: (0,) * a.ndim)
    grid_spec = pltpu.PrefetchScalarGridSpec(
        num_scalar_prefetch=3,
        grid=(bsz, len(pairs)),
        in_specs=[pl.BlockSpec((1, tq, DA_QK), lambda bi, p, qt, kt, fl: (bi, qt[p], 0)),
                  pl.BlockSpec((1, tk, DA_QK), lambda bi, p, qt, kt, fl: (bi, kt[p], 0)),
                  pl.BlockSpec((1, tk, DA_QK), lambda bi, p, qt, kt, fl: (bi, kt[p], 0)),
                  small(lq1), small(lk1), small(lq2), small(lk2), small(sg)],
        out_specs=pl.BlockSpec((1, tq, DA_QK), lambda bi, p, qt, kt, fl: (bi, qt[p], 0)),
        scratch_shapes=[pltpu.VMEM((DA_HEADS, 2 * tq, LANES), BF16), pltpu.VMEM((DA_HEADS, 2 * tq, 1), F32),
                        pltpu.VMEM((DA_HEADS, 2 * tq, 2 * LANES), F32)],
    )
    return pl.pallas_call(
        functools.partial(_attn_prompt_kernel, tq=tq, tk=tk, lam_init=lam_init),
        grid_spec=grid_spec,
        out_shape=jax.ShapeDtypeStruct((bsz, l, DA_QK), BF16),
        compiler_params=_cparams(("parallel", "arbitrary")),
        name="attn_prompt",
    )(*tabs, q, k, v, lq1, lk1, lq2, lk2, sg)


def _attn_sample_kernel(q_ref, kc_ref, vc_ref, kn_ref, vn_ref, lq1, lk1, lq2, lk2, sg_ref, o_ref,
                        *, tq, past_len, lam_init):
    qs = _split_q(q_ref[0])
    s_c = _dot_nt(qs, kc_ref[0].astype(BF16))
    s_n = _dot_nt(qs, kn_ref[0])
    r = lax.broadcasted_iota(jnp.int32, s_n.shape, 0)
    c = lax.broadcasted_iota(jnp.int32, s_n.shape, 1)
    qrow = jnp.where(r >= tq, r - tq, r)
    s_n = jnp.where(((past_len + c) >> 6) <= ((past_len + qrow) >> 6), s_n, -1e30)
    m = jnp.maximum(jnp.max(s_c, axis=-1, keepdims=True), jnp.max(s_n, axis=-1, keepdims=True))
    p_c = jnp.exp(s_c - m)
    p_n = jnp.exp(s_n - m)
    l = jnp.sum(p_c, axis=-1, keepdims=True) + jnp.sum(p_n, axis=-1, keepdims=True)
    acc = _dot(p_c.astype(BF16), vc_ref[0].astype(BF16)) + _dot(p_n.astype(BF16), vn_ref[0])
    lam = _lambda(lq1, lk1, lq2, lk2, lam_init)
    o_ref[0] = _diff_finish(acc, l, tq, lam, sg_ref[...], lam_init).astype(BF16)


def _attn_sample(q, k, v, k_past, v_past, lq1, lk1, lq2, lk2, sg, lam_init):
    bsz, l, _ = q.shape
    past_len = k_past.shape[1]
    small = lambda a: pl.BlockSpec(a.shape, lambda bi, h: (0,) * a.ndim)
    new = pl.BlockSpec((1, l, LANES), lambda bi, h: (bi, 0, h))
    old = pl.BlockSpec((1, past_len, LANES), lambda bi, h: (bi, 0, h))
    return pl.pallas_call(
        functools.partial(_attn_sample_kernel, tq=l, past_len=past_len, lam_init=lam_init),
        grid=(bsz, DA_HEADS),
        in_specs=[new, old, old, new, new, small(lq1), small(lk1), small(lq2), small(lk2), small(sg)],
        out_specs=new,
        out_shape=jax.ShapeDtypeStruct((bsz, l, DA_QK), BF16),
        compiler_params=_cparams(("parallel", "parallel")),
        name="attn_sample",
    )(q, k_past, v_past, k, v, lq1, lk1, lq2, lk2, sg)


def _out_proj_kernel(x_ref, a_ref, b_ref, w_ref, o_ref):
    o_ref[...] = x_ref[...] + _dot(a_ref[...], w_ref[0:512, :]) + _dot(b_ref[...], w_ref[512:1024, :])


def _out_proj(x, a, b, w):
    n = x.shape[0]
    tm = _row_tile(n)
    row = lambda c: pl.BlockSpec((tm, c), lambda i: (i, 0))
    return pl.pallas_call(
        _out_proj_kernel,
        grid=(n // tm,),
        in_specs=[row(D_MODEL), row(512), row(512), pl.BlockSpec(w.shape, lambda i: (0, 0))],
        out_specs=row(D_MODEL),
        out_shape=jax.ShapeDtypeStruct((n, D_MODEL), F32),
        compiler_params=_cparams(("parallel",)),
        name="out_proj",
    )(x, a, b, w)


def _mem_kv_kernel(m_ref, wk_ref, wv_ref, k32_ref, v32_ref, k16_ref, v16_ref):
    m = m_ref[...].astype(BF16)
    k = _dot(m, wk_ref[0])
    v = _dot(m, wv_ref[0])
    k32_ref[0] = k
    v32_ref[0] = v
    k16_ref[0] = k.astype(BF16)
    v16_ref[0] = v.astype(BF16)


def _mem_kv(mem, wk, wv):
    n = mem.shape[0]
    depth = wk.shape[0]
    tm = _row_tile(n)
    wspec = pl.BlockSpec((1, D_MODEL, D_MODEL), lambda l, i: (l, 0, 0))
    ospec = pl.BlockSpec((1, tm, D_MODEL), lambda l, i: (l, i, 0))
    return pl.pallas_call(
        _mem_kv_kernel,
        grid=(depth, n // tm),
        in_specs=[pl.BlockSpec((tm, D_MODEL), lambda l, i: (i, 0)), wspec, wspec],
        out_specs=[ospec] * 4,
        out_shape=[jax.ShapeDtypeStruct((depth, n, D_MODEL), F32)] * 2
        + [jax.ShapeDtypeStruct((depth, n, D_MODEL), BF16)] * 2,
        compiler_params=_cparams(("parallel", "parallel")),
        name="mem_kv",
    )(mem, wk, wv)


def _cross_kernel(x_ref, g_ref, wq_ref, wo_ref, mk_ref, mv_ref, o_ref):
    x = x_ref[0]
    h = _rms(x, g_ref[...]).astype(BF16)
    q = _dot(h, wq_ref[...]).astype(BF16)
    outs = []
    for hd in range(X_HEADS):
        sl = slice(hd * X_HEAD_DIM, (hd + 1) * X_HEAD_DIM)
        s = _dot_nt(q[:, sl], mk_ref[0, :, sl].astype(BF16)) * (X_HEAD_DIM ** -0.5)
        s = s - jnp.max(s, axis=-1, keepdims=True)
        e = jnp.exp(s)
        p = (e / jnp.sum(e, axis=-1, keepdims=True)).astype(BF16)
        outs.append(_dot(p, mv_ref[0, :, sl].astype(BF16)).astype(BF16))
    o = jnp.concatenate(outs, axis=-1)
    o_ref[0] = x + _dot(o, wo_ref[...])


def _cross(x, g, wq, wo, mk, mv):
    bsz, l, _ = x.shape
    tm = _row_tile(l)
    full = lambda a: pl.BlockSpec(a.shape, lambda bi, i: (0,) * a.ndim)
    xs = pl.BlockSpec((1, tm, D_MODEL), lambda bi, i: (bi, i, 0))
    ms = pl.BlockSpec((1, N_MEM, D_MODEL), lambda bi, i: (bi, 0, 0))
    return pl.pallas_call(
        _cross_kernel,
        grid=(bsz, l // tm),
        in_specs=[xs, full(g), full(wq), full(wo), ms, ms],
        out_specs=xs,
        out_shape=jax.ShapeDtypeStruct((bsz, l, D_MODEL), F32),
        compiler_params=_cparams(("parallel", "parallel")),
        name="cross_attn",
    )(x, g, wq, wo, mk, mv)


PEER_T = 512
PEER_EC = 2048
PEER_SUB = 256
N_CAND = 56


def _top_values(s, k):
    out = []
    work = s
    for _ in range(k):
        m = jnp.max(work, axis=0, keepdims=True)
        out.append(m)
        work = jnp.where(work >= m, NEG_INF, work)
    return out


def _bf16_value(x):
    return x.astype(BF16).astype(F32)


def _bf16_below(x):
    return lax.bitcast_convert_type(lax.bitcast_convert_type(x, jnp.int32) - 0x10000, F32)


def _bf16_pair_word(x):
    bits = lax.bitcast_convert_type(x, jnp.uint32)
    return lax.bitcast_convert_type((bits & jnp.uint32(0xFFFF0000)) | (bits >> 16), jnp.int32)


EXP_FLOOR = -80.0


def _ranked_weights(s, k):
    vals, nums = [], []
    work = s
    placed = jnp.zeros(s.shape, F32)
    for r in range(k):
        m = jnp.max(work, axis=0, keepdims=True)
        hit = work >= m
        v = _bf16_value(jnp.exp(jnp.maximum(m - vals[0], EXP_FLOOR))) if r else jnp.ones_like(m)
        if r:
            v = jnp.minimum(v, _bf16_below(nums[-1]))
        placed = jnp.where(hit, v, placed)
        work = jnp.where(hit, NEG_INF, work)
        vals.append(m)
        nums.append(v)
    return vals, nums, placed


def _peer_route_kernel(x_ref, g_ref, wq_ref, sk_ref, ht_ref, ecut_ref, e0_ref, e1_ref):
    h = _rms(x_ref[...], g_ref[...])
    ht_ref[...] = h.T.astype(BF16)
    q = _dot(h.astype(BF16), wq_ref[...]).astype(BF16)
    t = h.shape[0]
    for hd in range(PEER_HEADS):
        base = hd * 2 * PEER_KEYS
        s0 = _dot_nt(sk_ref[0], q[:, base:base + PEER_KEYS])
        s1 = _dot_nt(sk_ref[1], q[:, base + PEER_KEYS:base + 2 * PEER_KEYS])
        u0 = _top_values(s0, PEER_TOPK + 1)
        u1, n1, e1 = _ranked_weights(s1, PEER_TOPK + 1)
        cands = [u0[a] + u1[b] for a in range(PEER_TOPK + 1) for b in range(PEER_TOPK + 1)
                 if (a + 1) * (b + 1) <= PEER_TOPK + 1]
        cands += [jnp.full((1, t), NEG_INF, F32)] * (N_CAND - len(cands))
        best = _top_values(jnp.concatenate(cands, axis=0), PEER_TOPK + 1)
        thr = 0.5 * (best[PEER_TOPK - 1] + best[PEER_TOPK])
        z = jnp.ones_like(best[0])
        for b in best[1:PEER_TOPK]:
            z = z + jnp.exp(b - best[0])
        cut = thr - s0
        ecut = jnp.full(cut.shape, 2.0, F32)
        for r in range(PEER_TOPK + 1):
            ecut = jnp.where(u1[r] > cut, n1[r], ecut)
        ecut_ref[hd] = _bf16_pair_word(ecut)
        e0_ref[hd] = _bf16_pair_word(_bf16_value(jnp.exp(jnp.maximum(s0 - u0[0], EXP_FLOOR)) / z))
        half = PEER_KEYS // 2
        lo = lax.bitcast_convert_type(e1[0:half], jnp.uint32) >> 16
        hi = lax.bitcast_convert_type(e1[half:PEER_KEYS], jnp.uint32) & jnp.uint32(0xFFFF0000)
        e1_ref[hd] = lax.bitcast_convert_type(hi | lo, jnp.int32)


def _peer_route(x, g, wq, sk):
    n = x.shape[0]
    t = _row_tile(n, PEER_T)
    full = lambda a: pl.BlockSpec(a.shape, lambda i: (0,) * a.ndim)
    rspec = pl.BlockSpec((PEER_HEADS, PEER_KEYS, t), lambda i: (0, 0, i))
    rshape = jax.ShapeDtypeStruct((PEER_HEADS, PEER_KEYS, n), jnp.int32)
    e1spec = pl.BlockSpec((PEER_HEADS, PEER_KEYS // 2, t), lambda i: (0, 0, i))
    e1shape = jax.ShapeDtypeStruct((PEER_HEADS, PEER_KEYS // 2, n), jnp.int32)
    return pl.pallas_call(
        _peer_route_kernel,
        grid=(n // t,),
        in_specs=[pl.BlockSpec((t, D_MODEL), lambda i: (i, 0)), full(g), full(wq), full(sk)],
        out_specs=[pl.BlockSpec((D_MODEL, t), lambda i: (0, i)), rspec, rspec, e1spec],
        out_shape=[jax.ShapeDtypeStruct((D_MODEL, n), BF16), rshape, rshape, e1shape],
        compiler_params=_cparams(("parallel",)),
        name="peer_route",
    )(x, g, wq, sk)


def _peer_dense_kernel(ht_ref, ecut_ref, e0_ref, e1_ref, eu_ref, evt_ref, x_ref, o_ref, acc_scr):
    j = pl.program_id(1)
    rows_step = PEER_EC // PEER_KEYS
    rows_sub = PEER_SUB // PEER_KEYS

    @pl.when(j == 0)
    def _():
        acc_scr[...] = jnp.zeros(acc_scr.shape, F32)

    t = ht_ref.shape[1]

    def packed_row(row, cols):
        return pltpu.bitcast(jnp.broadcast_to(row[:, cols], (PEER_KEYS // 2, LANES)), BF16)

    total = None
    for sc in range(PEER_EC // PEER_SUB):
        at = _dot(eu_ref[sc * PEER_SUB:(sc + 1) * PEER_SUB, :], ht_ref[...])
        act = jax.nn.gelu(at.astype(BF16))
        ps = []
        for ii in range(rows_sub):
            i0 = j * rows_step + sc * rows_sub + ii
            ecut_rows = [ecut_ref[hd, pl.ds(i0, 1), :] for hd in range(PEER_HEADS)]
            e0_rows = [e0_ref[hd, pl.ds(i0, 1), :] for hd in range(PEER_HEADS)]
            blocks = []
            for tc in range(t // LANES):
                cols = slice(tc * LANES, (tc + 1) * LANES)
                w = None
                for hd in range(PEER_HEADS):
                    e1 = pltpu.bitcast(e1_ref[hd, :, cols], BF16)
                    term = jnp.where(e1 >= packed_row(ecut_rows[hd], cols), e1, jnp.zeros_like(e1))
                    term = term * packed_row(e0_rows[hd], cols)
                    w = term if w is None else w + term
                blocks.append(w * act[ii * PEER_KEYS:(ii + 1) * PEER_KEYS, cols])
            ps.append(jnp.concatenate(blocks, axis=1))
        part = _dot(evt_ref[:, sc * PEER_SUB:(sc + 1) * PEER_SUB], jnp.concatenate(ps, axis=0))
        total = part if total is None else total + part
    acc_scr[...] += total

    @pl.when(j == pl.num_programs(1) - 1)
    def _():
        o_ref[...] = x_ref[...] + acc_scr[...].T


def _peer_dense(x, ht, ecut, e0, e1, eu, evt):
    n = x.shape[0]
    t = _row_tile(n, PEER_T)
    rspec = pl.BlockSpec((PEER_HEADS, PEER_KEYS, t), lambda i, j: (0, 0, i))
    e1spec = pl.BlockSpec((PEER_HEADS, PEER_KEYS // 2, t), lambda i, j: (0, 0, i))
    xspec = pl.BlockSpec((t, D_MODEL), lambda i, j: (i, 0))
    return pl.pallas_call(
        _peer_dense_kernel,
        grid=(n // t, PEER_EXPERTS // PEER_EC),
        in_specs=[pl.BlockSpec((D_MODEL, t), lambda i, j: (0, i)), rspec, rspec, e1spec,
                  pl.BlockSpec((PEER_EC, D_MODEL), lambda i, j: (j, 0)),
                  pl.BlockSpec((D_MODEL, PEER_EC), lambda i, j: (0, j)), xspec],
        out_specs=xspec,
        out_shape=jax.ShapeDtypeStruct((n, D_MODEL), F32),
        scratch_shapes=[pltpu.VMEM((D_MODEL, t), F32)],
        compiler_params=_cparams(("parallel", "arbitrary")),
        name="peer_dense",
    )(ht, ecut, e0, e1, eu, evt, x)


def _cd_in_kernel(x_ref, g_ref, wc_ref, wg_ref, wx_ref, wd_ref, lg_ref, lb_ref,
                  u_ref, v_ref, gate_ref, xbc_ref, dt_ref):
    h = _rms(x_ref[...], g_ref[...]).astype(BF16)
    u_ref[...] = jax.nn.gelu(_dot(h, wc_ref[:, 0:512]))
    v_ref[...] = _layernorm(jax.nn.gelu(_dot(h, wc_ref[:, 512:1024])), lg_ref[...], lb_ref[...])
    gate_ref[...] = _dot(h, wg_ref[...])
    xbc_ref[...] = _dot(h, wx_ref[...])
    dt_ref[...] = _dot(h, wd_ref[...])


def _cd_in(x, g, wc, wg, wx, wd, lg, lb):
    n = x.shape[0]
    tm = _row_tile(n)
    row = lambda c: pl.BlockSpec((tm, c), lambda i: (i, 0))
    full = lambda a: pl.BlockSpec(a.shape, lambda i: (0,) * a.ndim)
    widths = (512, 512, 512, SSM_CONV_CH, LANES)
    return pl.pallas_call(
        _cd_in_kernel,
        grid=(n // tm,),
        in_specs=[row(D_MODEL), full(g), full(wc), full(wg), full(wx), full(wd), full(lg), full(lb)],
        out_specs=[row(c) for c in widths],
        out_shape=[jax.ShapeDtypeStruct((n, c), F32) for c in widths],
        compiler_params=_cparams(("parallel",)),
        name="cd_in",
    )(x, g, wc, wg, wx, wd, lg, lb)


def _gmlp_kernel(u_ref, v_ref, ws_ref, bs_ref, o_ref, *, gm_len, n_chunks):
    r = lax.broadcasted_iota(jnp.int32, (gm_len, gm_len), 0)
    c = lax.broadcasted_iota(jnp.int32, (gm_len, gm_len), 1)
    for g in range(GM_GROUPS):
        w = jnp.where(r >= c, ws_ref[g], 0.0).astype(BF16)
        ch = slice(g * LANES, (g + 1) * LANES)
        for ci in range(n_chunks):
            rows = slice(ci * gm_len, (ci + 1) * gm_len)
            mixed = _dot(w, v_ref[0, rows, ch].astype(BF16)) + bs_ref[:, g:g + 1]
            o_ref[0, rows, ch] = (u_ref[0, rows, ch] * mixed).astype(BF16)


def _gmlp(u, v, ws, bs_t, gm_len):
    bsz, l, _ = u.shape
    tl = _row_tile(l)
    full = lambda a: pl.BlockSpec(a.shape, lambda bi, i: (0,) * a.ndim)
    spec = pl.BlockSpec((1, tl, GM_WIDTH), lambda bi, i: (bi, i, 0))
    return pl.pallas_call(
        functools.partial(_gmlp_kernel, gm_len=gm_len, n_chunks=tl // gm_len),
        grid=(bsz, l // tl),
        in_specs=[spec, spec, full(ws), full(bs_t)],
        out_specs=spec,
        out_shape=jax.ShapeDtypeStruct((bsz, l, GM_WIDTH), BF16),
        compiler_params=_cparams(("parallel", "parallel")),
        name="gmlp",
    )(u, v, ws, bs_t)


SSD_HALO = 8
SSD_PAIRS = SSM_HEADS // 2
SSD_GROUP_W = SSM_INNER // 2


def _ssd_kernel(xbc_ref, halo_ref, ctx_ref, gate_ref, dt_ref, h0_ref, cw_ref, cb_ref, dtb_ref, alog_ref,
                dsk_ref, ng_ref, y_ref, fin_ref, xp_scr, st_scr, *, tl):
    i = pl.program_id(1)
    pad = SSD_HALO - (SSM_CONV_K - 1)
    q = CHUNK

    @pl.when(i == 0)
    def _():
        xp_scr[pad:SSD_HALO, :] = ctx_ref[0]
        for k in range(SSD_PAIRS):
            st_scr[k] = jnp.concatenate([h0_ref[0, 2 * k], h0_ref[0, 2 * k + 1]], axis=0).T

    @pl.when(i > 0)
    def _():
        xp_scr[0:SSD_HALO, :] = halo_ref[0]

    xp_scr[SSD_HALO:SSD_HALO + tl, :] = xbc_ref[0]

    lane128 = lax.broadcasted_iota(jnp.int32, (q, LANES), 1)
    row128 = lax.broadcasted_iota(jnp.int32, (q, LANES), 0)
    er = lax.broadcasted_iota(jnp.int32, (LANES, SSM_INNER), 0)
    ec = lax.broadcasted_iota(jnp.int32, (LANES, SSM_INNER), 1)
    expand = jnp.where((ec >> 6) == er, 1.0, 0.0).astype(F32)
    tr = lax.broadcasted_iota(jnp.int32, (q, q), 0)
    tc = lax.broadcasted_iota(jnp.int32, (q, q), 1)
    ltri = jnp.where(tr >= tc, 1.0, 0.0).astype(F32)
    a_neg = -jnp.exp(alog_ref[...])
    zeros_q = jnp.zeros((q, LANES), F32)

    for ci in range(tl // q):
        r0 = ci * q
        acc = jnp.broadcast_to(cb_ref[...], (q, SSM_CONV_CH))
        for k in range(SSM_CONV_K):
            acc = acc + cw_ref[k:k + 1, :] * xp_scr[r0 + pad + k:r0 + pad + k + q, :]
        xc = _silu(acc)
        xs = xc[:, 0:SSM_INNER]
        dt = jnp.where(lane128 < SSM_HEADS, _softplus(dt_ref[0, r0:r0 + q, :] + dtb_ref[...]), 0.0)
        acs = _dot_f32(ltri, dt * a_neg)
        acs_e = _dot_f32(acs, expand)
        dt_e = _dot_f32(dt, expand)
        tot_e = acs_e[q - 1:q, :]
        xdt = xs * dt_e
        xd = (xdt * jnp.exp(tot_e - acs_e)).astype(BF16)
        eacs = jnp.exp(acs_e)
        cdec = jnp.exp(tot_e)
        ys = []
        for k in range(SSD_PAIRS):
            g = k // 2
            blk = slice(k * LANES, (k + 1) * LANES)
            bm = xc[:, SSM_INNER + g * SSM_STATE:SSM_INNER + (g + 1) * SSM_STATE]
            cm = xc[:, SSM_INNER + 2 * SSM_STATE + g * SSM_STATE:
                    SSM_INNER + 2 * SSM_STATE + (g + 1) * SSM_STATE].astype(BF16)
            cb2 = _dot_nt(cm, jnp.concatenate([bm, bm], axis=0).astype(BF16))
            a_blk = acs_e[:, blk]
            a_row = jnp.sum(jnp.where(row128 == (lane128 & (q - 1)), a_blk, 0.0), axis=0, keepdims=True)
            lmat = jnp.exp(jnp.where((lane128 & (q - 1)) <= row128, a_blk - a_row, NEG_INF))
            sc = (cb2 * lmat).astype(BF16)
            x_blk = xdt[:, blk]
            rhs = jnp.concatenate([jnp.where(lane128 < q, x_blk, 0.0), jnp.where(lane128 >= q, x_blk, 0.0)],
                                  axis=0).astype(BF16)
            prev = st_scr[k]
            y_pair = _dot(sc, rhs) + _dot(cm, prev.astype(BF16)) * eacs[:, blk]
            ys.append(y_pair)
            bt = jnp.concatenate([bm, zeros_q], axis=0).T.astype(BF16)
            xd_pad = jnp.concatenate([xd[:, blk], zeros_q.astype(BF16)], axis=0)
            st_scr[k] = prev * cdec[:, blk] + _dot(bt, xd_pad)
        y = jnp.concatenate(ys, axis=-1) + dsk_ref[...] * xs
        y = y * _silu(gate_ref[0, r0:r0 + q, :])
        outs = []
        for g in range(2):
            seg = y[:, g * SSD_GROUP_W:(g + 1) * SSD_GROUP_W]
            outs.append(seg * lax.rsqrt(jnp.mean(seg * seg, axis=-1, keepdims=True) + 1e-6))
        y_ref[0, r0:r0 + q, :] = (jnp.concatenate(outs, axis=-1) * ng_ref[...]).astype(BF16)

    @pl.when(i == pl.num_programs(1) - 1)
    def _():
        for k in range(SSD_PAIRS):
            st = st_scr[k].T
            fin_ref[0, 2 * k] = st[0:SSM_HEAD_DIM]
            fin_ref[0, 2 * k + 1] = st[SSM_HEAD_DIM:2 * SSM_HEAD_DIM]


def _ssd(xbc, ctx, gate, dt, h0, cw, cb, dtb, alog, dsk, ng):
    bsz, l, _ = xbc.shape
    tl = 256 if l % 256 == 0 else l
    hb = tl // SSD_HALO
    full = lambda a: pl.BlockSpec(a.shape, lambda bi, i: (0,) * a.ndim)
    tile = lambda c: pl.BlockSpec((1, tl, c), lambda bi, i: (bi, i, 0))
    stspec = pl.BlockSpec((1, SSM_HEADS, SSM_HEAD_DIM, SSM_STATE), lambda bi, i: (bi, 0, 0, 0))
    return pl.pallas_call(
        functools.partial(_ssd_kernel, tl=tl),
        grid=(bsz, l // tl),
        in_specs=[tile(SSM_CONV_CH),
                  pl.BlockSpec((1, SSD_HALO, SSM_CONV_CH), lambda bi, i: (bi, jnp.maximum(i * hb - 1, 0), 0)),
                  pl.BlockSpec((1, SSM_CONV_K - 1, SSM_CONV_CH), lambda bi, i: (bi, 0, 0)),
                  tile(SSM_INNER), tile(LANES), stspec,
                  full(cw), full(cb), full(dtb), full(alog), full(dsk), full(ng)],
        out_specs=[tile(SSM_INNER), stspec],
        out_shape=[jax.ShapeDtypeStruct((bsz, l, SSM_INNER), BF16),
                   jax.ShapeDtypeStruct((bsz, SSM_HEADS, SSM_HEAD_DIM, SSM_STATE), F32)],
        scratch_shapes=[pltpu.VMEM((SSD_HALO + tl, SSM_CONV_CH), F32),
                        pltpu.VMEM((SSD_PAIRS, SSM_STATE, LANES), F32)],
        compiler_params=_cparams(("parallel", "arbitrary")),
        name="ssd",
    )(xbc, xbc, ctx, gate, dt, h0, cw, cb, dtb, alog, dsk, ng)


def _final_rms_kernel(x_ref, g_ref, o_ref):
    o_ref[...] = _rms(x_ref[...], g_ref[...])


def _final_rms(x, g):
    n = x.shape[0]
    tm = _row_tile(n)
    row = pl.BlockSpec((tm, D_MODEL), lambda i: (i, 0))
    return pl.pallas_call(
        _final_rms_kernel,
        grid=(n // tm,),
        in_specs=[row, pl.BlockSpec(g.shape, lambda i: (0, 0))],
        out_specs=row,
        out_shape=jax.ShapeDtypeStruct((n, D_MODEL), F32),
        compiler_params=_cparams(("parallel",)),
        name="final_rms",
    )(x, g)


def _interleave_experts(table):
    depth, _, d = table.shape
    half = PEER_KEYS // 2
    return table.reshape(depth, PEER_KEYS, 2, half, d).swapaxes(2, 3).reshape(depth, PEER_EXPERTS, d)


def _pad_lanes(a, width=LANES):
    return jnp.pad(a, ((0, 0), (0, width - a.shape[-1])))


def _trunk(x, mem_k, mem_v, attn_k, attn_v, conv_a, ssd_st, conv_ssm, gm_len, p):
    bsz, l, _ = x.shape
    n = bsz * l
    row = lambda a: a.reshape(1, -1)
    x2 = x.reshape(n, D_MODEL)

    lam_init = 0.8 - 0.6 * math.exp(-0.3 * 0)
    glu, q16, k32, v32, k16, v16 = _ab_in(x2, row(p["norm_mix_g"][0]), p["w_in_ab"])
    glu3 = glu.reshape(bsz, l, CONV_CH)
    ctx_a = jnp.zeros((bsz, CONV_K - 1, CONV_CH), F32) if conv_a is None else conv_a[0]
    ca = _conv_a(glu3, ctx_a, p["conv_a_w"], row(p["conv_a_b"]), row(p["ln_a_g"]), row(p["ln_a_b"]))
    lam_args = (row(p["lam_q1"]), row(p["lam_k1"]), row(p["lam_q2"]), row(p["lam_k2"]), row(p["subln_g"]))
    shp3 = lambda a: a.reshape(bsz, l, DA_QK)
    if attn_k is None:
        o = _attn_prompt(shp3(q16), shp3(k16), shp3(v16), *lam_args, lam_init)
    else:
        past = attn_k.shape[2]
        o = _attn_sample(shp3(q16), shp3(k16), shp3(v16), attn_k[0].reshape(bsz, past, DA_QK),
                         attn_v[0].reshape(bsz, past, DA_QK), *lam_args, lam_init)
    x2 = _out_proj(x2, ca.reshape(n, CONV_CH), o.reshape(n, DA_QK), p["w_out_ab"])
    new_k = k32.reshape(1, bsz, l, DA_HEADS, 2 * DA_HEAD_DIM)
    new_v = v32.reshape(1, bsz, l, DA_HEADS, 2 * DA_HEAD_DIM)
    new_ca = glu3[:, l - (CONV_K - 1):][None]

    def tail(x2, layer):
        x3 = _cross(x2.reshape(bsz, l, D_MODEL), row(p["norm_cross_g"][layer]), p["w_xq"][layer],
                    p["w_xo"][layer], mem_k[layer], mem_v[layer]).reshape(n, D_MODEL)
        routed = _peer_route(x3, row(p["norm_ffn_g"][layer]), p["w_pq"][layer], p["sub_keys"][layer])
        return _peer_dense(x3, *routed, p["expert_u"][layer], p["expert_vt"][layer])

    x2 = tail(x2, 0)

    u, vln, gate, xbc, dt = _cd_in(x2, row(p["norm_mix_g"][1]), p["w_cd_c"], p["w_cd_gate"], p["w_cd_xbc"],
                                   p["w_cd_dt"], row(p["ln_c_g"]), row(p["ln_c_b"]))
    shp = lambda a: a.reshape(bsz, l, a.shape[-1])
    c_out = _gmlp(shp(u), shp(vln), p["gm_w_s"][:, :gm_len, :gm_len], p["gm_b_s"][:, :gm_len].T, gm_len)
    xbc3 = shp(xbc)
    ctx_d = jnp.zeros((bsz, SSM_CONV_K - 1, SSM_CONV_CH), F32) if conv_ssm is None else conv_ssm[0]
    h0 = jnp.zeros((bsz, SSM_HEADS, SSM_HEAD_DIM, SSM_STATE), F32) if ssd_st is None else ssd_st[0]
    y, fin = _ssd(xbc3, ctx_d, shp(gate), shp(dt), h0, p["conv_d_w"], row(p["conv_d_b"]),
                  _pad_lanes(row(p["dt_bias"])), _pad_lanes(row(p["a_log"])),
                  row(jnp.repeat(p["d_skip"], SSM_HEAD_DIM)), row(p["norm_d_g"]))
    x2 = _out_proj(x2, c_out.reshape(n, GM_WIDTH), y.reshape(n, SSM_INNER), p["w_out_cd"])
    x2 = tail(x2, 1)
    y_out = _final_rms(x2, row(p["norm_final_g"])).reshape(bsz, l, D_MODEL)
    new_gv = vln.reshape(1, bsz, l, GM_GROUPS, GM_WIDTH // GM_GROUPS)
    new_cs = xbc3[:, l - (SSM_CONV_K - 1):][None]
    return y_out, new_k, new_v, new_ca, new_gv, fin[None], new_cs


def kernel(x_prompt, x_sample, cache_attn_k, cache_attn_v, state_conv_a, state_ssd, state_conv_ssm, cache_mem_k, cache_mem_v, mem_prompt, norm_mix_g, norm_cross_g, norm_ffn_g, norm_final_g, w_in_ab, conv_a_w, conv_a_b, ln_a_g, ln_a_b, lam_q1, lam_k1, lam_q2, lam_k2, subln_g, w_out_ab, w_in_cd, ln_c_g, ln_c_b, gm_w_s, gm_b_s, conv_d_w, conv_d_b, dt_bias, a_log, d_skip, norm_d_g, w_out_cd, w_xq, w_xk, w_xv, w_xo, w_pq, sub_keys, expert_u, expert_v):
    bf = lambda a: a.astype(BF16)
    w_cd = w_in_cd[0]
    p = {
        "norm_mix_g": norm_mix_g, "norm_cross_g": norm_cross_g, "norm_ffn_g": norm_ffn_g,
        "norm_final_g": norm_final_g,
        "w_in_ab": bf(w_in_ab[0]), "conv_a_w": conv_a_w[0], "conv_a_b": conv_a_b[0],
        "ln_a_g": ln_a_g[0], "ln_a_b": ln_a_b[0],
        "lam_q1": lam_q1[0], "lam_k1": lam_k1[0], "lam_q2": lam_q2[0], "lam_k2": lam_k2[0],
        "subln_g": subln_g[0], "w_out_ab": bf(w_out_ab[0]),
        "w_cd_c": bf(w_cd[:, 0:1024]), "w_cd_gate": bf(w_cd[:, 1024:1536]), "w_cd_xbc": bf(w_cd[:, 1536:2560]),
        "w_cd_dt": bf(_pad_lanes(w_cd[:, 2560:2568])),
        "ln_c_g": ln_c_g[0], "ln_c_b": ln_c_b[0], "gm_w_s": gm_w_s[0], "gm_b_s": gm_b_s[0],
        "conv_d_w": conv_d_w[0], "conv_d_b": conv_d_b[0], "dt_bias": dt_bias[0], "a_log": a_log[0],
        "d_skip": d_skip[0], "norm_d_g": norm_d_g[0], "w_out_cd": bf(w_out_cd[0]),
        "w_xq": bf(w_xq), "w_xo": bf(w_xo), "w_pq": bf(w_pq), "sub_keys": bf(sub_keys),
        "expert_u": _interleave_experts(bf(expert_u)),
        "expert_vt": jnp.swapaxes(_interleave_experts(bf(expert_v)), 1, 2),
    }
    bsz, seq, _ = x_prompt.shape
    dec_b, dec_l, _ = x_sample.shape
    depth = w_xk.shape[0]

    mk32, mv32, mk16, mv16 = _mem_kv(mem_prompt.reshape(bsz * N_MEM, D_MODEL), bf(w_xk), bf(w_xv))
    mem_k_p = mk32.reshape(depth, bsz, N_MEM, X_HEADS, X_HEAD_DIM)
    mem_v_p = mv32.reshape(depth, bsz, N_MEM, X_HEADS, X_HEAD_DIM)
    y_prompt, kp, vp, cap, _, ssdp, csp = _trunk(
        x_prompt, mk16.reshape(depth, bsz, N_MEM, D_MODEL), mv16.reshape(depth, bsz, N_MEM, D_MODEL),
        None, None, None, None, None, 2 * CHUNK, p)

    y_sample, ks, vs, cas, gvs, ssds, css = _trunk(
        x_sample, cache_mem_k.reshape(depth, dec_b, N_MEM, D_MODEL), cache_mem_v.reshape(depth, dec_b, N_MEM, D_MODEL),
        cache_attn_k, cache_attn_v, state_conv_a, state_ssd, state_conv_ssm, dec_l, p)
    return (y_prompt, y_sample, kp, vp, cap, ssdp, csp, mem_k_p, mem_v_p, ks, vs, cas, gvs, ssds, css)
```

```python
import functools
import math

import jax
import jax.numpy as jnp
from jax import lax
from jax.experimental import pallas as pl
from jax.experimental.pallas import tpu as pltpu

F32 = jnp.float32
BF16 = jnp.bfloat16
NEG_INF = float("-inf")

D_MODEL = 1024
CHUNK = 64
CONV_CH = 512
CONV_K = 31
DA_HEADS = 4
DA_HEAD_DIM = 64
DA_QK = 512
GM_WIDTH = 512
GM_GROUPS = 4
SSM_INNER = 512
SSM_HEADS = 8
SSM_HEAD_DIM = 64
SSM_STATE = 128
SSM_CONV_K = 4
SSM_CONV_CH = 1024
N_MEM = 256
X_HEADS = 4
X_HEAD_DIM = 256
PEER_HEADS = 8
PEER_KEYS = 128
PEER_EXPERTS = PEER_KEYS * PEER_KEYS
PEER_TOPK = 16
LANES = 128
SUBLANES = 8
BF16_ROWS = 16
VMEM_LIMIT = 56 * 1024 * 1024


def _cparams(sem):
    return pltpu.CompilerParams(dimension_semantics=sem, vmem_limit_bytes=VMEM_LIMIT)


def _dot(a, b):
    return jnp.dot(a, b, preferred_element_type=F32)


def _dot_nt(a, b):
    return lax.dot_general(a, b, (((1,), (1,)), ((), ())), preferred_element_type=F32)


def _dot_f32(a, b):
    return jnp.dot(a, b, preferred_element_type=F32, precision=lax.Precision.HIGHEST)


def _rms(x, g, eps=1e-6):
    return x * lax.rsqrt(jnp.mean(x * x, axis=-1, keepdims=True) + eps) * g


def _layernorm(x, g, b, eps=1e-5):
    xc = x - jnp.mean(x, axis=-1, keepdims=True)
    return xc * lax.rsqrt(jnp.mean(xc * xc, axis=-1, keepdims=True) + eps) * g + b


def _silu(x):
    return x * jax.nn.sigmoid(x)


def _softplus(x):
    return jnp.maximum(x, 0.0) + jnp.log(1.0 + jnp.exp(-jnp.abs(x)))


def _row_tile(n, pref=512):
    return pref if n % pref == 0 else n


def _ab_in_kernel(x_ref, g_ref, w_ref, glu_ref, q_ref, k32_ref, v32_ref, k16_ref, v16_ref):
    h = _rms(x_ref[...], g_ref[...]).astype(BF16)
    a_val = _dot(h, w_ref[:, 0:512])
    a_gate = _dot(h, w_ref[:, 512:1024])
    glu_ref[...] = a_val * jax.nn.sigmoid(a_gate)
    q = _dot(h, w_ref[:, 1024:1536])
    q_ref[...] = (q * (DA_HEAD_DIM ** -0.5 * math.log2(math.e))).astype(BF16)
    k = _dot(h, w_ref[:, 1536:2048])
    k32_ref[...] = k
    k16_ref[...] = k.astype(BF16)
    v = _dot(h, w_ref[:, 2048:2560])
    v32_ref[...] = v
    v16_ref[...] = v.astype(BF16)


def _ab_in(x, g, w):
    n = x.shape[0]
    tm = _row_tile(n)
    row = lambda c: pl.BlockSpec((tm, c), lambda i: (i, 0))
    full = lambda a: pl.BlockSpec(a.shape, lambda i: (0,) * a.ndim)
    return pl.pallas_call(
        _ab_in_kernel,
        grid=(n // tm,),
        in_specs=[row(D_MODEL), full(g), full(w)],
        out_specs=[row(512)] * 6,
        out_shape=[jax.ShapeDtypeStruct((n, 512), F32), jax.ShapeDtypeStruct((n, 512), BF16),
                   jax.ShapeDtypeStruct((n, 512), F32), jax.ShapeDtypeStruct((n, 512), F32),
                   jax.ShapeDtypeStruct((n, 512), BF16), jax.ShapeDtypeStruct((n, 512), BF16)],
        compiler_params=_cparams(("parallel",)),
        name="ab_in",
    )(x, g, w)


CONV_HALO = 32
CONV_RB = 64


def _conv_a_kernel(glu_ref, halo_ref, ctx_ref, w_ref, b_ref, lg_ref, lb_ref, o_ref, xp_scr, *, tl):
    i = pl.program_id(1)
    pad = CONV_HALO - (CONV_K - 1)

    @pl.when(i == 0)
    def _():
        xp_scr[pad:CONV_HALO, :] = ctx_ref[0]

    @pl.when(i > 0)
    def _():
        xp_scr[0:CONV_HALO, :] = halo_ref[0]

    xp_scr[CONV_HALO:CONV_HALO + tl, :] = glu_ref[0]
    rb = min(CONV_RB, tl)
    for r0 in range(0, tl, rb):
        acc = jnp.broadcast_to(b_ref[...], (rb, CONV_CH))
        for k in range(CONV_K):
            acc = acc + w_ref[k:k + 1, :] * xp_scr[r0 + pad + k:r0 + pad + k + rb, :]
        y = _layernorm(acc, lg_ref[...], lb_ref[...])
        o_ref[0, r0:r0 + rb, :] = _silu(y).astype(BF16)


def _conv_a(glu, ctx, w, b, lg, lb):
    bsz, l, _ = glu.shape
    tl = _row_tile(l)
    hb = tl // CONV_HALO
    full = lambda a: pl.BlockSpec(a.shape, lambda bi, i: (0,) * a.ndim)
    return pl.pallas_call(
        functools.partial(_conv_a_kernel, tl=tl),
        grid=(bsz, l // tl),
        in_specs=[pl.BlockSpec((1, tl, CONV_CH), lambda bi, i: (bi, i, 0)),
                  pl.BlockSpec((1, CONV_HALO, CONV_CH), lambda bi, i: (bi, jnp.maximum(i * hb - 1, 0), 0)),
                  pl.BlockSpec((1, CONV_K - 1, CONV_CH), lambda bi, i: (bi, 0, 0)),
                  full(w), full(b), full(lg), full(lb)],
        out_specs=pl.BlockSpec((1, tl, CONV_CH), lambda bi, i: (bi, i, 0)),
        out_shape=jax.ShapeDtypeStruct((bsz, l, CONV_CH), BF16),
        scratch_shapes=[pltpu.VMEM((CONV_HALO + tl, CONV_CH), F32)],
        compiler_params=_cparams(("parallel", "arbitrary")),
        name="conv_a",
    )(glu, glu, ctx, w, b, lg, lb)


def _lambda(lq1, lk1, lq2, lk2, lam_init):
    return (jnp.exp(jnp.sum(lq1[...] * lk1[...], axis=-1, keepdims=True))
            - jnp.exp(jnp.sum(lq2[...] * lk2[...], axis=-1, keepdims=True)) + lam_init)


def _split_q(q):
    lane = lax.broadcasted_iota(jnp.int32, q.shape, 1)
    z = jnp.zeros_like(q)
    return jnp.concatenate([jnp.where(lane < DA_HEAD_DIM, q, z), jnp.where(lane >= DA_HEAD_DIM, q, z)], axis=0)


def _diff_finish(acc, l, tq, lam, sg, lam_init):
    o = acc[0:tq] / l[0:tq] - lam * (acc[tq:2 * tq] / l[tq:2 * tq])
    return _rms(o, sg) * (1.0 - lam_init)


ATTN_TQ = 512
ATTN_TK = 512


def _attn_prompt_kernel(qi_ref, ki_ref, fl_ref, q_ref, k_ref, v_ref, lq1, lk1, lq2, lk2, sg_ref, o_ref,
                        qs_scr, m_scr, acc_scr, *, tq, tk, lam_init):
    p = pl.program_id(1)
    qi = qi_ref[p]
    ki = ki_ref[p]
    flags = fl_ref[p]

    @pl.when(ki == 0)
    def _():
        for h in range(DA_HEADS):
            qs_scr[h] = _split_q(q_ref[0, :, h * LANES:(h + 1) * LANES])
        m_scr[...] = jnp.full(m_scr.shape, NEG_INF, F32)
        acc_scr[...] = jnp.zeros(acc_scr.shape, F32)

    ones_col = jnp.where(lax.broadcasted_iota(jnp.int32, (tk, LANES), 1) == 0, 1.0, 0.0).astype(BF16)

    def step(masked):
        for h in range(DA_HEADS):
            sl = slice(h * LANES, (h + 1) * LANES)
            s = _dot_nt(qs_scr[h], k_ref[0, :, sl])
            if masked:
                r = lax.broadcasted_iota(jnp.int32, s.shape, 0)
                c = lax.broadcasted_iota(jnp.int32, s.shape, 1)
                q_pos = qi * tq + jnp.where(r >= tq, r - tq, r)
                s = jnp.where(((ki * tk + c) >> 6) <= (q_pos >> 6), s, -1e30)
            m_old = m_scr[h]
            m_new = jnp.maximum(m_old, jnp.broadcast_to(jnp.max(s, axis=-1, keepdims=True), m_old.shape))
            alpha = jnp.exp2(m_old - m_new)
            pr = jnp.exp2(s - jnp.concatenate([m_new] * (tk // LANES), axis=1)).astype(BF16)
            v_aug = jnp.concatenate([v_ref[0, :, sl], ones_col], axis=1)
            acc_scr[h] = jnp.concatenate([alpha, alpha], axis=1) * acc_scr[h] + _dot(pr, v_aug)
            m_scr[h] = m_new

    @pl.when((flags & 1) == 0)
    def _():
        step(False)

    @pl.when((flags & 1) == 1)
    def _():
        step(True)

    @pl.when((flags & 2) != 0)
    def _():
        lam = _lambda(lq1, lk1, lq2, lk2, lam_init)
        for h in range(DA_HEADS):
            acc = acc_scr[h]
            o = _diff_finish(acc[:, 0:LANES], acc[:, LANES:LANES + 1], tq, lam, sg_ref[...], lam_init)
            o_ref[0, :, h * LANES:(h + 1) * LANES] = o.astype(BF16)


def _attn_prompt(q, k, v, lq1, lk1, lq2, lk2, sg, lam_init):
    bsz, l, _ = q.shape
    tq = _row_tile(l, ATTN_TQ)
    tk = _row_tile(l, ATTN_TK)
    pairs = []
    for a in range(l // tq):
        last = ((a + 1) * tq - 1) // tk
        for b in range(last + 1):
            crosses = (b + 1) * tk > a * tq + CHUNK
            pairs.append((a, b, int(crosses) + 2 * int(b == last)))
    tabs = [jnp.asarray([pr[i] for pr in pairs], jnp.int32) for i in range(3)]
    small = lambda a: pl.BlockSpec(a.shape, lambda bi, p, qt, kt, fl: (0,) * a.ndim)
    grid_spec = pltpu.PrefetchScalarGridSpec(
        num_scalar_prefetch=3,
        grid=(bsz, len(pairs)),
        in_specs=[pl.BlockSpec((1, tq, DA_QK), lambda bi, p, qt, kt, fl: (bi, qt[p], 0)),
                  pl.BlockSpec((1, tk, DA_QK), lambda bi, p, qt, kt, fl: (bi, kt[p], 0)),
                  pl.BlockSpec((1, tk, DA_QK), lambda bi, p, qt, kt, fl: (bi, kt[p], 0)),
                  small(lq1), small(lk1), small(lq2), small(lk2), small(sg)],
        out_specs=pl.BlockSpec((1, tq, DA_QK), lambda bi, p, qt, kt, fl: (bi, qt[p], 0)),
        scratch_shapes=[pltpu.VMEM((DA_HEADS, 2 * tq, LANES), BF16), pltpu.VMEM((DA_HEADS, 2 * tq, LANES), F32),
                        pltpu.VMEM((DA_HEADS, 2 * tq, 2 * LANES), F32)],
    )
    return pl.pallas_call(
        functools.partial(_attn_prompt_kernel, tq=tq, tk=tk, lam_init=lam_init),
        grid_spec=grid_spec,
        out_shape=jax.ShapeDtypeStruct((bsz, l, DA_QK), BF16),
        compiler_params=_cparams(("parallel", "arbitrary")),
        name="attn_prompt",
    )(*tabs, q, k, v, lq1, lk1, lq2, lk2, sg)


def _attn_sample_kernel(q_ref, kc_ref, vc_ref, kn_ref, vn_ref, lq1, lk1, lq2, lk2, sg_ref, o_ref,
                        *, tq, past_len, lam_init):
    qs = _split_q(q_ref[0])
    s_c = _dot_nt(qs, kc_ref[0].astype(BF16))
    s_n = _dot_nt(qs, kn_ref[0])
    r = lax.broadcasted_iota(jnp.int32, s_n.shape, 0)
    c = lax.broadcasted_iota(jnp.int32, s_n.shape, 1)
    qrow = jnp.where(r >= tq, r - tq, r)
    s_n = jnp.where(((past_len + c) >> 6) <= ((past_len + qrow) >> 6), s_n, -1e30)
    m = jnp.maximum(jnp.max(s_c, axis=-1, keepdims=True), jnp.max(s_n, axis=-1, keepdims=True))
    p_c = jnp.exp2(s_c - m)
    p_n = jnp.exp2(s_n - m)
    l = jnp.sum(p_c, axis=-1, keepdims=True) + jnp.sum(p_n, axis=-1, keepdims=True)
    acc = _dot(p_c.astype(BF16), vc_ref[0].astype(BF16)) + _dot(p_n.astype(BF16), vn_ref[0])
    lam = _lambda(lq1, lk1, lq2, lk2, lam_init)
    o_ref[0] = _diff_finish(acc, l, tq, lam, sg_ref[...], lam_init).astype(BF16)


def _attn_sample(q, k, v, k_past, v_past, lq1, lk1, lq2, lk2, sg, lam_init):
    bsz, l, _ = q.shape
    past_len = k_past.shape[1]
    small = lambda a: pl.BlockSpec(a.shape, lambda bi, h: (0,) * a.ndim)
    new = pl.BlockSpec((1, l, LANES), lambda bi, h: (bi, 0, h))
    old = pl.BlockSpec((1, past_len, LANES), lambda bi, h: (bi, 0, h))
    return pl.pallas_call(
        functools.partial(_attn_sample_kernel, tq=l, past_len=past_len, lam_init=lam_init),
        grid=(bsz, DA_HEADS),
        in_specs=[new, old, old, new, new, small(lq1), small(lk1), small(lq2), small(lk2), small(sg)],
        out_specs=new,
        out_shape=jax.ShapeDtypeStruct((bsz, l, DA_QK), BF16),
        compiler_params=_cparams(("parallel", "parallel")),
        name="attn_sample",
    )(q, k_past, v_past, k, v, lq1, lk1, lq2, lk2, sg)


def _out_proj_kernel(x_ref, a_ref, b_ref, w_ref, o_ref):
    o_ref[...] = x_ref[...] + _dot(a_ref[...], w_ref[0:512, :]) + _dot(b_ref[...], w_ref[512:1024, :])


def _out_proj(x, a, b, w):
    n = x.shape[0]
    tm = _row_tile(n)
    row = lambda c: pl.BlockSpec((tm, c), lambda i: (i, 0))
    return pl.pallas_call(
        _out_proj_kernel,
        grid=(n // tm,),
        in_specs=[row(D_MODEL), row(512), row(512), pl.BlockSpec(w.shape, lambda i: (0, 0))],
        out_specs=row(D_MODEL),
        out_shape=jax.ShapeDtypeStruct((n, D_MODEL), F32),
        compiler_params=_cparams(("parallel",)),
        name="out_proj",
    )(x, a, b, w)


def _mem_kv_kernel(m_ref, wk_ref, wv_ref, k32_ref, v32_ref, k16_ref, v16_ref):
    m = m_ref[...].astype(BF16)
    k = _dot(m, wk_ref[0])
    v = _dot(m, wv_ref[0])
    k32_ref[0] = k
    v32_ref[0] = v
    k16_ref[0] = k.astype(BF16)
    v16_ref[0] = v.astype(BF16)


def _mem_kv(mem, wk, wv):
    n = mem.shape[0]
    depth = wk.shape[0]
    tm = _row_tile(n)
    wspec = pl.BlockSpec((1, D_MODEL, D_MODEL), lambda l, i: (l, 0, 0))
    ospec = pl.BlockSpec((1, tm, D_MODEL), lambda l, i: (l, i, 0))
    return pl.pallas_call(
        _mem_kv_kernel,
        grid=(depth, n // tm),
        in_specs=[pl.BlockSpec((tm, D_MODEL), lambda l, i: (i, 0)), wspec, wspec],
        out_specs=[ospec] * 4,
        out_shape=[jax.ShapeDtypeStruct((depth, n, D_MODEL), F32)] * 2
        + [jax.ShapeDtypeStruct((depth, n, D_MODEL), BF16)] * 2,
        compiler_params=_cparams(("parallel", "parallel")),
        name="mem_kv",
    )(mem, wk, wv)


def _cross_kernel(x_ref, g_ref, wq_ref, wo_ref, mk_ref, mv_ref, o_ref):
    x = x_ref[0]
    h = _rms(x, g_ref[...]).astype(BF16)
    q = _dot(h, wq_ref[...]).astype(BF16)
    outs = []
    for hd in range(X_HEADS):
        sl = slice(hd * X_HEAD_DIM, (hd + 1) * X_HEAD_DIM)
        s = _dot_nt(q[:, sl], mk_ref[0, :, sl].astype(BF16)) * (X_HEAD_DIM ** -0.5)
        s = s - jnp.max(s, axis=-1, keepdims=True)
        e = jnp.exp(s)
        p = (e / jnp.sum(e, axis=-1, keepdims=True)).astype(BF16)
        outs.append(_dot(p, mv_ref[0, :, sl].astype(BF16)).astype(BF16))
    o = jnp.concatenate(outs, axis=-1)
    o_ref[0] = x + _dot(o, wo_ref[...])


def _cross(x, g, wq, wo, mk, mv):
    bsz, l, _ = x.shape
    tm = _row_tile(l)
    full = lambda a: pl.BlockSpec(a.shape, lambda bi, i: (0,) * a.ndim)
    xs = pl.BlockSpec((1, tm, D_MODEL), lambda bi, i: (bi, i, 0))
    ms = pl.BlockSpec((1, N_MEM, D_MODEL), lambda bi, i: (bi, 0, 0))
    return pl.pallas_call(
        _cross_kernel,
        grid=(bsz, l // tm),
        in_specs=[xs, full(g), full(wq), full(wo), ms, ms],
        out_specs=xs,
        out_shape=jax.ShapeDtypeStruct((bsz, l, D_MODEL), F32),
        compiler_params=_cparams(("parallel", "parallel")),
        name="cross_attn",
    )(x, g, wq, wo, mk, mv)


PEER_T = 512
PEER_EC = 2048
PEER_SUB = 256
N_CAND = 56


def _top_values(s, k):
    out = []
    work = s
    for _ in range(k):
        m = jnp.max(work, axis=0, keepdims=True)
        out.append(m)
        work = jnp.where(work >= m, NEG_INF, work)
    return out


def _bf16_value(x):
    return x.astype(BF16).astype(F32)


def _bf16_below(x):
    return lax.bitcast_convert_type(lax.bitcast_convert_type(x, jnp.int32) - 0x10000, F32)


def _bf16_pair_word(x):
    bits = lax.bitcast_convert_type(x, jnp.uint32)
    return lax.bitcast_convert_type((bits & jnp.uint32(0xFFFF0000)) | (bits >> 16), jnp.int32)


EXP_FLOOR = -80.0


def _ranked_weights(s, k):
    vals, nums = [], []
    work = s
    placed = jnp.zeros(s.shape, F32)
    for r in range(k):
        m = jnp.max(work, axis=0, keepdims=True)
        hit = work >= m
        v = _bf16_value(jnp.exp(jnp.maximum(m - vals[0], EXP_FLOOR))) if r else jnp.ones_like(m)
        if r:
            v = jnp.minimum(v, _bf16_below(nums[-1]))
        placed = jnp.where(hit, v, placed)
        work = jnp.where(hit, NEG_INF, work)
        vals.append(m)
        nums.append(v)
    return vals, nums, placed


def _peer_route_kernel(x_ref, g_ref, wq_ref, sk_ref, ht_ref, ecut_ref, e0_ref, e1_ref):
    h = _rms(x_ref[...], g_ref[...])
    ht_ref[...] = h.T.astype(BF16)
    q = _dot(h.astype(BF16), wq_ref[...]).astype(BF16)
    t = h.shape[0]
    half = PEER_KEYS // 2
    for hd in range(PEER_HEADS):
        base = hd * 2 * PEER_KEYS
        s0_all = _dot_nt(sk_ref[0], q[:, base:base + PEER_KEYS])
        s1_all = _dot_nt(sk_ref[1], q[:, base + PEER_KEYS:base + 2 * PEER_KEYS])
        for tc in range(t // LANES):
            cols = slice(tc * LANES, (tc + 1) * LANES)
            s0 = s0_all[:, cols]
            u0 = _top_values(s0, PEER_TOPK + 1)
            u1, n1, e1 = _ranked_weights(s1_all[:, cols], PEER_TOPK + 1)
            cands = [u0[a] + u1[b] for a in range(PEER_TOPK + 1) for b in range(PEER_TOPK + 1)
                     if (a + 1) * (b + 1) <= PEER_TOPK + 1]
            cands += [jnp.full((1, LANES), NEG_INF, F32)] * (N_CAND - len(cands))
            best = _top_values(jnp.concatenate(cands, axis=0), PEER_TOPK + 1)
            thr = 0.5 * (best[PEER_TOPK - 1] + best[PEER_TOPK])
            z = jnp.ones_like(best[0])
            for b in best[1:PEER_TOPK]:
                z = z + jnp.exp(b - best[0])
            cut = thr - s0
            ecut = jnp.full(cut.shape, 2.0, F32)
            for r in range(PEER_TOPK + 1):
                ecut = jnp.where(u1[r] > cut, n1[r], ecut)
            ecut_ref[hd, :, cols] = _bf16_pair_word(ecut)
            e0_ref[hd, :, cols] = _bf16_pair_word(
                _bf16_value(jnp.exp(jnp.maximum(s0 - u0[0], EXP_FLOOR)) / z))
            lo = lax.bitcast_convert_type(e1[0:half], jnp.uint32) >> 16
            hi = lax.bitcast_convert_type(e1[half:PEER_KEYS], jnp.uint32) & jnp.uint32(0xFFFF0000)
            e1_ref[hd, :, cols] = lax.bitcast_convert_type(hi | lo, jnp.int32)


def _peer_route(x, g, wq, sk):
    n = x.shape[0]
    t = _row_tile(n, PEER_T)
    full = lambda a: pl.BlockSpec(a.shape, lambda i: (0,) * a.ndim)
    rspec = pl.BlockSpec((PEER_HEADS, PEER_KEYS, t), lambda i: (0, 0, i))
    rshape = jax.ShapeDtypeStruct((PEER_HEADS, PEER_KEYS, n), jnp.int32)
    e1spec = pl.BlockSpec((PEER_HEADS, PEER_KEYS // 2, t), lambda i: (0, 0, i))
    e1shape = jax.ShapeDtypeStruct((PEER_HEADS, PEER_KEYS // 2, n), jnp.int32)
    return pl.pallas_call(
        _peer_route_kernel,
        grid=(n // t,),
        in_specs=[pl.BlockSpec((t, D_MODEL), lambda i: (i, 0)), full(g), full(wq), full(sk)],
        out_specs=[pl.BlockSpec((D_MODEL, t), lambda i: (0, i)), rspec, rspec, e1spec],
        out_shape=[jax.ShapeDtypeStruct((D_MODEL, n), BF16), rshape, rshape, e1shape],
        compiler_params=_cparams(("parallel",)),
        name="peer_route",
    )(x, g, wq, sk)


def _peer_dense_kernel(ht_ref, ecut_ref, e0_ref, e1_ref, eu_ref, evt_ref, x_ref, o_ref,
                       acc_scr, at_scr, w_scr):
    j = pl.program_id(1)
    rows_step = PEER_EC // PEER_KEYS
    rows_sub = PEER_SUB // PEER_KEYS

    @pl.when(j == 0)
    def _():
        acc_scr[...] = jnp.zeros(acc_scr.shape, F32)

    t = ht_ref.shape[1]

    def packed_row(row, cols):
        return pltpu.bitcast(jnp.broadcast_to(row[:, cols], (PEER_KEYS // 2, LANES)), BF16)

    def hidden_and_weights(sc):
        rows = slice(sc * PEER_SUB, (sc + 1) * PEER_SUB)
        at_scr[rows, :] = _dot(eu_ref[rows, :], ht_ref[...]).astype(BF16)
        for ii in range(rows_sub):
            i0 = j * rows_step + sc * rows_sub + ii
            r0 = sc * PEER_SUB + ii * PEER_KEYS
            ecut_rows = [ecut_ref[hd, pl.ds(i0, 1), :] for hd in range(PEER_HEADS)]
            e0_rows = [e0_ref[hd, pl.ds(i0, 1), :] for hd in range(PEER_HEADS)]
            for tc in range(t // LANES):
                cols = slice(tc * LANES, (tc + 1) * LANES)
                w = None
                for hd in range(PEER_HEADS):
                    e1 = pltpu.bitcast(e1_ref[hd, :, cols], BF16)
                    term = jnp.where(e1 >= packed_row(ecut_rows[hd], cols), e1, jnp.zeros_like(e1))
                    term = term * packed_row(e0_rows[hd], cols)
                    w = term if w is None else w + term
                w_scr[r0:r0 + PEER_KEYS, cols] = w

    def expert_output(sc):
        rows = slice(sc * PEER_SUB, (sc + 1) * PEER_SUB)
        return _dot(evt_ref[:, rows], w_scr[rows, :] * jax.nn.gelu(at_scr[rows, :]))

    n_sub = PEER_EC // PEER_SUB
    hidden_and_weights(0)
    total = None
    for sc in range(n_sub):
        if sc + 1 < n_sub:
            hidden_and_weights(sc + 1)
        part = expert_output(sc)
        total = part if total is None else total + part
    acc_scr[...] += total

    @pl.when(j == pl.num_programs(1) - 1)
    def _():
        o_ref[...] = x_ref[...] + acc_scr[...].T


def _peer_dense(x, ht, ecut, e0, e1, eu, evt):
    n = x.shape[0]
    t = _row_tile(n, PEER_T)
    rspec = pl.BlockSpec((PEER_HEADS, PEER_KEYS, t), lambda i, j: (0, 0, i))
    e1spec = pl.BlockSpec((PEER_HEADS, PEER_KEYS // 2, t), lambda i, j: (0, 0, i))
    xspec = pl.BlockSpec((t, D_MODEL), lambda i, j: (i, 0))
    return pl.pallas_call(
        _peer_dense_kernel,
        grid=(n // t, PEER_EXPERTS // PEER_EC),
        in_specs=[pl.BlockSpec((D_MODEL, t), lambda i, j: (0, i)), rspec, rspec, e1spec,
                  pl.BlockSpec((PEER_EC, D_MODEL), lambda i, j: (j, 0)),
                  pl.BlockSpec((D_MODEL, PEER_EC), lambda i, j: (0, j)), xspec],
        out_specs=xspec,
        out_shape=jax.ShapeDtypeStruct((n, D_MODEL), F32),
        scratch_shapes=[pltpu.VMEM((D_MODEL, t), F32), pltpu.VMEM((PEER_EC, t), BF16),
                        pltpu.VMEM((PEER_EC, t), BF16)],
        compiler_params=_cparams(("parallel", "arbitrary")),
        name="peer_dense",
    )(ht, ecut, e0, e1, eu, evt, x)


def _cd_in_kernel(x_ref, g_ref, wc_ref, wg_ref, wx_ref, wd_ref, lg_ref, lb_ref,
                  u_ref, v_ref, gate_ref, xbc_ref, dt_ref):
    h = _rms(x_ref[...], g_ref[...]).astype(BF16)
    u_ref[...] = jax.nn.gelu(_dot(h, wc_ref[:, 0:512]))
    v_ref[...] = _layernorm(jax.nn.gelu(_dot(h, wc_ref[:, 512:1024])), lg_ref[...], lb_ref[...])
    gate_ref[...] = _dot(h, wg_ref[...])
    xbc_ref[...] = _dot(h, wx_ref[...])
    dt_ref[...] = _dot(h, wd_ref[...])


def _cd_in(x, g, wc, wg, wx, wd, lg, lb):
    n = x.shape[0]
    tm = _row_tile(n)
    row = lambda c: pl.BlockSpec((tm, c), lambda i: (i, 0))
    full = lambda a: pl.BlockSpec(a.shape, lambda i: (0,) * a.ndim)
    widths = (512, 512, 512, SSM_CONV_CH, LANES)
    return pl.pallas_call(
        _cd_in_kernel,
        grid=(n // tm,),
        in_specs=[row(D_MODEL), full(g), full(wc), full(wg), full(wx), full(wd), full(lg), full(lb)],
        out_specs=[row(c) for c in widths],
        out_shape=[jax.ShapeDtypeStruct((n, c), F32) for c in widths],
        compiler_params=_cparams(("parallel",)),
        name="cd_in",
    )(x, g, wc, wg, wx, wd, lg, lb)


def _gmlp_kernel(u_ref, v_ref, ws_ref, bs_ref, o_ref, *, gm_len, n_chunks):
    r = lax.broadcasted_iota(jnp.int32, (gm_len, gm_len), 0)
    c = lax.broadcasted_iota(jnp.int32, (gm_len, gm_len), 1)
    for g in range(GM_GROUPS):
        w = jnp.where(r >= c, ws_ref[g], 0.0).astype(BF16)
        ch = slice(g * LANES, (g + 1) * LANES)
        for ci in range(n_chunks):
            rows = slice(ci * gm_len, (ci + 1) * gm_len)
            mixed = _dot(w, v_ref[0, rows, ch].astype(BF16)) + bs_ref[:, g:g + 1]
            o_ref[0, rows, ch] = (u_ref[0, rows, ch] * mixed).astype(BF16)


def _gmlp(u, v, ws, bs_t, gm_len):
    bsz, l, _ = u.shape
    tl = _row_tile(l)
    full = lambda a: pl.BlockSpec(a.shape, lambda bi, i: (0,) * a.ndim)
    spec = pl.BlockSpec((1, tl, GM_WIDTH), lambda bi, i: (bi, i, 0))
    return pl.pallas_call(
        functools.partial(_gmlp_kernel, gm_len=gm_len, n_chunks=tl // gm_len),
        grid=(bsz, l // tl),
        in_specs=[spec, spec, full(ws), full(bs_t)],
        out_specs=spec,
        out_shape=jax.ShapeDtypeStruct((bsz, l, GM_WIDTH), BF16),
        compiler_params=_cparams(("parallel", "parallel")),
        name="gmlp",
    )(u, v, ws, bs_t)


SSD_HALO = 8
SSD_PAIRS = SSM_HEADS // 2
SSD_GROUP_W = SSM_INNER // 2


def _ssd_kernel(xbc_ref, halo_ref, ctx_ref, gate_ref, dt_ref, h0_ref, cw_ref, cb_ref, dtb_ref, alog_ref,
                dsk_ref, ng_ref, y_ref, fin_ref, xp_scr, st_scr, *, tl):
    i = pl.program_id(1)
    pad = SSD_HALO - (SSM_CONV_K - 1)
    q = CHUNK

    @pl.when(i == 0)
    def _():
        xp_scr[pad:SSD_HALO, :] = ctx_ref[0]
        for k in range(SSD_PAIRS):
            st_scr[k] = jnp.concatenate([h0_ref[0, 2 * k], h0_ref[0, 2 * k + 1]], axis=0).T

    @pl.when(i > 0)
    def _():
        xp_scr[0:SSD_HALO, :] = halo_ref[0]

    xp_scr[SSD_HALO:SSD_HALO + tl, :] = xbc_ref[0]

    lane128 = lax.broadcasted_iota(jnp.int32, (q, LANES), 1)
    row128 = lax.broadcasted_iota(jnp.int32, (q, LANES), 0)
    er = lax.broadcasted_iota(jnp.int32, (LANES, SSM_INNER), 0)
    ec = lax.broadcasted_iota(jnp.int32, (LANES, SSM_INNER), 1)
    expand = jnp.where((ec >> 6) == er, 1.0, 0.0).astype(F32)
    tr = lax.broadcasted_iota(jnp.int32, (q, q), 0)
    tc = lax.broadcasted_iota(jnp.int32, (q, q), 1)
    ltri = jnp.where(tr >= tc, 1.0, 0.0).astype(F32)
    a_neg = -jnp.exp(alog_ref[...])
    zeros_q = jnp.zeros((q, LANES), F32)

    for ci in range(tl // q):
        r0 = ci * q
        acc = jnp.broadcast_to(cb_ref[...], (q, SSM_CONV_CH))
        for k in range(SSM_CONV_K):
            acc = acc + cw_ref[k:k + 1, :] * xp_scr[r0 + pad + k:r0 + pad + k + q, :]
        xc = _silu(acc)
        xs = xc[:, 0:SSM_INNER]
        dt = jnp.where(lane128 < SSM_HEADS, _softplus(dt_ref[0, r0:r0 + q, :] + dtb_ref[...]), 0.0)
        acs = _dot_f32(ltri, dt * a_neg)
        acs_e = _dot_f32(acs, expand)
        dt_e = _dot_f32(dt, expand)
        tot_e = acs_e[q - 1:q, :]
        xdt = xs * dt_e
        xd = (xdt * jnp.exp(tot_e - acs_e)).astype(BF16)
        eacs = jnp.exp(acs_e)
        cdec = jnp.exp(tot_e)
        ys = []
        for k in range(SSD_PAIRS):
            g = k // 2
            blk = slice(k * LANES, (k + 1) * LANES)
            bm = xc[:, SSM_INNER + g * SSM_STATE:SSM_INNER + (g + 1) * SSM_STATE]
            cm = xc[:, SSM_INNER + 2 * SSM_STATE + g * SSM_STATE:
                    SSM_INNER + 2 * SSM_STATE + (g + 1) * SSM_STATE].astype(BF16)
            cb2 = _dot_nt(cm, jnp.concatenate([bm, bm], axis=0).astype(BF16))
            a_blk = acs_e[:, blk]
            a_row = jnp.sum(jnp.where(row128 == (lane128 & (q - 1)), a_blk, 0.0), axis=0, keepdims=True)
            lmat = jnp.exp(jnp.where((lane128 & (q - 1)) <= row128, a_blk - a_row, NEG_INF))
            sc = (cb2 * lmat).astype(BF16)
            x_blk = xdt[:, blk]
            rhs = jnp.concatenate([jnp.where(lane128 < q, x_blk, 0.0), jnp.where(lane128 >= q, x_blk, 0.0)],
                                  axis=0).astype(BF16)
            prev = st_scr[k]
            y_pair = _dot(sc, rhs) + _dot(cm, prev.astype(BF16)) * eacs[:, blk]
            ys.append(y_pair)
            bt = jnp.concatenate([bm, zeros_q], axis=0).T.astype(BF16)
            xd_pad = jnp.concatenate([xd[:, blk], zeros_q.astype(BF16)], axis=0)
            st_scr[k] = prev * cdec[:, blk] + _dot(bt, xd_pad)
        y = jnp.concatenate(ys, axis=-1) + dsk_ref[...] * xs
        y = y * _silu(gate_ref[0, r0:r0 + q, :])
        outs = []
        for g in range(2):
            seg = y[:, g * SSD_GROUP_W:(g + 1) * SSD_GROUP_W]
            outs.append(seg * lax.rsqrt(jnp.mean(seg * seg, axis=-1, keepdims=True) + 1e-6))
        y_ref[0, r0:r0 + q, :] = (jnp.concatenate(outs, axis=-1) * ng_ref[...]).astype(BF16)

    @pl.when(i == pl.num_programs(1) - 1)
    def _():
        for k in range(SSD_PAIRS):
            st = st_scr[k].T
            fin_ref[0, 2 * k] = st[0:SSM_HEAD_DIM]
            fin_ref[0, 2 * k + 1] = st[SSM_HEAD_DIM:2 * SSM_HEAD_DIM]


def _ssd(xbc, ctx, gate, dt, h0, cw, cb, dtb, alog, dsk, ng):
    bsz, l, _ = xbc.shape
    tl = 256 if l % 256 == 0 else l
    hb = tl // SSD_HALO
    full = lambda a: pl.BlockSpec(a.shape, lambda bi, i: (0,) * a.ndim)
    tile = lambda c: pl.BlockSpec((1, tl, c), lambda bi, i: (bi, i, 0))
    stspec = pl.BlockSpec((1, SSM_HEADS, SSM_HEAD_DIM, SSM_STATE), lambda bi, i: (bi, 0, 0, 0))
    return pl.pallas_call(
        functools.partial(_ssd_kernel, tl=tl),
        grid=(bsz, l // tl),
        in_specs=[tile(SSM_CONV_CH),
                  pl.BlockSpec((1, SSD_HALO, SSM_CONV_CH), lambda bi, i: (bi, jnp.maximum(i * hb - 1, 0), 0)),
                  pl.BlockSpec((1, SSM_CONV_K - 1, SSM_CONV_CH), lambda bi, i: (bi, 0, 0)),
                  tile(SSM_INNER), tile(LANES), stspec,
                  full(cw), full(cb), full(dtb), full(alog), full(dsk), full(ng)],
        out_specs=[tile(SSM_INNER), stspec],
        out_shape=[jax.ShapeDtypeStruct((bsz, l, SSM_INNER), BF16),
                   jax.ShapeDtypeStruct((bsz, SSM_HEADS, SSM_HEAD_DIM, SSM_STATE), F32)],
        scratch_shapes=[pltpu.VMEM((SSD_HALO + tl, SSM_CONV_CH), F32),
                        pltpu.VMEM((SSD_PAIRS, SSM_STATE, LANES), F32)],
        compiler_params=_cparams(("parallel", "arbitrary")),
        name="ssd",
    )(xbc, xbc, ctx, gate, dt, h0, cw, cb, dtb, alog, dsk, ng)


def _final_rms_kernel(x_ref, g_ref, o_ref):
    o_ref[...] = _rms(x_ref[...], g_ref[...])


def _final_rms(x, g):
    n = x.shape[0]
    tm = _row_tile(n)
    row = pl.BlockSpec((tm, D_MODEL), lambda i: (i, 0))
    return pl.pallas_call(
        _final_rms_kernel,
        grid=(n // tm,),
        in_specs=[row, pl.BlockSpec(g.shape, lambda i: (0, 0))],
        out_specs=row,
        out_shape=jax.ShapeDtypeStruct((n, D_MODEL), F32),
        compiler_params=_cparams(("parallel",)),
        name="final_rms",
    )(x, g)


def _interleave_experts(table):
    depth, _, d = table.shape
    half = PEER_KEYS // 2
    return table.reshape(depth, PEER_KEYS, 2, half, d).swapaxes(2, 3).reshape(depth, PEER_EXPERTS, d)


def _pad_lanes(a, width=LANES):
    return jnp.pad(a, ((0, 0), (0, width - a.shape[-1])))


def _trunk(x, mem_k, mem_v, attn_k, attn_v, conv_a, ssd_st, conv_ssm, gm_len, p):
    bsz, l, _ = x.shape
    n = bsz * l
    row = lambda a: a.reshape(1, -1)
    x2 = x.reshape(n, D_MODEL)

    lam_init = 0.8 - 0.6 * math.exp(-0.3 * 0)
    glu, q16, k32, v32, k16, v16 = _ab_in(x2, row(p["norm_mix_g"][0]), p["w_in_ab"])
    glu3 = glu.reshape(bsz, l, CONV_CH)
    ctx_a = jnp.zeros((bsz, CONV_K - 1, CONV_CH), F32) if conv_a is None else conv_a[0]
    ca = _conv_a(glu3, ctx_a, p["conv_a_w"], row(p["conv_a_b"]), row(p["ln_a_g"]), row(p["ln_a_b"]))
    lam_args = (row(p["lam_q1"]), row(p["lam_k1"]), row(p["lam_q2"]), row(p["lam_k2"]), row(p["subln_g"]))
    shp3 = lambda a: a.reshape(bsz, l, DA_QK)
    if attn_k is None:
        o = _attn_prompt(shp3(q16), shp3(k16), shp3(v16), *lam_args, lam_init)
    else:
        past = attn_k.shape[2]
        o = _attn_sample(shp3(q16), shp3(k16), shp3(v16), attn_k[0].reshape(bsz, past, DA_QK),
                         attn_v[0].reshape(bsz, past, DA_QK), *lam_args, lam_init)
    x2 = _out_proj(x2, ca.reshape(n, CONV_CH), o.reshape(n, DA_QK), p["w_out_ab"])
    new_k = k32.reshape(1, bsz, l, DA_HEADS, 2 * DA_HEAD_DIM)
    new_v = v32.reshape(1, bsz, l, DA_HEADS, 2 * DA_HEAD_DIM)
    new_ca = glu3[:, l - (CONV_K - 1):][None]

    def tail(x2, layer):
        x3 = _cross(x2.reshape(bsz, l, D_MODEL), row(p["norm_cross_g"][layer]), p["w_xq"][layer],
                    p["w_xo"][layer], mem_k[layer], mem_v[layer]).reshape(n, D_MODEL)
        routed = _peer_route(x3, row(p["norm_ffn_g"][layer]), p["w_pq"][layer], p["sub_keys"][layer])
        return _peer_dense(x3, *routed, p["expert_u"][layer], p["expert_vt"][layer])

    x2 = tail(x2, 0)

    u, vln, gate, xbc, dt = _cd_in(x2, row(p["norm_mix_g"][1]), p["w_cd_c"], p["w_cd_gate"], p["w_cd_xbc"],
                                   p["w_cd_dt"], row(p["ln_c_g"]), row(p["ln_c_b"]))
    shp = lambda a: a.reshape(bsz, l, a.shape[-1])
    c_out = _gmlp(shp(u), shp(vln), p["gm_w_s"][:, :gm_len, :gm_len], p["gm_b_s"][:, :gm_len].T, gm_len)
    xbc3 = shp(xbc)
    ctx_d = jnp.zeros((bsz, SSM_CONV_K - 1, SSM_CONV_CH), F32) if conv_ssm is None else conv_ssm[0]
    h0 = jnp.zeros((bsz, SSM_HEADS, SSM_HEAD_DIM, SSM_STATE), F32) if ssd_st is None else ssd_st[0]
    y, fin = _ssd(xbc3, ctx_d, shp(gate), shp(dt), h0, p["conv_d_w"], row(p["conv_d_b"]),
                  _pad_lanes(row(p["dt_bias"])), _pad_lanes(row(p["a_log"])),
                  row(jnp.repeat(p["d_skip"], SSM_HEAD_DIM)), row(p["norm_d_g"]))
    x2 = _out_proj(x2, c_out.reshape(n, GM_WIDTH), y.reshape(n, SSM_INNER), p["w_out_cd"])
    x2 = tail(x2, 1)
    y_out = _final_rms(x2, row(p["norm_final_g"])).reshape(bsz, l, D_MODEL)
    new_gv = vln.reshape(1, bsz, l, GM_GROUPS, GM_WIDTH // GM_GROUPS)
    new_cs = xbc3[:, l - (SSM_CONV_K - 1):][None]
    return y_out, new_k, new_v, new_ca, new_gv, fin[None], new_cs


def kernel(x_prompt, x_sample, cache_attn_k, cache_attn_v, state_conv_a, state_ssd, state_conv_ssm, cache_mem_k, cache_mem_v, mem_prompt, norm_mix_g, norm_cross_g, norm_ffn_g, norm_final_g, w_in_ab, conv_a_w, conv_a_b, ln_a_g, ln_a_b, lam_q1, lam_k1, lam_q2, lam_k2, subln_g, w_out_ab, w_in_cd, ln_c_g, ln_c_b, gm_w_s, gm_b_s, conv_d_w, conv_d_b, dt_bias, a_log, d_skip, norm_d_g, w_out_cd, w_xq, w_xk, w_xv, w_xo, w_pq, sub_keys, expert_u, expert_v):
    bf = lambda a: a.astype(BF16)
    w_cd = w_in_cd[0]
    p = {
        "norm_mix_g": norm_mix_g, "norm_cross_g": norm_cross_g, "norm_ffn_g": norm_ffn_g,
        "norm_final_g": norm_final_g,
        "w_in_ab": bf(w_in_ab[0]), "conv_a_w": conv_a_w[0], "conv_a_b": conv_a_b[0],
        "ln_a_g": ln_a_g[0], "ln_a_b": ln_a_b[0],
        "lam_q1": lam_q1[0], "lam_k1": lam_k1[0], "lam_q2": lam_q2[0], "lam_k2": lam_k2[0],
        "subln_g": subln_g[0], "w_out_ab": bf(w_out_ab[0]),
        "w_cd_c": bf(w_cd[:, 0:1024]), "w_cd_gate": bf(w_cd[:, 1024:1536]), "w_cd_xbc": bf(w_cd[:, 1536:2560]),
        "w_cd_dt": bf(_pad_lanes(w_cd[:, 2560:2568])),
        "ln_c_g": ln_c_g[0], "ln_c_b": ln_c_b[0], "gm_w_s": gm_w_s[0], "gm_b_s": gm_b_s[0],
        "conv_d_w": conv_d_w[0], "conv_d_b": conv_d_b[0], "dt_bias": dt_bias[0], "a_log": a_log[0],
        "d_skip": d_skip[0], "norm_d_g": norm_d_g[0], "w_out_cd": bf(w_out_cd[0]),
        "w_xq": bf(w_xq), "w_xo": bf(w_xo), "w_pq": bf(w_pq), "sub_keys": bf(sub_keys),
        "expert_u": _interleave_experts(bf(expert_u)),
        "expert_vt": jnp.swapaxes(_interleave_experts(bf(expert_v)), 1, 2),
    }
    bsz, seq, _ = x_prompt.shape
    dec_b, dec_l, _ = x_sample.shape
    depth = w_xk.shape[0]

    mk32, mv32, mk16, mv16 = _mem_kv(mem_prompt.reshape(bsz * N_MEM, D_MODEL), bf(w_xk), bf(w_xv))
    mem_k_p = mk32.reshape(depth, bsz, N_MEM, X_HEADS, X_HEAD_DIM)
    mem_v_p = mv32.reshape(depth, bsz, N_MEM, X_HEADS, X_HEAD_DIM)
    y_prompt, kp, vp, cap, _, ssdp, csp = _trunk(
        x_prompt, mk16.reshape(depth, bsz, N_MEM, D_MODEL), mv16.reshape(depth, bsz, N_MEM, D_MODEL),
        None, None, None, None, None, 2 * CHUNK, p)

    y_sample, ks, vs, cas, gvs, ssds, css = _trunk(
        x_sample, cache_mem_k.reshape(depth, dec_b, N_MEM, D_MODEL), cache_mem_v.reshape(depth, dec_b, N_MEM, D_MODEL),
        cache_attn_k, cache_attn_v, state_conv_a, state_ssd, state_conv_ssm, dec_l, p)
    return (y_prompt, y_sample, kp, vp, cap, ssdp, csp, mem_k_p, mem_v_p, ks, vs, cas, gvs, ssds, css)
```

```python
import functools
import math

import jax
import jax.numpy as jnp
from jax import lax
from jax.experimental import pallas as pl
from jax.experimental.pallas import tpu as pltpu

F32 = jnp.float32
BF16 = jnp.bfloat16
NEG_INF = float("-inf")

D_MODEL = 1024
CHUNK = 64
CONV_CH = 512
CONV_K = 31
DA_HEADS = 4
DA_HEAD_DIM = 64
DA_QK = 512
GM_WIDTH = 512
GM_GROUPS = 4
SSM_INNER = 512
SSM_HEADS = 8
SSM_HEAD_DIM = 64
SSM_STATE = 128
SSM_CONV_K = 4
SSM_CONV_CH = 1024
N_MEM = 256
X_HEADS = 4
X_HEAD_DIM = 256
PEER_HEADS = 8
PEER_KEYS = 128
PEER_EXPERTS = PEER_KEYS * PEER_KEYS
PEER_TOPK = 16
LANES = 128
SUBLANES = 8
BF16_ROWS = 16
VMEM_LIMIT = 56 * 1024 * 1024


def _cparams(sem):
    return pltpu.CompilerParams(dimension_semantics=sem, vmem_limit_bytes=VMEM_LIMIT)


def _dot(a, b):
    return jnp.dot(a, b, preferred_element_type=F32)


def _dot_nt(a, b):
    return lax.dot_general(a, b, (((1,), (1,)), ((), ())), preferred_element_type=F32)


def _dot_f32(a, b):
    return jnp.dot(a, b, preferred_element_type=F32, precision=lax.Precision.HIGHEST)


def _rms(x, g, eps=1e-6):
    return x * lax.rsqrt(jnp.mean(x * x, axis=-1, keepdims=True) + eps) * g


def _layernorm(x, g, b, eps=1e-5):
    xc = x - jnp.mean(x, axis=-1, keepdims=True)
    return xc * lax.rsqrt(jnp.mean(xc * xc, axis=-1, keepdims=True) + eps) * g + b


def _silu(x):
    return x * jax.nn.sigmoid(x)


def _softplus(x):
    return jnp.maximum(x, 0.0) + jnp.log(1.0 + jnp.exp(-jnp.abs(x)))


def _row_tile(n, pref=512):
    return pref if n % pref == 0 else n


def _ab_in_kernel(x_ref, g_ref, w_ref, glu_ref, q_ref, k32_ref, v32_ref, k16_ref, v16_ref):
    h = _rms(x_ref[...], g_ref[...]).astype(BF16)
    a_val = _dot(h, w_ref[:, 0:512])
    a_gate = _dot(h, w_ref[:, 512:1024])
    glu_ref[...] = a_val * jax.nn.sigmoid(a_gate)
    q = _dot(h, w_ref[:, 1024:1536])
    q_ref[...] = (q * (DA_HEAD_DIM ** -0.5 * math.log2(math.e))).astype(BF16)
    k = _dot(h, w_ref[:, 1536:2048])
    k32_ref[...] = k
    k16_ref[...] = k.astype(BF16)
    v = _dot(h, w_ref[:, 2048:2560])
    v32_ref[...] = v
    v16_ref[...] = v.astype(BF16)


def _ab_in(x, g, w):
    n = x.shape[0]
    tm = _row_tile(n)
    row = lambda c: pl.BlockSpec((tm, c), lambda i: (i, 0))
    full = lambda a: pl.BlockSpec(a.shape, lambda i: (0,) * a.ndim)
    return pl.pallas_call(
        _ab_in_kernel,
        grid=(n // tm,),
        in_specs=[row(D_MODEL), full(g), full(w)],
        out_specs=[row(512)] * 6,
        out_shape=[jax.ShapeDtypeStruct((n, 512), F32), jax.ShapeDtypeStruct((n, 512), BF16),
                   jax.ShapeDtypeStruct((n, 512), F32), jax.ShapeDtypeStruct((n, 512), F32),
                   jax.ShapeDtypeStruct((n, 512), BF16), jax.ShapeDtypeStruct((n, 512), BF16)],
        compiler_params=_cparams(("parallel",)),
        name="ab_in",
    )(x, g, w)


CONV_HALO = 32
CONV_RB = 64


def _conv_a_kernel(glu_ref, halo_ref, ctx_ref, w_ref, b_ref, lg_ref, lb_ref, o_ref, xp_scr, sh_scr, *, tl):
    i = pl.program_id(1)
    pad = CONV_HALO - (CONV_K - 1)

    @pl.when(i == 0)
    def _():
        xp_scr[0:pad, :] = jnp.zeros((pad, CONV_CH), F32)
        xp_scr[pad:CONV_HALO, :] = ctx_ref[0]

    @pl.when(i > 0)
    def _():
        xp_scr[0:CONV_HALO, :] = halo_ref[0]

    xp_scr[CONV_HALO:CONV_HALO + tl, :] = glu_ref[0]
    for s in range(1, SUBLANES):
        sh_scr[s - 1, 0:CONV_HALO + tl - SUBLANES, :] = xp_scr[s:s + CONV_HALO + tl - SUBLANES, :]
    rb = min(CONV_RB, tl)
    for r0 in range(0, tl, rb):
        acc = jnp.broadcast_to(b_ref[...], (rb, CONV_CH))
        for k in range(CONV_K):
            phase = (pad + k) % SUBLANES
            base = r0 + pad + k - phase
            rows = xp_scr[base:base + rb, :] if phase == 0 else sh_scr[phase - 1, base:base + rb, :]
            acc = acc + w_ref[k:k + 1, :] * rows
        y = _layernorm(acc, lg_ref[...], lb_ref[...])
        o_ref[0, r0:r0 + rb, :] = _silu(y).astype(BF16)


def _conv_a(glu, ctx, w, b, lg, lb):
    bsz, l, _ = glu.shape
    tl = _row_tile(l)
    hb = tl // CONV_HALO
    full = lambda a: pl.BlockSpec(a.shape, lambda bi, i: (0,) * a.ndim)
    return pl.pallas_call(
        functools.partial(_conv_a_kernel, tl=tl),
        grid=(bsz, l // tl),
        in_specs=[pl.BlockSpec((1, tl, CONV_CH), lambda bi, i: (bi, i, 0)),
                  pl.BlockSpec((1, CONV_HALO, CONV_CH), lambda bi, i: (bi, jnp.maximum(i * hb - 1, 0), 0)),
                  pl.BlockSpec((1, CONV_K - 1, CONV_CH), lambda bi, i: (bi, 0, 0)),
                  full(w), full(b), full(lg), full(lb)],
        out_specs=pl.BlockSpec((1, tl, CONV_CH), lambda bi, i: (bi, i, 0)),
        out_shape=jax.ShapeDtypeStruct((bsz, l, CONV_CH), BF16),
        scratch_shapes=[pltpu.VMEM((CONV_HALO + tl, CONV_CH), F32),
                        pltpu.VMEM((SUBLANES - 1, CONV_HALO + tl, CONV_CH), F32)],
        compiler_params=_cparams(("parallel", "arbitrary")),
        name="conv_a",
    )(glu, glu, ctx, w, b, lg, lb)


def _lambda(lq1, lk1, lq2, lk2, lam_init):
    return (jnp.exp(jnp.sum(lq1[...] * lk1[...], axis=-1, keepdims=True))
            - jnp.exp(jnp.sum(lq2[...] * lk2[...], axis=-1, keepdims=True)) + lam_init)


def _split_q(q):
    lane = lax.broadcasted_iota(jnp.int32, q.shape, 1)
    z = jnp.zeros_like(q)
    return jnp.concatenate([jnp.where(lane < DA_HEAD_DIM, q, z), jnp.where(lane >= DA_HEAD_DIM, q, z)], axis=0)


def _diff_finish(acc, l, tq, lam, sg, lam_init):
    o = acc[0:tq] / l[0:tq] - lam * (acc[tq:2 * tq] / l[tq:2 * tq])
    return _rms(o, sg) * (1.0 - lam_init)


ATTN_TQ = 512
ATTN_TK = 512


def _attn_prompt_kernel(qi_ref, ki_ref, fl_ref, q_ref, k_ref, v_ref, lq1, lk1, lq2, lk2, sg_ref, o_ref,
                        qs_scr, m_scr, acc_scr, *, tq, tk, lam_init):
    p = pl.program_id(1)
    qi = qi_ref[p]
    ki = ki_ref[p]
    flags = fl_ref[p]

    @pl.when(ki == 0)
    def _():
        for h in range(DA_HEADS):
            qs_scr[h] = _split_q(q_ref[0, :, h * LANES:(h + 1) * LANES])
        m_scr[...] = jnp.full(m_scr.shape, NEG_INF, F32)
        acc_scr[...] = jnp.zeros(acc_scr.shape, F32)

    ones_col = jnp.where(lax.broadcasted_iota(jnp.int32, (tk, LANES), 1) == 0, 1.0, 0.0).astype(BF16)

    def step(masked):
        for h in range(DA_HEADS):
            sl = slice(h * LANES, (h + 1) * LANES)
            s = _dot_nt(qs_scr[h], k_ref[0, :, sl])
            if masked:
                r = lax.broadcasted_iota(jnp.int32, s.shape, 0)
                c = lax.broadcasted_iota(jnp.int32, s.shape, 1)
                q_pos = qi * tq + jnp.where(r >= tq, r - tq, r)
                s = jnp.where(((ki * tk + c) >> 6) <= (q_pos >> 6), s, -1e30)
            m_old = m_scr[h]
            m_new = jnp.maximum(m_old, jnp.broadcast_to(jnp.max(s, axis=-1, keepdims=True), m_old.shape))
            alpha = jnp.exp2(m_old - m_new)
            pr = jnp.exp2(s - jnp.concatenate([m_new] * (tk // LANES), axis=1)).astype(BF16)
            v_aug = jnp.concatenate([v_ref[0, :, sl], ones_col], axis=1)
            acc_scr[h] = jnp.concatenate([alpha, alpha], axis=1) * acc_scr[h] + _dot(pr, v_aug)
            m_scr[h] = m_new

    @pl.when((flags & 1) == 0)
    def _():
        step(False)

    @pl.when((flags & 1) == 1)
    def _():
        step(True)

    @pl.when((flags & 2) != 0)
    def _():
        lam = _lambda(lq1, lk1, lq2, lk2, lam_init)
        for h in range(DA_HEADS):
            acc = acc_scr[h]
            o = _diff_finish(acc[:, 0:LANES], acc[:, LANES:LANES + 1], tq, lam, sg_ref[...], lam_init)
            o_ref[0, :, h * LANES:(h + 1) * LANES] = o.astype(BF16)


def _attn_prompt(q, k, v, lq1, lk1, lq2, lk2, sg, lam_init):
    bsz, l, _ = q.shape
    tq = _row_tile(l, ATTN_TQ)
    tk = _row_tile(l, ATTN_TK)
    pairs = []
    for a in range(l // tq):
        last = ((a + 1) * tq - 1) // tk
        for b in range(last + 1):
            crosses = (b + 1) * tk > a * tq + CHUNK
            pairs.append((a, b, int(crosses) + 2 * int(b == last)))
    tabs = [jnp.asarray([pr[i] for pr in pairs], jnp.int32) for i in range(3)]
    small = lambda a: pl.BlockSpec(a.shape, lambda bi, p, qt, kt, fl: (0,) * a.ndim)
    grid_spec = pltpu.PrefetchScalarGridSpec(
        num_scalar_prefetch=3,
        grid=(bsz, len(pairs)),
        in_specs=[pl.BlockSpec((1, tq, DA_QK), lambda bi, p, qt, kt, fl: (bi, qt[p], 0)),
                  pl.BlockSpec((1, tk, DA_QK), lambda bi, p, qt, kt, fl: (bi, kt[p], 0)),
                  pl.BlockSpec((1, tk, DA_QK), lambda bi, p, qt, kt, fl: (bi, kt[p], 0)),
                  small(lq1), small(lk1), small(lq2), small(lk2), small(sg)],
        out_specs=pl.BlockSpec((1, tq, DA_QK), lambda bi, p, qt, kt, fl: (bi, qt[p], 0)),
        scratch_shapes=[pltpu.VMEM((DA_HEADS, 2 * tq, LANES), BF16), pltpu.VMEM((DA_HEADS, 2 * tq, LANES), F32),
                        pltpu.VMEM((DA_HEADS, 2 * tq, 2 * LANES), F32)],
    )
    return pl.pallas_call(
        functools.partial(_attn_prompt_kernel, tq=tq, tk=tk, lam_init=lam_init),
        grid_spec=grid_spec,
        out_shape=jax.ShapeDtypeStruct((bsz, l, DA_QK), BF16),
        compiler_params=_cparams(("parallel", "arbitrary")),
        name="attn_prompt",
    )(*tabs, q, k, v, lq1, lk1, lq2, lk2, sg)


def _attn_sample_kernel(q_ref, kc_ref, vc_ref, kn_ref, vn_ref, lq1, lk1, lq2, lk2, sg_ref, o_ref,
                        *, tq, past_len, lam_init):
    qs = _split_q(q_ref[0])
    s_c = _dot_nt(qs, kc_ref[0].astype(BF16))
    s_n = _dot_nt(qs, kn_ref[0])
    r = lax.broadcasted_iota(jnp.int32, s_n.shape, 0)
    c = lax.broadcasted_iota(jnp.int32, s_n.shape, 1)
    qrow = jnp.where(r >= tq, r - tq, r)
    s_n = jnp.where(((past_len + c) >> 6) <= ((past_len + qrow) >> 6), s_n, -1e30)
    m = jnp.maximum(jnp.max(s_c, axis=-1, keepdims=True), jnp.max(s_n, axis=-1, keepdims=True))
    p_c = jnp.exp2(s_c - m)
    p_n = jnp.exp2(s_n - m)
    l = jnp.sum(p_c, axis=-1, keepdims=True) + jnp.sum(p_n, axis=-1, keepdims=True)
    acc = _dot(p_c.astype(BF16), vc_ref[0].astype(BF16)) + _dot(p_n.astype(BF16), vn_ref[0])
    lam = _lambda(lq1, lk1, lq2, lk2, lam_init)
    o_ref[0] = _diff_finish(acc, l, tq, lam, sg_ref[...], lam_init).astype(BF16)


def _attn_sample(q, k, v, k_past, v_past, lq1, lk1, lq2, lk2, sg, lam_init):
    bsz, l, _ = q.shape
    past_len = k_past.shape[1]
    small = lambda a: pl.BlockSpec(a.shape, lambda bi, h: (0,) * a.ndim)
    new = pl.BlockSpec((1, l, LANES), lambda bi, h: (bi, 0, h))
    old = pl.BlockSpec((1, past_len, LANES), lambda bi, h: (bi, 0, h))
    return pl.pallas_call(
        functools.partial(_attn_sample_kernel, tq=l, past_len=past_len, lam_init=lam_init),
        grid=(bsz, DA_HEADS),
        in_specs=[new, old, old, new, new, small(lq1), small(lk1), small(lq2), small(lk2), small(sg)],
        out_specs=new,
        out_shape=jax.ShapeDtypeStruct((bsz, l, DA_QK), BF16),
        compiler_params=_cparams(("parallel", "parallel")),
        name="attn_sample",
    )(q, k_past, v_past, k, v, lq1, lk1, lq2, lk2, sg)


def _mem_kv_kernel(m_ref, wk_ref, wv_ref, k32_ref, v32_ref, k16_ref, v16_ref):
    m = m_ref[...].astype(BF16)
    k = _dot(m, wk_ref[0])
    v = _dot(m, wv_ref[0])
    k32_ref[0] = k
    v32_ref[0] = v
    k16_ref[0] = k.astype(BF16)
    v16_ref[0] = v.astype(BF16)


def _mem_kv(mem, wk, wv):
    n = mem.shape[0]
    depth = wk.shape[0]
    tm = _row_tile(n)
    wspec = pl.BlockSpec((1, D_MODEL, D_MODEL), lambda l, i: (l, 0, 0))
    ospec = pl.BlockSpec((1, tm, D_MODEL), lambda l, i: (l, i, 0))
    return pl.pallas_call(
        _mem_kv_kernel,
        grid=(depth, n // tm),
        in_specs=[pl.BlockSpec((tm, D_MODEL), lambda l, i: (i, 0)), wspec, wspec],
        out_specs=[ospec] * 4,
        out_shape=[jax.ShapeDtypeStruct((depth, n, D_MODEL), F32)] * 2
        + [jax.ShapeDtypeStruct((depth, n, D_MODEL), BF16)] * 2,
        compiler_params=_cparams(("parallel", "parallel")),
        name="mem_kv",
    )(mem, wk, wv)


def _cross_kernel(x_ref, a_ref, b_ref, wm_ref, g_ref, wq_ref, wo_ref, mk_ref, mv_ref, o_ref):
    x = x_ref[0] + _dot(a_ref[0], wm_ref[0:512, :]) + _dot(b_ref[0], wm_ref[512:1024, :])
    h = _rms(x, g_ref[...]).astype(BF16)
    q = _dot(h, wq_ref[...]).astype(BF16)
    outs = []
    for hd in range(X_HEADS):
        sl = slice(hd * X_HEAD_DIM, (hd + 1) * X_HEAD_DIM)
        s = _dot_nt(q[:, sl], mk_ref[0, :, sl].astype(BF16)) * (X_HEAD_DIM ** -0.5)
        s = s - jnp.max(s, axis=-1, keepdims=True)
        e = jnp.exp(s)
        p = (e / jnp.sum(e, axis=-1, keepdims=True)).astype(BF16)
        outs.append(_dot(p, mv_ref[0, :, sl].astype(BF16)).astype(BF16))
    o = jnp.concatenate(outs, axis=-1)
    o_ref[0] = x + _dot(o, wo_ref[...])


def _cross(x, a, b, wm, g, wq, wo, mk, mv):
    bsz, l, _ = x.shape
    tm = _row_tile(l)
    full = lambda arr: pl.BlockSpec(arr.shape, lambda bi, i: (0,) * arr.ndim)
    xs = pl.BlockSpec((1, tm, D_MODEL), lambda bi, i: (bi, i, 0))
    hs = pl.BlockSpec((1, tm, 512), lambda bi, i: (bi, i, 0))
    ms = pl.BlockSpec((1, N_MEM, D_MODEL), lambda bi, i: (bi, 0, 0))
    return pl.pallas_call(
        _cross_kernel,
        grid=(bsz, l // tm),
        in_specs=[xs, hs, hs, full(wm), full(g), full(wq), full(wo), ms, ms],
        out_specs=xs,
        out_shape=jax.ShapeDtypeStruct((bsz, l, D_MODEL), F32),
        compiler_params=_cparams(("parallel", "parallel")),
        name="cross_attn",
    )(x, a, b, wm, g, wq, wo, mk, mv)


PEER_T = 512
PEER_EC = 2048
PEER_SUB = 256
N_CAND = 56


def _top_values(s, k):
    out = []
    work = s
    for _ in range(k):
        m = jnp.max(work, axis=0, keepdims=True)
        out.append(m)
        work = jnp.where(work >= m, NEG_INF, work)
    return out


def _bf16_value(x):
    return x.astype(BF16).astype(F32)


def _bf16_below(x):
    return lax.bitcast_convert_type(lax.bitcast_convert_type(x, jnp.int32) - 0x10000, F32)


def _bf16_pair_word(x):
    bits = lax.bitcast_convert_type(x, jnp.uint32)
    return lax.bitcast_convert_type((bits & jnp.uint32(0xFFFF0000)) | (bits >> 16), jnp.int32)


EXP_FLOOR = -80.0


def _sorting_network(n):
    def merge(lo, hi, r):
        step = r * 2
        if step < hi - lo:
            yield from merge(lo, hi, step)
            yield from merge(lo + r, hi, step)
            yield from [(i, i + r) for i in range(lo + r, hi - r, step)]
        else:
            yield (lo, lo + r)

    def sort(lo, hi):
        if hi - lo >= 1:
            mid = lo + (hi - lo) // 2
            yield from sort(lo, mid)
            yield from sort(mid + 1, hi)
            yield from merge(lo, hi, 1)

    return list(sort(0, n - 1))


KEY_VREGS = PEER_KEYS // SUBLANES
KEY_SORT = _sorting_network(KEY_VREGS)


def _top_values_keys(s, k):
    rows = [s[SUBLANES * r:SUBLANES * (r + 1)] for r in range(KEY_VREGS)]
    for i, j in KEY_SORT:
        rows[i], rows[j] = jnp.maximum(rows[i], rows[j]), jnp.minimum(rows[i], rows[j])
    vals = []
    for r in range(k):
        m = jnp.max(rows[0], axis=0, keepdims=True)
        vals.append(m)
        still_needed = k - 1 - r
        if still_needed == 0:
            break
        hit = rows[0] >= m
        for d in range(min(still_needed, KEY_VREGS)):
            below = rows[d + 1] if d + 1 < KEY_VREGS else NEG_INF
            rows[d] = jnp.where(hit, below, rows[d])
    return vals


def _ranked_weights(s, k):
    vals = _top_values_keys(s, k)
    nums = [jnp.ones_like(vals[0])]
    for r in range(1, k):
        v = _bf16_value(jnp.exp(jnp.maximum(vals[r] - vals[0], EXP_FLOOR)))
        nums.append(jnp.minimum(v, _bf16_below(nums[-1])))
    placed = jnp.zeros(s.shape, F32)
    for r in range(k):
        placed = jnp.where(s == vals[r], nums[r], placed)
    return vals, nums, placed


def _peer_route_kernel(x_ref, g_ref, wq_ref, sk_ref, ht_ref, ecut_ref, e0_ref, e1_ref, e1_scr):
    h = _rms(x_ref[...], g_ref[...])
    ht_ref[...] = h.T.astype(BF16)
    q = _dot(h.astype(BF16), wq_ref[...]).astype(BF16)
    t = h.shape[0]
    half = PEER_KEYS // 2
    for hd in range(PEER_HEADS):
        base = hd * 2 * PEER_KEYS
        s0_all = _dot_nt(sk_ref[0], q[:, base:base + PEER_KEYS])
        s1_all = _dot_nt(sk_ref[1], q[:, base + PEER_KEYS:base + 2 * PEER_KEYS])
        for tc in range(t // LANES):
            cols = slice(tc * LANES, (tc + 1) * LANES)
            s0 = s0_all[:, cols]
            u0 = _top_values_keys(s0, PEER_TOPK + 1)
            u1, n1, e1 = _ranked_weights(s1_all[:, cols], PEER_TOPK + 1)
            cands = [u0[a] + u1[b] for a in range(PEER_TOPK + 1) for b in range(PEER_TOPK + 1)
                     if (a + 1) * (b + 1) <= PEER_TOPK + 1]
            cands += [jnp.full((1, LANES), NEG_INF, F32)] * (N_CAND - len(cands))
            best = _top_values(jnp.concatenate(cands, axis=0), PEER_TOPK + 1)
            thr = 0.5 * (best[PEER_TOPK - 1] + best[PEER_TOPK])
            z = jnp.ones_like(best[0])
            for b in best[1:PEER_TOPK]:
                z = z + jnp.exp(b - best[0])
            cut = thr - s0
            ecut = jnp.full(cut.shape, 2.0, F32)
            for r in range(PEER_TOPK + 1):
                ecut = jnp.where(u1[r] > cut, n1[r], ecut)
            ecut_ref[hd, :, cols] = _bf16_pair_word(ecut)
            e0_ref[hd, :, cols] = _bf16_pair_word(
                _bf16_value(jnp.exp(jnp.maximum(s0 - u0[0], EXP_FLOOR)) / z))
            e1_scr[hd, tc] = e1
            lo = lax.bitcast_convert_type(e1_scr[hd, tc, pl.ds(0, half, stride=2), :], jnp.uint32) >> 16
            hi = (lax.bitcast_convert_type(e1_scr[hd, tc, pl.ds(1, half, stride=2), :], jnp.uint32)
                  & jnp.uint32(0xFFFF0000))
            e1_ref[hd, :, cols] = lax.bitcast_convert_type(hi | lo, jnp.int32)


def _peer_route(x, g, wq, sk):
    n = x.shape[0]
    t = _row_tile(n, PEER_T)
    full = lambda a: pl.BlockSpec(a.shape, lambda i: (0,) * a.ndim)
    rspec = pl.BlockSpec((PEER_HEADS, PEER_KEYS, t), lambda i: (0, 0, i))
    rshape = jax.ShapeDtypeStruct((PEER_HEADS, PEER_KEYS, n), jnp.int32)
    e1spec = pl.BlockSpec((PEER_HEADS, PEER_KEYS // 2, t), lambda i: (0, 0, i))
    e1shape = jax.ShapeDtypeStruct((PEER_HEADS, PEER_KEYS // 2, n), jnp.int32)
    return pl.pallas_call(
        _peer_route_kernel,
        grid=(n // t,),
        in_specs=[pl.BlockSpec((t, D_MODEL), lambda i: (i, 0)), full(g), full(wq), full(sk)],
        out_specs=[pl.BlockSpec((D_MODEL, t), lambda i: (0, i)), rspec, rspec, e1spec],
        out_shape=[jax.ShapeDtypeStruct((D_MODEL, n), BF16), rshape, rshape, e1shape],
        scratch_shapes=[pltpu.VMEM((PEER_HEADS, t // LANES, PEER_KEYS, LANES), F32)],
        compiler_params=_cparams(("parallel",)),
        name="peer_route",
    )(x, g, wq, sk)


def _peer_dense_kernel(ht_ref, ecut_ref, e0_ref, e1_ref, eu_ref, evt_ref, x_ref, *rest, final):
    if final:
        fg_ref, o_ref, acc_scr, at_scr, w_scr = rest
    else:
        o_ref, acc_scr, at_scr, w_scr = rest
    j = pl.program_id(1)
    rows_step = PEER_EC // PEER_KEYS
    rows_sub = PEER_SUB // PEER_KEYS

    @pl.when(j == 0)
    def _():
        acc_scr[...] = jnp.zeros(acc_scr.shape, F32)

    t = ht_ref.shape[1]

    def packed_row(row, cols):
        return pltpu.bitcast(jnp.broadcast_to(row[:, cols], (PEER_KEYS // 2, LANES)), BF16)

    def hidden_and_weights(sc):
        rows = slice(sc * PEER_SUB, (sc + 1) * PEER_SUB)
        at_scr[rows, :] = _dot(eu_ref[rows, :], ht_ref[...]).astype(BF16)
        for ii in range(rows_sub):
            i0 = j * rows_step + sc * rows_sub + ii
            r0 = sc * PEER_SUB + ii * PEER_KEYS
            ecut_rows = [ecut_ref[hd, pl.ds(i0, 1), :] for hd in range(PEER_HEADS)]
            e0_rows = [e0_ref[hd, pl.ds(i0, 1), :] for hd in range(PEER_HEADS)]
            for tc in range(t // LANES):
                cols = slice(tc * LANES, (tc + 1) * LANES)
                w = None
                for hd in range(PEER_HEADS):
                    e1 = pltpu.bitcast(e1_ref[hd, :, cols], BF16)
                    term = jnp.where(e1 >= packed_row(ecut_rows[hd], cols), e1, jnp.zeros_like(e1))
                    term = term * packed_row(e0_rows[hd], cols)
                    w = term if w is None else w + term
                w_scr[r0:r0 + PEER_KEYS, cols] = w

    def expert_output(sc):
        rows = slice(sc * PEER_SUB, (sc + 1) * PEER_SUB)
        return _dot(evt_ref[:, rows], w_scr[rows, :] * jax.nn.gelu(at_scr[rows, :]))

    n_sub = PEER_EC // PEER_SUB
    hidden_and_weights(0)
    total = None
    for sc in range(n_sub):
        if sc + 1 < n_sub:
            hidden_and_weights(sc + 1)
        part = expert_output(sc)
        total = part if total is None else total + part
    acc_scr[...] += total

    @pl.when(j == pl.num_programs(1) - 1)
    def _():
        y = x_ref[...] + acc_scr[...].T
        o_ref[...] = _rms(y, fg_ref[...]) if final else y


def _peer_dense(x, ht, ecut, e0, e1, eu, evt, final_g=None):
    n = x.shape[0]
    t = _row_tile(n, PEER_T)
    rspec = pl.BlockSpec((PEER_HEADS, PEER_KEYS, t), lambda i, j: (0, 0, i))
    e1spec = pl.BlockSpec((PEER_HEADS, PEER_KEYS // 2, t), lambda i, j: (0, 0, i))
    xspec = pl.BlockSpec((t, D_MODEL), lambda i, j: (i, 0))
    final = final_g is not None
    extra = [final_g] if final else []
    return pl.pallas_call(
        functools.partial(_peer_dense_kernel, final=final),
        grid=(n // t, PEER_EXPERTS // PEER_EC),
        in_specs=[pl.BlockSpec((D_MODEL, t), lambda i, j: (0, i)), rspec, rspec, e1spec,
                  pl.BlockSpec((PEER_EC, D_MODEL), lambda i, j: (j, 0)),
                  pl.BlockSpec((D_MODEL, PEER_EC), lambda i, j: (0, j)), xspec]
        + [pl.BlockSpec(g.shape, lambda i, j: (0, 0)) for g in extra],
        out_specs=xspec,
        out_shape=jax.ShapeDtypeStruct((n, D_MODEL), F32),
        scratch_shapes=[pltpu.VMEM((D_MODEL, t), F32), pltpu.VMEM((PEER_EC, t), BF16),
                        pltpu.VMEM((PEER_EC, t), BF16)],
        compiler_params=_cparams(("parallel", "arbitrary")),
        name="peer_dense",
    )(ht, ecut, e0, e1, eu, evt, x, *extra)


def _cd_in_kernel(x_ref, g_ref, wc_ref, wg_ref, wx_ref, wd_ref, lg_ref, lb_ref,
                  u_ref, v_ref, gate_ref, xbc_ref, dt_ref):
    h = _rms(x_ref[...], g_ref[...]).astype(BF16)
    u_ref[...] = jax.nn.gelu(_dot(h, wc_ref[:, 0:512]))
    v_ref[...] = _layernorm(jax.nn.gelu(_dot(h, wc_ref[:, 512:1024])), lg_ref[...], lb_ref[...])
    gate_ref[...] = _dot(h, wg_ref[...])
    xbc_ref[...] = _dot(h, wx_ref[...])
    dt_ref[...] = _dot(h, wd_ref[...])


def _cd_in(x, g, wc, wg, wx, wd, lg, lb):
    n = x.shape[0]
    tm = _row_tile(n)
    row = lambda c: pl.BlockSpec((tm, c), lambda i: (i, 0))
    full = lambda a: pl.BlockSpec(a.shape, lambda i: (0,) * a.ndim)
    widths = (512, 512, 512, SSM_CONV_CH, LANES)
    return pl.pallas_call(
        _cd_in_kernel,
        grid=(n // tm,),
        in_specs=[row(D_MODEL), full(g), full(wc), full(wg), full(wx), full(wd), full(lg), full(lb)],
        out_specs=[row(c) for c in widths],
        out_shape=[jax.ShapeDtypeStruct((n, c), F32) for c in widths],
        compiler_params=_cparams(("parallel",)),
        name="cd_in",
    )(x, g, wc, wg, wx, wd, lg, lb)


def _gmlp_kernel(u_ref, v_ref, ws_ref, bs_ref, o_ref, *, gm_len, n_chunks):
    r = lax.broadcasted_iota(jnp.int32, (gm_len, gm_len), 0)
    c = lax.broadcasted_iota(jnp.int32, (gm_len, gm_len), 1)
    for g in range(GM_GROUPS):
        w = jnp.where(r >= c, ws_ref[g], 0.0).astype(BF16)
        ch = slice(g * LANES, (g + 1) * LANES)
        for ci in range(n_chunks):
            rows = slice(ci * gm_len, (ci + 1) * gm_len)
            mixed = _dot(w, v_ref[0, rows, ch].astype(BF16)) + bs_ref[:, g:g + 1]
            o_ref[0, rows, ch] = (u_ref[0, rows, ch] * mixed).astype(BF16)


def _gmlp(u, v, ws, bs_t, gm_len):
    bsz, l, _ = u.shape
    tl = _row_tile(l)
    full = lambda a: pl.BlockSpec(a.shape, lambda bi, i: (0,) * a.ndim)
    spec = pl.BlockSpec((1, tl, GM_WIDTH), lambda bi, i: (bi, i, 0))
    return pl.pallas_call(
        functools.partial(_gmlp_kernel, gm_len=gm_len, n_chunks=tl // gm_len),
        grid=(bsz, l // tl),
        in_specs=[spec, spec, full(ws), full(bs_t)],
        out_specs=spec,
        out_shape=jax.ShapeDtypeStruct((bsz, l, GM_WIDTH), BF16),
        compiler_params=_cparams(("parallel", "parallel")),
        name="gmlp",
    )(u, v, ws, bs_t)


SSD_HALO = 8
SSD_PAIRS = SSM_HEADS // 2
SSD_GROUP_W = SSM_INNER // 2


def _ssd_kernel(xbc_ref, halo_ref, ctx_ref, gate_ref, dt_ref, h0_ref, cw_ref, cb_ref, dtb_ref, alog_ref,
                dsk_ref, ng_ref, y_ref, fin_ref, xp_scr, st_scr, *, tl):
    i = pl.program_id(1)
    pad = SSD_HALO - (SSM_CONV_K - 1)
    q = CHUNK

    @pl.when(i == 0)
    def _():
        xp_scr[pad:SSD_HALO, :] = ctx_ref[0]
        for k in range(SSD_PAIRS):
            st_scr[k] = jnp.concatenate([h0_ref[0, 2 * k], h0_ref[0, 2 * k + 1]], axis=0).T

    @pl.when(i > 0)
    def _():
        xp_scr[0:SSD_HALO, :] = halo_ref[0]

    xp_scr[SSD_HALO:SSD_HALO + tl, :] = xbc_ref[0]

    lane128 = lax.broadcasted_iota(jnp.int32, (q, LANES), 1)
    row128 = lax.broadcasted_iota(jnp.int32, (q, LANES), 0)
    er = lax.broadcasted_iota(jnp.int32, (LANES, SSM_INNER), 0)
    ec = lax.broadcasted_iota(jnp.int32, (LANES, SSM_INNER), 1)
    expand = jnp.where((ec >> 6) == er, 1.0, 0.0).astype(F32)
    tr = lax.broadcasted_iota(jnp.int32, (q, q), 0)
    tc = lax.broadcasted_iota(jnp.int32, (q, q), 1)
    ltri = jnp.where(tr >= tc, 1.0, 0.0).astype(F32)
    a_neg = -jnp.exp(alog_ref[...])
    zeros_q = jnp.zeros((q, LANES), F32)

    for ci in range(tl // q):
        r0 = ci * q
        acc = jnp.broadcast_to(cb_ref[...], (q, SSM_CONV_CH))
        for k in range(SSM_CONV_K):
            acc = acc + cw_ref[k:k + 1, :] * xp_scr[r0 + pad + k:r0 + pad + k + q, :]
        xc = _silu(acc)
        xs = xc[:, 0:SSM_INNER]
        dt = jnp.where(lane128 < SSM_HEADS, _softplus(dt_ref[0, r0:r0 + q, :] + dtb_ref[...]), 0.0)
        acs = _dot_f32(ltri, dt * a_neg)
        acs_e = _dot_f32(acs, expand)
        dt_e = _dot_f32(dt, expand)
        tot_e = acs_e[q - 1:q, :]
        xdt = xs * dt_e
        xd = (xdt * jnp.exp(tot_e - acs_e)).astype(BF16)
        eacs = jnp.exp(acs_e)
        cdec = jnp.exp(tot_e)
        ys = []
        for k in range(SSD_PAIRS):
            g = k // 2
            blk = slice(k * LANES, (k + 1) * LANES)
            bm = xc[:, SSM_INNER + g * SSM_STATE:SSM_INNER + (g + 1) * SSM_STATE]
            cm = xc[:, SSM_INNER + 2 * SSM_STATE + g * SSM_STATE:
                    SSM_INNER + 2 * SSM_STATE + (g + 1) * SSM_STATE].astype(BF16)
            cb2 = _dot_nt(cm, jnp.concatenate([bm, bm], axis=0).astype(BF16))
            a_blk = acs_e[:, blk]
            a_row = jnp.sum(jnp.where(row128 == (lane128 & (q - 1)), a_blk, 0.0), axis=0, keepdims=True)
            lmat = jnp.exp(jnp.where((lane128 & (q - 1)) <= row128, a_blk - a_row, NEG_INF))
            sc = (cb2 * lmat).astype(BF16)
            x_blk = xdt[:, blk]
            rhs = jnp.concatenate([jnp.where(lane128 < q, x_blk, 0.0), jnp.where(lane128 >= q, x_blk, 0.0)],
                                  axis=0).astype(BF16)
            prev = st_scr[k]
            y_pair = _dot(sc, rhs) + _dot(cm, prev.astype(BF16)) * eacs[:, blk]
            ys.append(y_pair)
            bt = jnp.concatenate([bm, zeros_q], axis=0).T.astype(BF16)
            xd_pad = jnp.concatenate([xd[:, blk], zeros_q.astype(BF16)], axis=0)
            st_scr[k] = prev * cdec[:, blk] + _dot(bt, xd_pad)
        y = jnp.concatenate(ys, axis=-1) + dsk_ref[...] * xs
        y = y * _silu(gate_ref[0, r0:r0 + q, :])
        outs = []
        for g in range(2):
            seg = y[:, g * SSD_GROUP_W:(g + 1) * SSD_GROUP_W]
            outs.append(seg * lax.rsqrt(jnp.mean(seg * seg, axis=-1, keepdims=True) + 1e-6))
        y_ref[0, r0:r0 + q, :] = (jnp.concatenate(outs, axis=-1) * ng_ref[...]).astype(BF16)

    @pl.when(i == pl.num_programs(1) - 1)
    def _():
        for k in range(SSD_PAIRS):
            st = st_scr[k].T
            fin_ref[0, 2 * k] = st[0:SSM_HEAD_DIM]
            fin_ref[0, 2 * k + 1] = st[SSM_HEAD_DIM:2 * SSM_HEAD_DIM]


def _ssd(xbc, ctx, gate, dt, h0, cw, cb, dtb, alog, dsk, ng):
    bsz, l, _ = xbc.shape
    tl = 256 if l % 256 == 0 else l
    hb = tl // SSD_HALO
    full = lambda a: pl.BlockSpec(a.shape, lambda bi, i: (0,) * a.ndim)
    tile = lambda c: pl.BlockSpec((1, tl, c), lambda bi, i: (bi, i, 0))
    stspec = pl.BlockSpec((1, SSM_HEADS, SSM_HEAD_DIM, SSM_STATE), lambda bi, i: (bi, 0, 0, 0))
    return pl.pallas_call(
        functools.partial(_ssd_kernel, tl=tl),
        grid=(bsz, l // tl),
        in_specs=[tile(SSM_CONV_CH),
                  pl.BlockSpec((1, SSD_HALO, SSM_CONV_CH), lambda bi, i: (bi, jnp.maximum(i * hb - 1, 0), 0)),
                  pl.BlockSpec((1, SSM_CONV_K - 1, SSM_CONV_CH), lambda bi, i: (bi, 0, 0)),
                  tile(SSM_INNER), tile(LANES), stspec,
                  full(cw), full(cb), full(dtb), full(alog), full(dsk), full(ng)],
        out_specs=[tile(SSM_INNER), stspec],
        out_shape=[jax.ShapeDtypeStruct((bsz, l, SSM_INNER), BF16),
                   jax.ShapeDtypeStruct((bsz, SSM_HEADS, SSM_HEAD_DIM, SSM_STATE), F32)],
        scratch_shapes=[pltpu.VMEM((SSD_HALO + tl, SSM_CONV_CH), F32),
                        pltpu.VMEM((SSD_PAIRS, SSM_STATE, LANES), F32)],
        compiler_params=_cparams(("parallel", "arbitrary")),
        name="ssd",
    )(xbc, xbc, ctx, gate, dt, h0, cw, cb, dtb, alog, dsk, ng)


def _pad_lanes(a, width=LANES):
    return jnp.pad(a, ((0, 0), (0, width - a.shape[-1])))


def _trunk(x, mem_k, mem_v, attn_k, attn_v, conv_a, ssd_st, conv_ssm, gm_len, p):
    bsz, l, _ = x.shape
    n = bsz * l
    row = lambda a: a.reshape(1, -1)
    x2 = x.reshape(n, D_MODEL)

    lam_init = 0.8 - 0.6 * math.exp(-0.3 * 0)
    glu, q16, k32, v32, k16, v16 = _ab_in(x2, row(p["norm_mix_g"][0]), p["w_in_ab"])
    glu3 = glu.reshape(bsz, l, CONV_CH)
    ctx_a = jnp.zeros((bsz, CONV_K - 1, CONV_CH), F32) if conv_a is None else conv_a[0]
    ca = _conv_a(glu3, ctx_a, p["conv_a_w"], row(p["conv_a_b"]), row(p["ln_a_g"]), row(p["ln_a_b"]))
    lam_args = (row(p["lam_q1"]), row(p["lam_k1"]), row(p["lam_q2"]), row(p["lam_k2"]), row(p["subln_g"]))
    shp3 = lambda a: a.reshape(bsz, l, DA_QK)
    if attn_k is None:
        o = _attn_prompt(shp3(q16), shp3(k16), shp3(v16), *lam_args, lam_init)
    else:
        past = attn_k.shape[2]
        o = _attn_sample(shp3(q16), shp3(k16), shp3(v16), attn_k[0].reshape(bsz, past, DA_QK),
                         attn_v[0].reshape(bsz, past, DA_QK), *lam_args, lam_init)
    new_k = k32.reshape(1, bsz, l, DA_HEADS, 2 * DA_HEAD_DIM)
    new_v = v32.reshape(1, bsz, l, DA_HEADS, 2 * DA_HEAD_DIM)
    new_ca = glu3[:, l - (CONV_K - 1):][None]

    def tail(x2, a, b, w_mix, layer, final_g=None):
        x3 = _cross(x2.reshape(bsz, l, D_MODEL), a, b, w_mix, row(p["norm_cross_g"][layer]), p["w_xq"][layer],
                    p["w_xo"][layer], mem_k[layer], mem_v[layer]).reshape(n, D_MODEL)
        routed = _peer_route(x3, row(p["norm_ffn_g"][layer]), p["w_pq"][layer], p["sub_keys"][layer])
        return _peer_dense(x3, *routed, p["expert_u"][layer], p["expert_vt"][layer], final_g)

    x2 = tail(x2, ca, o, p["w_out_ab"], 0)

    u, vln, gate, xbc, dt = _cd_in(x2, row(p["norm_mix_g"][1]), p["w_cd_c"], p["w_cd_gate"], p["w_cd_xbc"],
                                   p["w_cd_dt"], row(p["ln_c_g"]), row(p["ln_c_b"]))
    shp = lambda a: a.reshape(bsz, l, a.shape[-1])
    c_out = _gmlp(shp(u), shp(vln), p["gm_w_s"][:, :gm_len, :gm_len], p["gm_b_s"][:, :gm_len].T, gm_len)
    xbc3 = shp(xbc)
    ctx_d = jnp.zeros((bsz, SSM_CONV_K - 1, SSM_CONV_CH), F32) if conv_ssm is None else conv_ssm[0]
    h0 = jnp.zeros((bsz, SSM_HEADS, SSM_HEAD_DIM, SSM_STATE), F32) if ssd_st is None else ssd_st[0]
    y, fin = _ssd(xbc3, ctx_d, shp(gate), shp(dt), h0, p["conv_d_w"], row(p["conv_d_b"]),
                  _pad_lanes(row(p["dt_bias"])), _pad_lanes(row(p["a_log"])),
                  row(jnp.repeat(p["d_skip"], SSM_HEAD_DIM)), row(p["norm_d_g"]))
    y_out = tail(x2, c_out, y, p["w_out_cd"], 1, row(p["norm_final_g"])).reshape(bsz, l, D_MODEL)
    new_gv = vln.reshape(1, bsz, l, GM_GROUPS, GM_WIDTH // GM_GROUPS)
    new_cs = xbc3[:, l - (SSM_CONV_K - 1):][None]
    return y_out, new_k, new_v, new_ca, new_gv, fin[None], new_cs


def kernel(x_prompt, x_sample, cache_attn_k, cache_attn_v, state_conv_a, state_ssd, state_conv_ssm, cache_mem_k, cache_mem_v, mem_prompt, norm_mix_g, norm_cross_g, norm_ffn_g, norm_final_g, w_in_ab, conv_a_w, conv_a_b, ln_a_g, ln_a_b, lam_q1, lam_k1, lam_q2, lam_k2, subln_g, w_out_ab, w_in_cd, ln_c_g, ln_c_b, gm_w_s, gm_b_s, conv_d_w, conv_d_b, dt_bias, a_log, d_skip, norm_d_g, w_out_cd, w_xq, w_xk, w_xv, w_xo, w_pq, sub_keys, expert_u, expert_v):
    bf = lambda a: a.astype(BF16)
    w_cd = w_in_cd[0]
    p = {
        "norm_mix_g": norm_mix_g, "norm_cross_g": norm_cross_g, "norm_ffn_g": norm_ffn_g,
        "norm_final_g": norm_final_g,
        "w_in_ab": bf(w_in_ab[0]), "conv_a_w": conv_a_w[0], "conv_a_b": conv_a_b[0],
        "ln_a_g": ln_a_g[0], "ln_a_b": ln_a_b[0],
        "lam_q1": lam_q1[0], "lam_k1": lam_k1[0], "lam_q2": lam_q2[0], "lam_k2": lam_k2[0],
        "subln_g": subln_g[0], "w_out_ab": bf(w_out_ab[0]),
        "w_cd_c": bf(w_cd[:, 0:1024]), "w_cd_gate": bf(w_cd[:, 1024:1536]), "w_cd_xbc": bf(w_cd[:, 1536:2560]),
        "w_cd_dt": bf(_pad_lanes(w_cd[:, 2560:2568])),
        "ln_c_g": ln_c_g[0], "ln_c_b": ln_c_b[0], "gm_w_s": gm_w_s[0], "gm_b_s": gm_b_s[0],
        "conv_d_w": conv_d_w[0], "conv_d_b": conv_d_b[0], "dt_bias": dt_bias[0], "a_log": a_log[0],
        "d_skip": d_skip[0], "norm_d_g": norm_d_g[0], "w_out_cd": bf(w_out_cd[0]),
        "w_xq": bf(w_xq), "w_xo": bf(w_xo), "w_pq": bf(w_pq), "sub_keys": bf(sub_keys),
        "expert_u": bf(expert_u), "expert_vt": jnp.swapaxes(bf(expert_v), 1, 2),
    }
    bsz, seq, _ = x_prompt.shape
    dec_b, dec_l, _ = x_sample.shape
    depth = w_xk.shape[0]

    mk32, mv32, mk16, mv16 = _mem_kv(mem_prompt.reshape(bsz * N_MEM, D_MODEL), bf(w_xk), bf(w_xv))
    mem_k_p = mk32.reshape(depth, bsz, N_MEM, X_HEADS, X_HEAD_DIM)
    mem_v_p = mv32.reshape(depth, bsz, N_MEM, X_HEADS, X_HEAD_DIM)
    y_prompt, kp, vp, cap, _, ssdp, csp = _trunk(
        x_prompt, mk16.reshape(depth, bsz, N_MEM, D_MODEL), mv16.reshape(depth, bsz, N_MEM, D_MODEL),
        None, None, None, None, None, 2 * CHUNK, p)

    y_sample, ks, vs, cas, gvs, ssds, css = _trunk(
        x_sample, cache_mem_k.reshape(depth, dec_b, N_MEM, D_MODEL), cache_mem_v.reshape(depth, dec_b, N_MEM, D_MODEL),
        cache_attn_k, cache_attn_v, state_conv_a, state_ssd, state_conv_ssm, dec_l, p)
    return (y_prompt, y_sample, kp, vp, cap, ssdp, csp, mem_k_p, mem_v_p, ks, vs, cas, gvs, ssds, css)
```

```python
import functools
import math

import jax
import jax.numpy as jnp
from jax import lax
from jax.experimental import pallas as pl
from jax.experimental.pallas import tpu as pltpu

F32 = jnp.float32
BF16 = jnp.bfloat16
NEG_INF = float("-inf")

D_MODEL = 1024
CHUNK = 64
CONV_CH = 512
CONV_K = 31
DA_HEADS = 4
DA_HEAD_DIM = 64
DA_QK = 512
GM_WIDTH = 512
GM_GROUPS = 4
SSM_INNER = 512
SSM_HEADS = 8
SSM_HEAD_DIM = 64
SSM_STATE = 128
SSM_CONV_K = 4
SSM_CONV_CH = 1024
N_MEM = 256
X_HEADS = 4
X_HEAD_DIM = 256
PEER_HEADS = 8
PEER_KEYS = 128
PEER_EXPERTS = PEER_KEYS * PEER_KEYS
PEER_TOPK = 16
LANES = 128
SUBLANES = 8
BF16_ROWS = 16
VMEM_LIMIT = 56 * 1024 * 1024


def _cparams(sem):
    return pltpu.CompilerParams(dimension_semantics=sem, vmem_limit_bytes=VMEM_LIMIT)


def _dot(a, b):
    return jnp.dot(a, b, preferred_element_type=F32)


def _dot_nt(a, b):
    return lax.dot_general(a, b, (((1,), (1,)), ((), ())), preferred_element_type=F32)


def _dot_f32(a, b):
    return jnp.dot(a, b, preferred_element_type=F32, precision=lax.Precision.HIGHEST)


def _rms(x, g, eps=1e-6):
    return x * lax.rsqrt(jnp.mean(x * x, axis=-1, keepdims=True) + eps) * g


def _layernorm(x, g, b, eps=1e-5):
    xc = x - jnp.mean(x, axis=-1, keepdims=True)
    return xc * lax.rsqrt(jnp.mean(xc * xc, axis=-1, keepdims=True) + eps) * g + b


def _silu(x):
    return x * jax.nn.sigmoid(x)


def _softplus(x):
    return jnp.maximum(x, 0.0) + jnp.log(1.0 + jnp.exp(-jnp.abs(x)))


def _row_tile(n, pref=512):
    return pref if n % pref == 0 else n


def _ab_in_kernel(x_ref, g_ref, w_ref, glu_ref, q_ref, k32_ref, v32_ref, k16_ref, v16_ref):
    h = _rms(x_ref[...], g_ref[...]).astype(BF16)
    a_val = _dot(h, w_ref[:, 0:512])
    a_gate = _dot(h, w_ref[:, 512:1024])
    glu_ref[...] = a_val * jax.nn.sigmoid(a_gate)
    q = _dot(h, w_ref[:, 1024:1536])
    q_ref[...] = (q * (DA_HEAD_DIM ** -0.5 * math.log2(math.e))).astype(BF16)
    k = _dot(h, w_ref[:, 1536:2048])
    k32_ref[...] = k
    k16_ref[...] = k.astype(BF16)
    v = _dot(h, w_ref[:, 2048:2560])
    v32_ref[...] = v
    v16_ref[...] = v.astype(BF16)


def _ab_in(x, g, w):
    n = x.shape[0]
    tm = _row_tile(n)
    row = lambda c: pl.BlockSpec((tm, c), lambda i: (i, 0))
    full = lambda a: pl.BlockSpec(a.shape, lambda i: (0,) * a.ndim)
    return pl.pallas_call(
        _ab_in_kernel,
        grid=(n // tm,),
        in_specs=[row(D_MODEL), full(g), full(w)],
        out_specs=[row(512)] * 6,
        out_shape=[jax.ShapeDtypeStruct((n, 512), F32), jax.ShapeDtypeStruct((n, 512), BF16),
                   jax.ShapeDtypeStruct((n, 512), F32), jax.ShapeDtypeStruct((n, 512), F32),
                   jax.ShapeDtypeStruct((n, 512), BF16), jax.ShapeDtypeStruct((n, 512), BF16)],
        compiler_params=_cparams(("parallel",)),
        name="ab_in",
    )(x, g, w)


CONV_HALO = 32
CONV_RB = 64


def _conv_a_kernel(glu_ref, halo_ref, ctx_ref, w_ref, b_ref, lg_ref, lb_ref, o_ref, xp_scr, sh_scr, *, tl):
    i = pl.program_id(1)
    pad = CONV_HALO - (CONV_K - 1)

    @pl.when(i == 0)
    def _():
        xp_scr[0:pad, :] = jnp.zeros((pad, CONV_CH), F32)
        xp_scr[pad:CONV_HALO, :] = ctx_ref[0]

    @pl.when(i > 0)
    def _():
        xp_scr[0:CONV_HALO, :] = halo_ref[0]

    xp_scr[CONV_HALO:CONV_HALO + tl, :] = glu_ref[0]
    for s in range(1, SUBLANES):
        sh_scr[s - 1, 0:CONV_HALO + tl - SUBLANES, :] = xp_scr[s:s + CONV_HALO + tl - SUBLANES, :]
    rb = min(CONV_RB, tl)
    for r0 in range(0, tl, rb):
        acc = jnp.broadcast_to(b_ref[...], (rb, CONV_CH))
        for k in range(CONV_K):
            phase = (pad + k) % SUBLANES
            base = r0 + pad + k - phase
            rows = xp_scr[base:base + rb, :] if phase == 0 else sh_scr[phase - 1, base:base + rb, :]
            acc = acc + w_ref[k:k + 1, :] * rows
        y = _layernorm(acc, lg_ref[...], lb_ref[...])
        o_ref[0, r0:r0 + rb, :] = _silu(y).astype(BF16)


def _conv_a(glu, ctx, w, b, lg, lb):
    bsz, l, _ = glu.shape
    tl = _row_tile(l)
    hb = tl // CONV_HALO
    full = lambda a: pl.BlockSpec(a.shape, lambda bi, i: (0,) * a.ndim)
    return pl.pallas_call(
        functools.partial(_conv_a_kernel, tl=tl),
        grid=(bsz, l // tl),
        in_specs=[pl.BlockSpec((1, tl, CONV_CH), lambda bi, i: (bi, i, 0)),
                  pl.BlockSpec((1, CONV_HALO, CONV_CH), lambda bi, i: (bi, jnp.maximum(i * hb - 1, 0), 0)),
                  pl.BlockSpec((1, CONV_K - 1, CONV_CH), lambda bi, i: (bi, 0, 0)),
                  full(w), full(b), full(lg), full(lb)],
        out_specs=pl.BlockSpec((1, tl, CONV_CH), lambda bi, i: (bi, i, 0)),
        out_shape=jax.ShapeDtypeStruct((bsz, l, CONV_CH), BF16),
        scratch_shapes=[pltpu.VMEM((CONV_HALO + tl, CONV_CH), F32),
                        pltpu.VMEM((SUBLANES - 1, CONV_HALO + tl, CONV_CH), F32)],
        compiler_params=_cparams(("parallel", "arbitrary")),
        name="conv_a",
    )(glu, glu, ctx, w, b, lg, lb)


def _lambda(lq1, lk1, lq2, lk2, lam_init):
    return (jnp.exp(jnp.sum(lq1[...] * lk1[...], axis=-1, keepdims=True))
            - jnp.exp(jnp.sum(lq2[...] * lk2[...], axis=-1, keepdims=True)) + lam_init)


def _split_q(q):
    lane = lax.broadcasted_iota(jnp.int32, q.shape, 1)
    z = jnp.zeros_like(q)
    return jnp.concatenate([jnp.where(lane < DA_HEAD_DIM, q, z), jnp.where(lane >= DA_HEAD_DIM, q, z)], axis=0)


def _diff_finish(acc, l, tq, lam, sg, lam_init):
    o = acc[0:tq] / l[0:tq] - lam * (acc[tq:2 * tq] / l[tq:2 * tq])
    return _rms(o, sg) * (1.0 - lam_init)


ATTN_TQ = 512
ATTN_TK = 512


def _attn_prompt_kernel(qi_ref, ki_ref, fl_ref, q_ref, k_ref, v_ref, lq1, lk1, lq2, lk2, sg_ref, o_ref,
                        qs_scr, m_scr, acc_scr, *, tq, tk, lam_init):
    p = pl.program_id(1)
    qi = qi_ref[p]
    ki = ki_ref[p]
    flags = fl_ref[p]

    @pl.when(ki == 0)
    def _():
        for h in range(DA_HEADS):
            qs_scr[h] = _split_q(q_ref[0, :, h * LANES:(h + 1) * LANES])
        m_scr[...] = jnp.full(m_scr.shape, NEG_INF, F32)
        acc_scr[...] = jnp.zeros(acc_scr.shape, F32)

    ones_col = jnp.where(lax.broadcasted_iota(jnp.int32, (tk, LANES), 1) == 0, 1.0, 0.0).astype(BF16)

    def step(masked):
        for h in range(DA_HEADS):
            sl = slice(h * LANES, (h + 1) * LANES)
            s = _dot_nt(qs_scr[h], k_ref[0, :, sl])
            if masked:
                r = lax.broadcasted_iota(jnp.int32, s.shape, 0)
                c = lax.broadcasted_iota(jnp.int32, s.shape, 1)
                q_pos = qi * tq + jnp.where(r >= tq, r - tq, r)
                s = jnp.where(((ki * tk + c) >> 6) <= (q_pos >> 6), s, -1e30)
            m_old = m_scr[h]
            m_new = jnp.maximum(m_old, jnp.broadcast_to(jnp.max(s, axis=-1, keepdims=True), m_old.shape))
            alpha = jnp.exp2(m_old - m_new)
            pr = jnp.exp2(s - jnp.concatenate([m_new] * (tk // LANES), axis=1)).astype(BF16)
            v_aug = jnp.concatenate([v_ref[0, :, sl], ones_col], axis=1)
            acc_scr[h] = jnp.concatenate([alpha, alpha], axis=1) * acc_scr[h] + _dot(pr, v_aug)
            m_scr[h] = m_new

    @pl.when((flags & 1) == 0)
    def _():
        step(False)

    @pl.when((flags & 1) == 1)
    def _():
        step(True)

    @pl.when((flags & 2) != 0)
    def _():
        lam = _lambda(lq1, lk1, lq2, lk2, lam_init)
        for h in range(DA_HEADS):
            acc = acc_scr[h]
            o = _diff_finish(acc[:, 0:LANES], acc[:, LANES:LANES + 1], tq, lam, sg_ref[...], lam_init)
            o_ref[0, :, h * LANES:(h + 1) * LANES] = o.astype(BF16)


def _attn_prompt(q, k, v, lq1, lk1, lq2, lk2, sg, lam_init):
    bsz, l, _ = q.shape
    tq = _row_tile(l, ATTN_TQ)
    tk = _row_tile(l, ATTN_TK)
    pairs = []
    for a in range(l // tq):
        last = ((a + 1) * tq - 1) // tk
        for b in range(last + 1):
            crosses = (b + 1) * tk > a * tq + CHUNK
            pairs.append((a, b, int(crosses) + 2 * int(b == last)))
    tabs = [jnp.asarray([pr[i] for pr in pairs], jnp.int32) for i in range(3)]
    small = lambda a: pl.BlockSpec(a.shape, lambda bi, p, qt, kt, fl: (0,) * a.ndim)
    grid_spec = pltpu.PrefetchScalarGridSpec(
        num_scalar_prefetch=3,
        grid=(bsz, len(pairs)),
        in_specs=[pl.BlockSpec((1, tq, DA_QK), lambda bi, p, qt, kt, fl: (bi, qt[p], 0)),
                  pl.BlockSpec((1, tk, DA_QK), lambda bi, p, qt, kt, fl: (bi, kt[p], 0)),
                  pl.BlockSpec((1, tk, DA_QK), lambda bi, p, qt, kt, fl: (bi, kt[p], 0)),
                  small(lq1), small(lk1), small(lq2), small(lk2), small(sg)],
        out_specs=pl.BlockSpec((1, tq, DA_QK), lambda bi, p, qt, kt, fl: (bi, qt[p], 0)),
        scratch_shapes=[pltpu.VMEM((DA_HEADS, 2 * tq, LANES), BF16), pltpu.VMEM((DA_HEADS, 2 * tq, LANES), F32),
                        pltpu.VMEM((DA_HEADS, 2 * tq, 2 * LANES), F32)],
    )
    return pl.pallas_call(
        functools.partial(_attn_prompt_kernel, tq=tq, tk=tk, lam_init=lam_init),
        grid_spec=grid_spec,
        out_shape=jax.ShapeDtypeStruct((bsz, l, DA_QK), BF16),
        compiler_params=_cparams(("parallel", "arbitrary")),
        name="attn_prompt",
    )(*tabs, q, k, v, lq1, lk1, lq2, lk2, sg)


def _attn_sample_kernel(q_ref, kc_ref, vc_ref, kn_ref, vn_ref, lq1, lk1, lq2, lk2, sg_ref, o_ref,
                        *, tq, past_len, lam_init):
    qs = _split_q(q_ref[0])
    s_c = _dot_nt(qs, kc_ref[0].astype(BF16))
    s_n = _dot_nt(qs, kn_ref[0])
    r = lax.broadcasted_iota(jnp.int32, s_n.shape, 0)
    c = lax.broadcasted_iota(jnp.int32, s_n.shape, 1)
    qrow = jnp.where(r >= tq, r - tq, r)
    s_n = jnp.where(((past_len + c) >> 6) <= ((past_len + qrow) >> 6), s_n, -1e30)
    m = jnp.maximum(jnp.max(s_c, axis=-1, keepdims=True), jnp.max(s_n, axis=-1, keepdims=True))
    p_c = jnp.exp2(s_c - m)
    p_n = jnp.exp2(s_n - m)
    l = jnp.sum(p_c, axis=-1, keepdims=True) + jnp.sum(p_n, axis=-1, keepdims=True)
    acc = _dot(p_c.astype(BF16), vc_ref[0].astype(BF16)) + _dot(p_n.astype(BF16), vn_ref[0])
    lam = _lambda(lq1, lk1, lq2, lk2, lam_init)
    o_ref[0] = _diff_finish(acc, l, tq, lam, sg_ref[...], lam_init).astype(BF16)


def _attn_sample(q, k, v, k_past, v_past, lq1, lk1, lq2, lk2, sg, lam_init):
    bsz, l, _ = q.shape
    past_len = k_past.shape[1]
    small = lambda a: pl.BlockSpec(a.shape, lambda bi, h: (0,) * a.ndim)
    new = pl.BlockSpec((1, l, LANES), lambda bi, h: (bi, 0, h))
    old = pl.BlockSpec((1, past_len, LANES), lambda bi, h: (bi, 0, h))
    return pl.pallas_call(
        functools.partial(_attn_sample_kernel, tq=l, past_len=past_len, lam_init=lam_init),
        grid=(bsz, DA_HEADS),
        in_specs=[new, old, old, new, new, small(lq1), small(lk1), small(lq2), small(lk2), small(sg)],
        out_specs=new,
        out_shape=jax.ShapeDtypeStruct((bsz, l, DA_QK), BF16),
        compiler_params=_cparams(("parallel", "parallel")),
        name="attn_sample",
    )(q, k_past, v_past, k, v, lq1, lk1, lq2, lk2, sg)


def _mem_kv_kernel(m_ref, wk_ref, wv_ref, k32_ref, v32_ref, k16_ref, v16_ref):
    m = m_ref[...].astype(BF16)
    k = _dot(m, wk_ref[0])
    v = _dot(m, wv_ref[0])
    k32_ref[0] = k
    v32_ref[0] = v
    k16_ref[0] = k.astype(BF16)
    v16_ref[0] = v.astype(BF16)


def _mem_kv(mem, wk, wv):
    n = mem.shape[0]
    depth = wk.shape[0]
    tm = _row_tile(n)
    wspec = pl.BlockSpec((1, D_MODEL, D_MODEL), lambda l, i: (l, 0, 0))
    ospec = pl.BlockSpec((1, tm, D_MODEL), lambda l, i: (l, i, 0))
    return pl.pallas_call(
        _mem_kv_kernel,
        grid=(depth, n // tm),
        in_specs=[pl.BlockSpec((tm, D_MODEL), lambda l, i: (i, 0)), wspec, wspec],
        out_specs=[ospec] * 4,
        out_shape=[jax.ShapeDtypeStruct((depth, n, D_MODEL), F32)] * 2
        + [jax.ShapeDtypeStruct((depth, n, D_MODEL), BF16)] * 2,
        compiler_params=_cparams(("parallel", "parallel")),
        name="mem_kv",
    )(mem, wk, wv)


def _cross_kernel(x_ref, a_ref, b_ref, wm_ref, g_ref, wq_ref, wo_ref, mk_ref, mv_ref, o_ref):
    x = x_ref[0] + _dot(a_ref[0], wm_ref[0:512, :]) + _dot(b_ref[0], wm_ref[512:1024, :])
    h = _rms(x, g_ref[...]).astype(BF16)
    q = _dot(h, wq_ref[...]).astype(BF16)
    outs = []
    for hd in range(X_HEADS):
        sl = slice(hd * X_HEAD_DIM, (hd + 1) * X_HEAD_DIM)
        s = _dot_nt(q[:, sl], mk_ref[0, 0, :, sl].astype(BF16)) * (X_HEAD_DIM ** -0.5)
        s = s - jnp.max(s, axis=-1, keepdims=True)
        e = jnp.exp(s)
        p = (e / jnp.sum(e, axis=-1, keepdims=True)).astype(BF16)
        outs.append(_dot(p, mv_ref[0, 0, :, sl].astype(BF16)).astype(BF16))
    o = jnp.concatenate(outs, axis=-1)
    o_ref[0] = x + _dot(o, wo_ref[...])


def _cross(x, a, b, wm, g, wq, wo, mk, mv, layer):
    bsz, l, _ = x.shape
    tm = _row_tile(l)
    full = lambda arr: pl.BlockSpec(arr.shape, lambda bi, i: (0,) * arr.ndim)
    xs = pl.BlockSpec((1, tm, D_MODEL), lambda bi, i: (bi, i, 0))
    hs = pl.BlockSpec((1, tm, 512), lambda bi, i: (bi, i, 0))
    ms = pl.BlockSpec((1, 1, N_MEM, D_MODEL), lambda bi, i: (layer, bi, 0, 0))
    return pl.pallas_call(
        _cross_kernel,
        grid=(bsz, l // tm),
        in_specs=[xs, hs, hs, full(wm), full(g), full(wq), full(wo), ms, ms],
        out_specs=xs,
        out_shape=jax.ShapeDtypeStruct((bsz, l, D_MODEL), F32),
        compiler_params=_cparams(("parallel", "parallel")),
        name="cross_attn",
    )(x, a, b, wm, g, wq, wo, mk, mv)


PEER_T = 512
PEER_EC = 2048
PEER_SUB = 512
N_CAND = 56


def _top_values(s, k):
    out = []
    work = s
    for _ in range(k):
        m = jnp.max(work, axis=0, keepdims=True)
        out.append(m)
        work = jnp.where(work >= m, NEG_INF, work)
    return out


def _bf16_value(x):
    return x.astype(BF16).astype(F32)


def _bf16_below(x):
    return lax.bitcast_convert_type(lax.bitcast_convert_type(x, jnp.int32) - 0x10000, F32)


def _bf16_pair_word(x):
    bits = lax.bitcast_convert_type(x, jnp.uint32)
    return lax.bitcast_convert_type((bits & jnp.uint32(0xFFFF0000)) | (bits >> 16), jnp.int32)


EXP_FLOOR = -80.0


def _sorting_network(n):
    def merge(lo, hi, r):
        step = r * 2
        if step < hi - lo:
            yield from merge(lo, hi, step)
            yield from merge(lo + r, hi, step)
            yield from [(i, i + r) for i in range(lo + r, hi - r, step)]
        else:
            yield (lo, lo + r)

    def sort(lo, hi):
        if hi - lo >= 1:
            mid = lo + (hi - lo) // 2
            yield from sort(lo, mid)
            yield from sort(mid + 1, hi)
            yield from merge(lo, hi, 1)

    return list(sort(0, n - 1))


KEY_VREGS = PEER_KEYS // SUBLANES
KEY_SORT = _sorting_network(KEY_VREGS)


def _top_values_keys(s, k):
    rows = [s[SUBLANES * r:SUBLANES * (r + 1)] for r in range(KEY_VREGS)]
    for i, j in KEY_SORT:
        rows[i], rows[j] = jnp.maximum(rows[i], rows[j]), jnp.minimum(rows[i], rows[j])
    vals = []
    for r in range(k):
        m = jnp.max(rows[0], axis=0, keepdims=True)
        vals.append(m)
        still_needed = k - 1 - r
        if still_needed == 0:
            break
        hit = rows[0] >= m
        for d in range(min(still_needed, KEY_VREGS)):
            below = rows[d + 1] if d + 1 < KEY_VREGS else NEG_INF
            rows[d] = jnp.where(hit, below, rows[d])
    return vals


def _ranked_weights(s, k):
    vals = _top_values_keys(s, k)
    nums = [jnp.ones_like(vals[0])]
    for r in range(1, k):
        v = _bf16_value(jnp.exp(jnp.maximum(vals[r] - vals[0], EXP_FLOOR)))
        nums.append(jnp.minimum(v, _bf16_below(nums[-1])))
    placed = jnp.zeros(s.shape, F32)
    for r in range(k):
        placed = jnp.where(s == vals[r], nums[r], placed)
    return vals, nums, placed


def _peer_route_kernel(x_ref, g_ref, wq_ref, sk_ref, ht_ref, ecut_ref, e0_ref, e1_ref, e1_scr):
    h = _rms(x_ref[...], g_ref[...])
    ht_ref[...] = h.T.astype(BF16)
    q = _dot(h.astype(BF16), wq_ref[...]).astype(BF16)
    t = h.shape[0]
    half = PEER_KEYS // 2
    for hd in range(PEER_HEADS):
        base = hd * 2 * PEER_KEYS
        s0_all = _dot_nt(sk_ref[0], q[:, base:base + PEER_KEYS])
        s1_all = _dot_nt(sk_ref[1], q[:, base + PEER_KEYS:base + 2 * PEER_KEYS])
        for tc in range(t // LANES):
            cols = slice(tc * LANES, (tc + 1) * LANES)
            s0 = s0_all[:, cols]
            u0 = _top_values_keys(s0, PEER_TOPK + 1)
            u1, n1, e1 = _ranked_weights(s1_all[:, cols], PEER_TOPK + 1)
            cands = [u0[a] + u1[b] for a in range(PEER_TOPK + 1) for b in range(PEER_TOPK + 1)
                     if (a + 1) * (b + 1) <= PEER_TOPK + 1]
            cands += [jnp.full((1, LANES), NEG_INF, F32)] * (N_CAND - len(cands))
            best = _top_values(jnp.concatenate(cands, axis=0), PEER_TOPK + 1)
            thr = 0.5 * (best[PEER_TOPK - 1] + best[PEER_TOPK])
            z = jnp.ones_like(best[0])
            for b in best[1:PEER_TOPK]:
                z = z + jnp.exp(b - best[0])
            cut = thr - s0
            ecut = jnp.full(cut.shape, 2.0, F32)
            for r in range(PEER_TOPK + 1):
                ecut = jnp.where(u1[r] > cut, n1[r], ecut)
            ecut_ref[hd, :, cols] = _bf16_pair_word(ecut)
            e0_ref[hd, :, cols] = _bf16_pair_word(
                _bf16_value(jnp.exp(jnp.maximum(s0 - u0[0], EXP_FLOOR)) / z))
            e1_scr[hd, tc] = e1
            lo = lax.bitcast_convert_type(e1_scr[hd, tc, pl.ds(0, half, stride=2), :], jnp.uint32) >> 16
            hi = (lax.bitcast_convert_type(e1_scr[hd, tc, pl.ds(1, half, stride=2), :], jnp.uint32)
                  & jnp.uint32(0xFFFF0000))
            e1_ref[hd, :, cols] = lax.bitcast_convert_type(hi | lo, jnp.int32)


def _peer_route(x, g, wq, sk):
    n = x.shape[0]
    t = _row_tile(n, PEER_T)
    full = lambda a: pl.BlockSpec(a.shape, lambda i: (0,) * a.ndim)
    rspec = pl.BlockSpec((PEER_HEADS, PEER_KEYS, t), lambda i: (0, 0, i))
    rshape = jax.ShapeDtypeStruct((PEER_HEADS, PEER_KEYS, n), jnp.int32)
    e1spec = pl.BlockSpec((PEER_HEADS, PEER_KEYS // 2, t), lambda i: (0, 0, i))
    e1shape = jax.ShapeDtypeStruct((PEER_HEADS, PEER_KEYS // 2, n), jnp.int32)
    return pl.pallas_call(
        _peer_route_kernel,
        grid=(n // t,),
        in_specs=[pl.BlockSpec((t, D_MODEL), lambda i: (i, 0)), full(g), full(wq), full(sk)],
        out_specs=[pl.BlockSpec((D_MODEL, t), lambda i: (0, i)), rspec, rspec, e1spec],
        out_shape=[jax.ShapeDtypeStruct((D_MODEL, n), BF16), rshape, rshape, e1shape],
        scratch_shapes=[pltpu.VMEM((PEER_HEADS, t // LANES, PEER_KEYS, LANES), F32)],
        compiler_params=_cparams(("parallel",)),
        name="peer_route",
    )(x, g, wq, sk)


def _peer_dense_kernel(ht_ref, ecut_ref, e0_ref, e1_ref, eu_ref, evt_ref, x_ref, *rest, final):
    if final:
        fg_ref, o_ref, acc_scr, at_scr, w_scr = rest
    else:
        o_ref, acc_scr, at_scr, w_scr = rest
    j = pl.program_id(1)
    rows_step = PEER_EC // PEER_KEYS
    rows_sub = PEER_SUB // PEER_KEYS

    @pl.when(j == 0)
    def _():
        acc_scr[...] = jnp.zeros(acc_scr.shape, F32)

    t = ht_ref.shape[1]

    def packed_row(row, cols):
        return pltpu.bitcast(jnp.broadcast_to(row[:, cols], (PEER_KEYS // 2, LANES)), BF16)

    def hidden_and_weights(sc):
        rows = slice(sc * PEER_SUB, (sc + 1) * PEER_SUB)
        for ii in range(rows_sub):
            i0 = j * rows_step + sc * rows_sub + ii
            r0 = sc * PEER_SUB + ii * PEER_KEYS
            ecut_rows = [ecut_ref[hd, pl.ds(i0, 1), :] for hd in range(PEER_HEADS)]
            e0_rows = [e0_ref[hd, pl.ds(i0, 1), :] for hd in range(PEER_HEADS)]
            for tc in range(t // LANES):
                cols = slice(tc * LANES, (tc + 1) * LANES)
                w = None
                for hd in range(PEER_HEADS):
                    e1 = pltpu.bitcast(e1_ref[hd, :, cols], BF16)
                    term = jnp.where(e1 >= packed_row(ecut_rows[hd], cols), e1, jnp.zeros_like(e1))
                    term = term * packed_row(e0_rows[hd], cols)
                    w = term if w is None else w + term
                w_scr[r0:r0 + PEER_KEYS, cols] = w
        at_scr[rows, :] = _dot(eu_ref[rows, :], ht_ref[...]).astype(BF16)

    def expert_output(sc):
        rows = slice(sc * PEER_SUB, (sc + 1) * PEER_SUB)
        return _dot(evt_ref[0, :, rows], w_scr[rows, :] * jax.nn.gelu(at_scr[rows, :]))

    n_sub = PEER_EC // PEER_SUB
    hidden_and_weights(0)
    total = None
    for sc in range(n_sub):
        if sc + 1 < n_sub:
            hidden_and_weights(sc + 1)
        part = expert_output(sc)
        total = part if total is None else total + part
    acc_scr[...] += total

    @pl.when(j == pl.num_programs(1) - 1)
    def _():
        y = x_ref[...] + acc_scr[...].T
        o_ref[...] = _rms(y, fg_ref[...]) if final else y


def _peer_dense(x, ht, ecut, e0, e1, eu, evt, final_g=None):
    n = x.shape[0]
    t = _row_tile(n, PEER_T)
    rspec = pl.BlockSpec((PEER_HEADS, PEER_KEYS, t), lambda i, j: (0, 0, i))
    e1spec = pl.BlockSpec((PEER_HEADS, PEER_KEYS // 2, t), lambda i, j: (0, 0, i))
    xspec = pl.BlockSpec((t, D_MODEL), lambda i, j: (i, 0))
    final = final_g is not None
    extra = [final_g] if final else []
    return pl.pallas_call(
        functools.partial(_peer_dense_kernel, final=final),
        grid=(n // t, PEER_EXPERTS // PEER_EC),
        in_specs=[pl.BlockSpec((D_MODEL, t), lambda i, j: (0, i)), rspec, rspec, e1spec,
                  pl.BlockSpec((PEER_EC, D_MODEL), lambda i, j: (j, 0)),
                  pl.BlockSpec((1, D_MODEL, PEER_EC), lambda i, j: (j, 0, 0)), xspec]
        + [pl.BlockSpec(g.shape, lambda i, j: (0, 0)) for g in extra],
        out_specs=xspec,
        out_shape=jax.ShapeDtypeStruct((n, D_MODEL), F32),
        scratch_shapes=[pltpu.VMEM((D_MODEL, t), F32), pltpu.VMEM((PEER_EC, t), BF16),
                        pltpu.VMEM((PEER_EC, t), BF16)],
        compiler_params=_cparams(("parallel", "arbitrary")),
        name="peer_dense",
    )(ht, ecut, e0, e1, eu, evt, x, *extra)


def _cd_in_kernel(x_ref, g_ref, wc_ref, wg_ref, wx_ref, wd_ref, lg_ref, lb_ref,
                  u_ref, v_ref, gate_ref, xbc_ref, dt_ref):
    h = _rms(x_ref[...], g_ref[...]).astype(BF16)
    u_ref[...] = jax.nn.gelu(_dot(h, wc_ref[:, 0:512]))
    v_ref[...] = _layernorm(jax.nn.gelu(_dot(h, wc_ref[:, 512:1024])), lg_ref[...], lb_ref[...])
    gate_ref[...] = _dot(h, wg_ref[...])
    xbc_ref[...] = _dot(h, wx_ref[...])
    dt_ref[...] = _dot(h, wd_ref[...])


def _cd_in(x, g, wc, wg, wx, wd, lg, lb):
    n = x.shape[0]
    tm = _row_tile(n)
    row = lambda c: pl.BlockSpec((tm, c), lambda i: (i, 0))
    full = lambda a: pl.BlockSpec(a.shape, lambda i: (0,) * a.ndim)
    widths = (512, 512, 512, SSM_CONV_CH, LANES)
    return pl.pallas_call(
        _cd_in_kernel,
        grid=(n // tm,),
        in_specs=[row(D_MODEL), full(g), full(wc), full(wg), full(wx), full(wd), full(lg), full(lb)],
        out_specs=[row(c) for c in widths],
        out_shape=[jax.ShapeDtypeStruct((n, c), F32) for c in widths],
        compiler_params=_cparams(("parallel",)),
        name="cd_in",
    )(x, g, wc, wg, wx, wd, lg, lb)


def _gmlp_kernel(u_ref, v_ref, ws_ref, bs_ref, o_ref, *, gm_len, n_chunks):
    r = lax.broadcasted_iota(jnp.int32, (gm_len, gm_len), 0)
    c = lax.broadcasted_iota(jnp.int32, (gm_len, gm_len), 1)
    for g in range(GM_GROUPS):
        w = jnp.where(r >= c, ws_ref[g], 0.0).astype(BF16)
        ch = slice(g * LANES, (g + 1) * LANES)
        for ci in range(n_chunks):
            rows = slice(ci * gm_len, (ci + 1) * gm_len)
            mixed = _dot(w, v_ref[0, rows, ch].astype(BF16)) + bs_ref[:, g:g + 1]
            o_ref[0, rows, ch] = (u_ref[0, rows, ch] * mixed).astype(BF16)


def _gmlp(u, v, ws, bs_t, gm_len):
    bsz, l, _ = u.shape
    tl = _row_tile(l)
    full = lambda a: pl.BlockSpec(a.shape, lambda bi, i: (0,) * a.ndim)
    spec = pl.BlockSpec((1, tl, GM_WIDTH), lambda bi, i: (bi, i, 0))
    return pl.pallas_call(
        functools.partial(_gmlp_kernel, gm_len=gm_len, n_chunks=tl // gm_len),
        grid=(bsz, l // tl),
        in_specs=[spec, spec, full(ws), full(bs_t)],
        out_specs=spec,
        out_shape=jax.ShapeDtypeStruct((bsz, l, GM_WIDTH), BF16),
        compiler_params=_cparams(("parallel", "parallel")),
        name="gmlp",
    )(u, v, ws, bs_t)


SSD_HALO = 8
SSD_PAIRS = SSM_HEADS // 2
SSD_GROUP_W = SSM_INNER // 2


def _ssd_kernel(xbc_ref, halo_ref, ctx_ref, gate_ref, dt_ref, h0_ref, cw_ref, cb_ref, dtb_ref, alog_ref,
                dsk_ref, ng_ref, y_ref, fin_ref, xp_scr, st_scr, *, tl):
    i = pl.program_id(1)
    pad = SSD_HALO - (SSM_CONV_K - 1)
    q = CHUNK

    @pl.when(i == 0)
    def _():
        xp_scr[pad:SSD_HALO, :] = ctx_ref[0]
        for k in range(SSD_PAIRS):
            st_scr[k] = jnp.concatenate([h0_ref[0, 2 * k], h0_ref[0, 2 * k + 1]], axis=0).T

    @pl.when(i > 0)
    def _():
        xp_scr[0:SSD_HALO, :] = halo_ref[0]

    xp_scr[SSD_HALO:SSD_HALO + tl, :] = xbc_ref[0]

    lane128 = lax.broadcasted_iota(jnp.int32, (q, LANES), 1)
    row128 = lax.broadcasted_iota(jnp.int32, (q, LANES), 0)
    er = lax.broadcasted_iota(jnp.int32, (LANES, SSM_INNER), 0)
    ec = lax.broadcasted_iota(jnp.int32, (LANES, SSM_INNER), 1)
    expand = jnp.where((ec >> 6) == er, 1.0, 0.0).astype(F32)
    tr = lax.broadcasted_iota(jnp.int32, (q, q), 0)
    tc = lax.broadcasted_iota(jnp.int32, (q, q), 1)
    ltri = jnp.where(tr >= tc, 1.0, 0.0).astype(F32)
    a_neg = -jnp.exp(alog_ref[...])
    zeros_q = jnp.zeros((q, LANES), F32)

    for ci in range(tl // q):
        r0 = ci * q
        acc = jnp.broadcast_to(cb_ref[...], (q, SSM_CONV_CH))
        for k in range(SSM_CONV_K):
            acc = acc + cw_ref[k:k + 1, :] * xp_scr[r0 + pad + k:r0 + pad + k + q, :]
        xc = _silu(acc)
        xs = xc[:, 0:SSM_INNER]
        dt = jnp.where(lane128 < SSM_HEADS, _softplus(dt_ref[0, r0:r0 + q, :] + dtb_ref[...]), 0.0)
        acs = _dot_f32(ltri, dt * a_neg)
        acs_e = _dot_f32(acs, expand)
        dt_e = _dot_f32(dt, expand)
        tot_e = acs_e[q - 1:q, :]
        xdt = xs * dt_e
        xd = (xdt * jnp.exp(tot_e - acs_e)).astype(BF16)
        eacs = jnp.exp(acs_e)
        cdec = jnp.exp(tot_e)
        ys = []
        for k in range(SSD_PAIRS):
            g = k // 2
            blk = slice(k * LANES, (k + 1) * LANES)
            bm = xc[:, SSM_INNER + g * SSM_STATE:SSM_INNER + (g + 1) * SSM_STATE]
            cm = xc[:, SSM_INNER + 2 * SSM_STATE + g * SSM_STATE:
                    SSM_INNER + 2 * SSM_STATE + (g + 1) * SSM_STATE].astype(BF16)
            cb2 = _dot_nt(cm, jnp.concatenate([bm, bm], axis=0).astype(BF16))
            a_blk = acs_e[:, blk]
            a_row = jnp.sum(jnp.where(row128 == (lane128 & (q - 1)), a_blk, 0.0), axis=0, keepdims=True)
            lmat = jnp.exp(jnp.where((lane128 & (q - 1)) <= row128, a_blk - a_row, NEG_INF))
            sc = (cb2 * lmat).astype(BF16)
            x_blk = xdt[:, blk]
            rhs = jnp.concatenate([jnp.where(lane128 < q, x_blk, 0.0), jnp.where(lane128 >= q, x_blk, 0.0)],
                                  axis=0).astype(BF16)
            prev = st_scr[k]
            y_pair = _dot(sc, rhs) + _dot(cm, prev.astype(BF16)) * eacs[:, blk]
            ys.append(y_pair)
            bt = jnp.concatenate([bm, zeros_q], axis=0).T.astype(BF16)
            xd_pad = jnp.concatenate([xd[:, blk], zeros_q.astype(BF16)], axis=0)
            st_scr[k] = prev * cdec[:, blk] + _dot(bt, xd_pad)
        y = jnp.concatenate(ys, axis=-1) + dsk_ref[...] * xs
        y = y * _silu(gate_ref[0, r0:r0 + q, :])
        outs = []
        for g in range(2):
            seg = y[:, g * SSD_GROUP_W:(g + 1) * SSD_GROUP_W]
            outs.append(seg * lax.rsqrt(jnp.mean(seg * seg, axis=-1, keepdims=True) + 1e-6))
        y_ref[0, r0:r0 + q, :] = (jnp.concatenate(outs, axis=-1) * ng_ref[...]).astype(BF16)

    @pl.when(i == pl.num_programs(1) - 1)
    def _():
        for k in range(SSD_PAIRS):
            st = st_scr[k].T
            fin_ref[0, 2 * k] = st[0:SSM_HEAD_DIM]
            fin_ref[0, 2 * k + 1] = st[SSM_HEAD_DIM:2 * SSM_HEAD_DIM]


def _ssd(xbc, ctx, gate, dt, h0, cw, cb, dtb, alog, dsk, ng):
    bsz, l, _ = xbc.shape
    tl = 256 if l % 256 == 0 else l
    hb = tl // SSD_HALO
    full = lambda a: pl.BlockSpec(a.shape, lambda bi, i: (0,) * a.ndim)
    tile = lambda c: pl.BlockSpec((1, tl, c), lambda bi, i: (bi, i, 0))
    stspec = pl.BlockSpec((1, SSM_HEADS, SSM_HEAD_DIM, SSM_STATE), lambda bi, i: (bi, 0, 0, 0))
    return pl.pallas_call(
        functools.partial(_ssd_kernel, tl=tl),
        grid=(bsz, l // tl),
        in_specs=[tile(SSM_CONV_CH),
                  pl.BlockSpec((1, SSD_HALO, SSM_CONV_CH), lambda bi, i: (bi, jnp.maximum(i * hb - 1, 0), 0)),
                  pl.BlockSpec((1, SSM_CONV_K - 1, SSM_CONV_CH), lambda bi, i: (bi, 0, 0)),
                  tile(SSM_INNER), tile(LANES), stspec,
                  full(cw), full(cb), full(dtb), full(alog), full(dsk), full(ng)],
        out_specs=[tile(SSM_INNER), stspec],
        out_shape=[jax.ShapeDtypeStruct((bsz, l, SSM_INNER), BF16),
                   jax.ShapeDtypeStruct((bsz, SSM_HEADS, SSM_HEAD_DIM, SSM_STATE), F32)],
        scratch_shapes=[pltpu.VMEM((SSD_HALO + tl, SSM_CONV_CH), F32),
                        pltpu.VMEM((SSD_PAIRS, SSM_STATE, LANES), F32)],
        compiler_params=_cparams(("parallel", "arbitrary")),
        name="ssd",
    )(xbc, xbc, ctx, gate, dt, h0, cw, cb, dtb, alog, dsk, ng)


def _blocked_transpose(table):
    depth, _, d = table.shape
    return table.reshape(depth, PEER_EXPERTS // PEER_EC, PEER_EC, d).swapaxes(2, 3)


def _pad_lanes(a, width=LANES):
    return jnp.pad(a, ((0, 0), (0, width - a.shape[-1])))


def _trunk(x, mem_k, mem_v, attn_k, attn_v, conv_a, ssd_st, conv_ssm, gm_len, p):
    bsz, l, _ = x.shape
    n = bsz * l
    row = lambda a: a.reshape(1, -1)
    x2 = x.reshape(n, D_MODEL)

    lam_init = 0.8 - 0.6 * math.exp(-0.3 * 0)
    glu, q16, k32, v32, k16, v16 = _ab_in(x2, row(p["norm_mix_g"][0]), p["w_in_ab"])
    glu3 = glu.reshape(bsz, l, CONV_CH)
    ctx_a = jnp.zeros((bsz, CONV_K - 1, CONV_CH), F32) if conv_a is None else conv_a[0]
    ca = _conv_a(glu3, ctx_a, p["conv_a_w"], row(p["conv_a_b"]), row(p["ln_a_g"]), row(p["ln_a_b"]))
    lam_args = (row(p["lam_q1"]), row(p["lam_k1"]), row(p["lam_q2"]), row(p["lam_k2"]), row(p["subln_g"]))
    shp3 = lambda a: a.reshape(bsz, l, DA_QK)
    if attn_k is None:
        o = _attn_prompt(shp3(q16), shp3(k16), shp3(v16), *lam_args, lam_init)
    else:
        past = attn_k.shape[2]
        o = _attn_sample(shp3(q16), shp3(k16), shp3(v16), attn_k[0].reshape(bsz, past, DA_QK),
                         attn_v[0].reshape(bsz, past, DA_QK), *lam_args, lam_init)
    new_k = k32.reshape(1, bsz, l, DA_HEADS, 2 * DA_HEAD_DIM)
    new_v = v32.reshape(1, bsz, l, DA_HEADS, 2 * DA_HEAD_DIM)
    new_ca = glu3[:, l - (CONV_K - 1):][None]

    def tail(x2, a, b, w_mix, layer, final_g=None):
        x3 = _cross(x2.reshape(bsz, l, D_MODEL), a, b, w_mix, row(p["norm_cross_g"][layer]), p["w_xq"][layer],
                    p["w_xo"][layer], mem_k, mem_v, layer).reshape(n, D_MODEL)
        routed = _peer_route(x3, row(p["norm_ffn_g"][layer]), p["w_pq"][layer], p["sub_keys"][layer])
        return _peer_dense(x3, *routed, p["expert_u"][layer], p["expert_vt"][layer], final_g)

    x2 = tail(x2, ca, o, p["w_out_ab"], 0)

    u, vln, gate, xbc, dt = _cd_in(x2, row(p["norm_mix_g"][1]), p["w_cd_c"], p["w_cd_gate"], p["w_cd_xbc"],
                                   p["w_cd_dt"], row(p["ln_c_g"]), row(p["ln_c_b"]))
    shp = lambda a: a.reshape(bsz, l, a.shape[-1])
    c_out = _gmlp(shp(u), shp(vln), p["gm_w_s"][:, :gm_len, :gm_len], p["gm_b_s"][:, :gm_len].T, gm_len)
    xbc3 = shp(xbc)
    ctx_d = jnp.zeros((bsz, SSM_CONV_K - 1, SSM_CONV_CH), F32) if conv_ssm is None else conv_ssm[0]
    h0 = jnp.zeros((bsz, SSM_HEADS, SSM_HEAD_DIM, SSM_STATE), F32) if ssd_st is None else ssd_st[0]
    y, fin = _ssd(xbc3, ctx_d, shp(gate), shp(dt), h0, p["conv_d_w"], row(p["conv_d_b"]),
                  _pad_lanes(row(p["dt_bias"])), _pad_lanes(row(p["a_log"])),
                  row(jnp.repeat(p["d_skip"], SSM_HEAD_DIM)), row(p["norm_d_g"]))
    y_out = tail(x2, c_out, y, p["w_out_cd"], 1, row(p["norm_final_g"])).reshape(bsz, l, D_MODEL)
    new_gv = vln.reshape(1, bsz, l, GM_GROUPS, GM_WIDTH // GM_GROUPS)
    new_cs = xbc3[:, l - (SSM_CONV_K - 1):][None]
    return y_out, new_k, new_v, new_ca, new_gv, fin[None], new_cs


def kernel(x_prompt, x_sample, cache_attn_k, cache_attn_v, state_conv_a, state_ssd, state_conv_ssm, cache_mem_k, cache_mem_v, mem_prompt, norm_mix_g, norm_cross_g, norm_ffn_g, norm_final_g, w_in_ab, conv_a_w, conv_a_b, ln_a_g, ln_a_b, lam_q1, lam_k1, lam_q2, lam_k2, subln_g, w_out_ab, w_in_cd, ln_c_g, ln_c_b, gm_w_s, gm_b_s, conv_d_w, conv_d_b, dt_bias, a_log, d_skip, norm_d_g, w_out_cd, w_xq, w_xk, w_xv, w_xo, w_pq, sub_keys, expert_u, expert_v):
    bf = lambda a: a.astype(BF16)
    w_cd = w_in_cd[0]
    p = {
        "norm_mix_g": norm_mix_g, "norm_cross_g": norm_cross_g, "norm_ffn_g": norm_ffn_g,
        "norm_final_g": norm_final_g,
        "w_in_ab": bf(w_in_ab[0]), "conv_a_w": conv_a_w[0], "conv_a_b": conv_a_b[0],
        "ln_a_g": ln_a_g[0], "ln_a_b": ln_a_b[0],
        "lam_q1": lam_q1[0], "lam_k1": lam_k1[0], "lam_q2": lam_q2[0], "lam_k2": lam_k2[0],
        "subln_g": subln_g[0], "w_out_ab": bf(w_out_ab[0]),
        "w_cd_c": bf(w_cd[:, 0:1024]), "w_cd_gate": bf(w_cd[:, 1024:1536]), "w_cd_xbc": bf(w_cd[:, 1536:2560]),
        "w_cd_dt": bf(_pad_lanes(w_cd[:, 2560:2568])),
        "ln_c_g": ln_c_g[0], "ln_c_b": ln_c_b[0], "gm_w_s": gm_w_s[0], "gm_b_s": gm_b_s[0],
        "conv_d_w": conv_d_w[0], "conv_d_b": conv_d_b[0], "dt_bias": dt_bias[0], "a_log": a_log[0],
        "d_skip": d_skip[0], "norm_d_g": norm_d_g[0], "w_out_cd": bf(w_out_cd[0]),
        "w_xq": bf(w_xq), "w_xo": bf(w_xo), "w_pq": bf(w_pq), "sub_keys": bf(sub_keys),
        "expert_u": bf(expert_u), "expert_vt": _blocked_transpose(bf(expert_v)),
    }
    bsz, seq, _ = x_prompt.shape
    dec_b, dec_l, _ = x_sample.shape
    depth = w_xk.shape[0]

    mk32, mv32, mk16, mv16 = _mem_kv(mem_prompt.reshape(bsz * N_MEM, D_MODEL), bf(w_xk), bf(w_xv))
    mem_k_p = mk32.reshape(depth, bsz, N_MEM, X_HEADS, X_HEAD_DIM)
    mem_v_p = mv32.reshape(depth, bsz, N_MEM, X_HEADS, X_HEAD_DIM)
    y_prompt, kp, vp, cap, _, ssdp, csp = _trunk(
        x_prompt, mk16.reshape(depth, bsz, N_MEM, D_MODEL), mv16.reshape(depth, bsz, N_MEM, D_MODEL),
        None, None, None, None, None, 2 * CHUNK, p)

    y_sample, ks, vs, cas, gvs, ssds, css = _trunk(
        x_sample, cache_mem_k.reshape(depth, dec_b, N_MEM, D_MODEL), cache_mem_v.reshape(depth, dec_b, N_MEM, D_MODEL),
        cache_attn_k, cache_attn_v, state_conv_a, state_ssd, state_conv_ssm, dec_l, p)
    return (y_prompt, y_sample, kp, vp, cap, ssdp, csp, mem_k_p, mem_v_p, ks, vs, cas, gvs, ssds, css)
```

```python
import functools
import math

import jax
import jax.numpy as jnp
from jax import lax
from jax.experimental import pallas as pl
from jax.experimental.pallas import tpu as pltpu

F32 = jnp.float32
BF16 = jnp.bfloat16
NEG_INF = float("-inf")

D_MODEL = 1024
CHUNK = 64
CONV_CH = 512
CONV_K = 31
DA_HEADS = 4
DA_HEAD_DIM = 64
DA_QK = 512
GM_WIDTH = 512
GM_GROUPS = 4
SSM_INNER = 512
SSM_HEADS = 8
SSM_HEAD_DIM = 64
SSM_STATE = 128
SSM_CONV_K = 4
SSM_CONV_CH = 1024
N_MEM = 256
X_HEADS = 4
X_HEAD_DIM = 256
PEER_HEADS = 8
PEER_KEYS = 128
PEER_EXPERTS = PEER_KEYS * PEER_KEYS
PEER_TOPK = 16
LANES = 128
SUBLANES = 8
BF16_ROWS = 16
VMEM_LIMIT = 56 * 1024 * 1024


def _cparams(sem):
    return pltpu.CompilerParams(dimension_semantics=sem, vmem_limit_bytes=VMEM_LIMIT)


def _dot(a, b):
    return jnp.dot(a, b, preferred_element_type=F32)


def _dot_nt(a, b):
    return lax.dot_general(a, b, (((1,), (1,)), ((), ())), preferred_element_type=F32)


def _dot_f32(a, b):
    return jnp.dot(a, b, preferred_element_type=F32, precision=lax.Precision.HIGHEST)


def _rms(x, g, eps=1e-6):
    return x * lax.rsqrt(jnp.mean(x * x, axis=-1, keepdims=True) + eps) * g


def _layernorm(x, g, b, eps=1e-5):
    xc = x - jnp.mean(x, axis=-1, keepdims=True)
    return xc * lax.rsqrt(jnp.mean(xc * xc, axis=-1, keepdims=True) + eps) * g + b


def _silu(x):
    return x * jax.nn.sigmoid(x)


def _softplus(x):
    return jnp.maximum(x, 0.0) + jnp.log(1.0 + jnp.exp(-jnp.abs(x)))


def _row_tile(n, pref=512):
    return pref if n % pref == 0 else n


def _ab_in_kernel(x_ref, g_ref, w_ref, glu_ref, q_ref, k32_ref, v32_ref, k16_ref, v16_ref):
    h = _rms(x_ref[...], g_ref[...]).astype(BF16)
    a_val = _dot(h, w_ref[:, 0:512])
    a_gate = _dot(h, w_ref[:, 512:1024])
    glu_ref[...] = a_val * jax.nn.sigmoid(a_gate)
    q = _dot(h, w_ref[:, 1024:1536])
    q_ref[...] = (q * (DA_HEAD_DIM ** -0.5 * math.log2(math.e))).astype(BF16)
    k = _dot(h, w_ref[:, 1536:2048])
    k32_ref[...] = k
    k16_ref[...] = k.astype(BF16)
    v = _dot(h, w_ref[:, 2048:2560])
    v32_ref[...] = v
    v16_ref[...] = v.astype(BF16)


def _ab_in(x, g, w):
    n = x.shape[0]
    tm = _row_tile(n)
    row = lambda c: pl.BlockSpec((tm, c), lambda i: (i, 0))
    full = lambda a: pl.BlockSpec(a.shape, lambda i: (0,) * a.ndim)
    return pl.pallas_call(
        _ab_in_kernel,
        grid=(n // tm,),
        in_specs=[row(D_MODEL), full(g), full(w)],
        out_specs=[row(512)] * 6,
        out_shape=[jax.ShapeDtypeStruct((n, 512), F32), jax.ShapeDtypeStruct((n, 512), BF16),
                   jax.ShapeDtypeStruct((n, 512), F32), jax.ShapeDtypeStruct((n, 512), F32),
                   jax.ShapeDtypeStruct((n, 512), BF16), jax.ShapeDtypeStruct((n, 512), BF16)],
        compiler_params=_cparams(("parallel",)),
        name="ab_in",
    )(x, g, w)


CONV_HALO = 32
CONV_RB = 64


def _conv_a_kernel(glu_ref, halo_ref, ctx_ref, w_ref, b_ref, lg_ref, lb_ref, o_ref, xp_scr, sh_scr, *, tl):
    i = pl.program_id(1)
    pad = CONV_HALO - (CONV_K - 1)

    @pl.when(i == 0)
    def _():
        xp_scr[0:pad, :] = jnp.zeros((pad, CONV_CH), F32)
        xp_scr[pad:CONV_HALO, :] = ctx_ref[0]

    @pl.when(i > 0)
    def _():
        xp_scr[0:CONV_HALO, :] = halo_ref[0]

    xp_scr[CONV_HALO:CONV_HALO + tl, :] = glu_ref[0]
    for s in range(1, SUBLANES):
        sh_scr[s - 1, 0:CONV_HALO + tl - SUBLANES, :] = xp_scr[s:s + CONV_HALO + tl - SUBLANES, :]
    rb = min(CONV_RB, tl)
    for r0 in range(0, tl, rb):
        acc = jnp.broadcast_to(b_ref[...], (rb, CONV_CH))
        for k in range(CONV_K):
            phase = (pad + k) % SUBLANES
            base = r0 + pad + k - phase
            rows = xp_scr[base:base + rb, :] if phase == 0 else sh_scr[phase - 1, base:base + rb, :]
            acc = acc + w_ref[k:k + 1, :] * rows
        y = _layernorm(acc, lg_ref[...], lb_ref[...])
        o_ref[0, r0:r0 + rb, :] = _silu(y).astype(BF16)


def _conv_a(glu, ctx, w, b, lg, lb):
    bsz, l, _ = glu.shape
    tl = _row_tile(l)
    hb = tl // CONV_HALO
    full = lambda a: pl.BlockSpec(a.shape, lambda bi, i: (0,) * a.ndim)
    return pl.pallas_call(
        functools.partial(_conv_a_kernel, tl=tl),
        grid=(bsz, l // tl),
        in_specs=[pl.BlockSpec((1, tl, CONV_CH), lambda bi, i: (bi, i, 0)),
                  pl.BlockSpec((1, CONV_HALO, CONV_CH), lambda bi, i: (bi, jnp.maximum(i * hb - 1, 0), 0)),
                  pl.BlockSpec((1, CONV_K - 1, CONV_CH), lambda bi, i: (bi, 0, 0)),
                  full(w), full(b), full(lg), full(lb)],
        out_specs=pl.BlockSpec((1, tl, CONV_CH), lambda bi, i: (bi, i, 0)),
        out_shape=jax.ShapeDtypeStruct((bsz, l, CONV_CH), BF16),
        scratch_shapes=[pltpu.VMEM((CONV_HALO + tl, CONV_CH), F32),
                        pltpu.VMEM((SUBLANES - 1, CONV_HALO + tl, CONV_CH), F32)],
        compiler_params=_cparams(("parallel", "arbitrary")),
        name="conv_a",
    )(glu, glu, ctx, w, b, lg, lb)


def _lambda(lq1, lk1, lq2, lk2, lam_init):
    return (jnp.exp(jnp.sum(lq1[...] * lk1[...], axis=-1, keepdims=True))
            - jnp.exp(jnp.sum(lq2[...] * lk2[...], axis=-1, keepdims=True)) + lam_init)


def _split_q(q):
    lane = lax.broadcasted_iota(jnp.int32, q.shape, 1)
    z = jnp.zeros_like(q)
    return jnp.concatenate([jnp.where(lane < DA_HEAD_DIM, q, z), jnp.where(lane >= DA_HEAD_DIM, q, z)], axis=0)


def _diff_finish(acc, l, tq, lam, sg, lam_init):
    o = acc[0:tq] / l[0:tq] - lam * (acc[tq:2 * tq] / l[tq:2 * tq])
    return _rms(o, sg) * (1.0 - lam_init)


ATTN_TQ = 512
ATTN_TK = 512


def _attn_prompt_kernel(qi_ref, ki_ref, fl_ref, q_ref, k_ref, v_ref, lq1, lk1, lq2, lk2, sg_ref, o_ref,
                        qs_scr, m_scr, acc_scr, *, tq, tk, lam_init):
    p = pl.program_id(1)
    qi = qi_ref[p]
    ki = ki_ref[p]
    flags = fl_ref[p]

    @pl.when(ki == 0)
    def _():
        for h in range(DA_HEADS):
            qs_scr[h] = _split_q(q_ref[0, :, h * LANES:(h + 1) * LANES])
        m_scr[...] = jnp.full(m_scr.shape, NEG_INF, F32)
        acc_scr[...] = jnp.zeros(acc_scr.shape, F32)

    ones_col = jnp.where(lax.broadcasted_iota(jnp.int32, (tk, LANES), 1) == 0, 1.0, 0.0).astype(BF16)

    def step(masked):
        for h in range(DA_HEADS):
            sl = slice(h * LANES, (h + 1) * LANES)
            s = _dot_nt(qs_scr[h], k_ref[0, :, sl])
            if masked:
                r = lax.broadcasted_iota(jnp.int32, s.shape, 0)
                c = lax.broadcasted_iota(jnp.int32, s.shape, 1)
                q_pos = qi * tq + jnp.where(r >= tq, r - tq, r)
                s = jnp.where(((ki * tk + c) >> 6) <= (q_pos >> 6), s, -1e30)
            m_old = m_scr[h]
            m_new = jnp.maximum(m_old, jnp.broadcast_to(jnp.max(s, axis=-1, keepdims=True), m_old.shape))
            alpha = jnp.exp2(m_old - m_new)
            pr = jnp.exp2(s - jnp.concatenate([m_new] * (tk // LANES), axis=1)).astype(BF16)
            v_aug = jnp.concatenate([v_ref[0, :, sl], ones_col], axis=1)
            acc_scr[h] = jnp.concatenate([alpha, alpha], axis=1) * acc_scr[h] + _dot(pr, v_aug)
            m_scr[h] = m_new

    @pl.when((flags & 1) == 0)
    def _():
        step(False)

    @pl.when((flags & 1) == 1)
    def _():
        step(True)

    @pl.when((flags & 2) != 0)
    def _():
        lam = _lambda(lq1, lk1, lq2, lk2, lam_init)
        for h in range(DA_HEADS):
            acc = acc_scr[h]
            o = _diff_finish(acc[:, 0:LANES], acc[:, LANES:LANES + 1], tq, lam, sg_ref[...], lam_init)
            o_ref[0, :, h * LANES:(h + 1) * LANES] = o.astype(BF16)


def _attn_prompt(q, k, v, lq1, lk1, lq2, lk2, sg, lam_init):
    bsz, l, _ = q.shape
    tq = _row_tile(l, ATTN_TQ)
    tk = _row_tile(l, ATTN_TK)
    pairs = []
    for a in range(l // tq):
        last = ((a + 1) * tq - 1) // tk
        for b in range(last + 1):
            crosses = (b + 1) * tk > a * tq + CHUNK
            pairs.append((a, b, int(crosses) + 2 * int(b == last)))
    tabs = [jnp.asarray([pr[i] for pr in pairs], jnp.int32) for i in range(3)]
    small = lambda a: pl.BlockSpec(a.shape, lambda bi, p, qt, kt, fl: (0,) * a.ndim)
    grid_spec = pltpu.PrefetchScalarGridSpec(
        num_scalar_prefetch=3,
        grid=(bsz, len(pairs)),
        in_specs=[pl.BlockSpec((1, tq, DA_QK), lambda bi, p, qt, kt, fl: (bi, qt[p], 0)),
                  pl.BlockSpec((1, tk, DA_QK), lambda bi, p, qt, kt, fl: (bi, kt[p], 0)),
                  pl.BlockSpec((1, tk, DA_QK), lambda bi, p, qt, kt, fl: (bi, kt[p], 0)),
                  small(lq1), small(lk1), small(lq2), small(lk2), small(sg)],
        out_specs=pl.BlockSpec((1, tq, DA_QK), lambda bi, p, qt, kt, fl: (bi, qt[p], 0)),
        scratch_shapes=[pltpu.VMEM((DA_HEADS, 2 * tq, LANES), BF16), pltpu.VMEM((DA_HEADS, 2 * tq, LANES), F32),
                        pltpu.VMEM((DA_HEADS, 2 * tq, 2 * LANES), F32)],
    )
    return pl.pallas_call(
        functools.partial(_attn_prompt_kernel, tq=tq, tk=tk, lam_init=lam_init),
        grid_spec=grid_spec,
        out_shape=jax.ShapeDtypeStruct((bsz, l, DA_QK), BF16),
        compiler_params=_cparams(("parallel", "arbitrary")),
        name="attn_prompt",
    )(*tabs, q, k, v, lq1, lk1, lq2, lk2, sg)


def _attn_sample_kernel(q_ref, kc_ref, vc_ref, kn_ref, vn_ref, lq1, lk1, lq2, lk2, sg_ref, o_ref,
                        *, tq, past_len, lam_init):
    qs = _split_q(q_ref[0])
    s_c = _dot_nt(qs, kc_ref[0].astype(BF16))
    s_n = _dot_nt(qs, kn_ref[0])
    r = lax.broadcasted_iota(jnp.int32, s_n.shape, 0)
    c = lax.broadcasted_iota(jnp.int32, s_n.shape, 1)
    qrow = jnp.where(r >= tq, r - tq, r)
    s_n = jnp.where(((past_len + c) >> 6) <= ((past_len + qrow) >> 6), s_n, -1e30)
    m = jnp.maximum(jnp.max(s_c, axis=-1, keepdims=True), jnp.max(s_n, axis=-1, keepdims=True))
    p_c = jnp.exp2(s_c - m)
    p_n = jnp.exp2(s_n - m)
    l = jnp.sum(p_c, axis=-1, keepdims=True) + jnp.sum(p_n, axis=-1, keepdims=True)
    acc = _dot(p_c.astype(BF16), vc_ref[0].astype(BF16)) + _dot(p_n.astype(BF16), vn_ref[0])
    lam = _lambda(lq1, lk1, lq2, lk2, lam_init)
    o_ref[0] = _diff_finish(acc, l, tq, lam, sg_ref[...], lam_init).astype(BF16)


def _attn_sample(q, k, v, k_past, v_past, lq1, lk1, lq2, lk2, sg, lam_init):
    bsz, l, _ = q.shape
    past_len = k_past.shape[1]
    small = lambda a: pl.BlockSpec(a.shape, lambda bi, h: (0,) * a.ndim)
    new = pl.BlockSpec((1, l, LANES), lambda bi, h: (bi, 0, h))
    old = pl.BlockSpec((1, past_len, LANES), lambda bi, h: (bi, 0, h))
    return pl.pallas_call(
        functools.partial(_attn_sample_kernel, tq=l, past_len=past_len, lam_init=lam_init),
        grid=(bsz, DA_HEADS),
        in_specs=[new, old, old, new, new, small(lq1), small(lk1), small(lq2), small(lk2), small(sg)],
        out_specs=new,
        out_shape=jax.ShapeDtypeStruct((bsz, l, DA_QK), BF16),
        compiler_params=_cparams(("parallel", "parallel")),
        name="attn_sample",
    )(q, k_past, v_past, k, v, lq1, lk1, lq2, lk2, sg)


def _mem_kv_kernel(m_ref, wk_ref, wv_ref, k32_ref, v32_ref, k16_ref, v16_ref):
    m = m_ref[...].astype(BF16)
    k = _dot(m, wk_ref[0])
    v = _dot(m, wv_ref[0])
    k32_ref[0] = k
    v32_ref[0] = v
    k16_ref[0] = k.astype(BF16)
    v16_ref[0] = v.astype(BF16)


def _mem_kv(mem, wk, wv):
    n = mem.shape[0]
    depth = wk.shape[0]
    tm = _row_tile(n)
    wspec = pl.BlockSpec((1, D_MODEL, D_MODEL), lambda l, i: (l, 0, 0))
    ospec = pl.BlockSpec((1, tm, D_MODEL), lambda l, i: (l, i, 0))
    return pl.pallas_call(
        _mem_kv_kernel,
        grid=(depth, n // tm),
        in_specs=[pl.BlockSpec((tm, D_MODEL), lambda l, i: (i, 0)), wspec, wspec],
        out_specs=[ospec] * 4,
        out_shape=[jax.ShapeDtypeStruct((depth, n, D_MODEL), F32)] * 2
        + [jax.ShapeDtypeStruct((depth, n, D_MODEL), BF16)] * 2,
        compiler_params=_cparams(("parallel", "parallel")),
        name="mem_kv",
    )(mem, wk, wv)


def _cross_kernel(x_ref, a_ref, b_ref, wm_ref, g_ref, wq_ref, wo_ref, mk_ref, mv_ref, o_ref):
    x = x_ref[0] + _dot(a_ref[0], wm_ref[0:512, :]) + _dot(b_ref[0], wm_ref[512:1024, :])
    h = _rms(x, g_ref[...]).astype(BF16)
    q = _dot(h, wq_ref[...]).astype(BF16)
    outs = []
    for hd in range(X_HEADS):
        sl = slice(hd * X_HEAD_DIM, (hd + 1) * X_HEAD_DIM)
        s = _dot_nt(q[:, sl], mk_ref[0, 0, :, sl].astype(BF16)) * (X_HEAD_DIM ** -0.5)
        s = s - jnp.max(s, axis=-1, keepdims=True)
        e = jnp.exp(s)
        p = (e / jnp.sum(e, axis=-1, keepdims=True)).astype(BF16)
        outs.append(_dot(p, mv_ref[0, 0, :, sl].astype(BF16)).astype(BF16))
    o = jnp.concatenate(outs, axis=-1)
    o_ref[0] = x + _dot(o, wo_ref[...])


def _cross(x, a, b, wm, g, wq, wo, mk, mv, layer):
    bsz, l, _ = x.shape
    tm = _row_tile(l)
    full = lambda arr: pl.BlockSpec(arr.shape, lambda bi, i: (0,) * arr.ndim)
    xs = pl.BlockSpec((1, tm, D_MODEL), lambda bi, i: (bi, i, 0))
    hs = pl.BlockSpec((1, tm, 512), lambda bi, i: (bi, i, 0))
    ms = pl.BlockSpec((1, 1, N_MEM, D_MODEL), lambda bi, i: (layer, bi, 0, 0))
    return pl.pallas_call(
        _cross_kernel,
        grid=(bsz, l // tm),
        in_specs=[xs, hs, hs, full(wm), full(g), full(wq), full(wo), ms, ms],
        out_specs=xs,
        out_shape=jax.ShapeDtypeStruct((bsz, l, D_MODEL), F32),
        compiler_params=_cparams(("parallel", "parallel")),
        name="cross_attn",
    )(x, a, b, wm, g, wq, wo, mk, mv)


PEER_T = 512
PEER_EC = 2048
PEER_SUB = 512
N_CAND = 56


def _top_values(s, k):
    out = []
    work = s
    for _ in range(k):
        m = jnp.max(work, axis=0, keepdims=True)
        out.append(m)
        work = jnp.where(work >= m, NEG_INF, work)
    return out


def _bf16_value(x):
    return x.astype(BF16).astype(F32)


def _bf16_below(x):
    return lax.bitcast_convert_type(lax.bitcast_convert_type(x, jnp.int32) - 0x10000, F32)


def _bf16_pair_word(x):
    bits = lax.bitcast_convert_type(x, jnp.uint32)
    return lax.bitcast_convert_type((bits & jnp.uint32(0xFFFF0000)) | (bits >> 16), jnp.int32)


EXP_FLOOR = -80.0


def _sorting_network(n):
    def merge(lo, hi, r):
        step = r * 2
        if step < hi - lo:
            yield from merge(lo, hi, step)
            yield from merge(lo + r, hi, step)
            yield from [(i, i + r) for i in range(lo + r, hi - r, step)]
        else:
            yield (lo, lo + r)

    def sort(lo, hi):
        if hi - lo >= 1:
            mid = lo + (hi - lo) // 2
            yield from sort(lo, mid)
            yield from sort(mid + 1, hi)
            yield from merge(lo, hi, 1)

    return list(sort(0, n - 1))


KEY_VREGS = PEER_KEYS // SUBLANES
KEY_SORT = _sorting_network(KEY_VREGS)


def _top_values_keys(s, k):
    rows = [s[SUBLANES * r:SUBLANES * (r + 1)] for r in range(KEY_VREGS)]
    for i, j in KEY_SORT:
        rows[i], rows[j] = jnp.maximum(rows[i], rows[j]), jnp.minimum(rows[i], rows[j])
    vals = []
    for r in range(k):
        m = jnp.max(rows[0], axis=0, keepdims=True)
        vals.append(m)
        still_needed = k - 1 - r
        if still_needed == 0:
            break
        hit = rows[0] >= m
        for d in range(min(still_needed, KEY_VREGS)):
            below = rows[d + 1] if d + 1 < KEY_VREGS else NEG_INF
            rows[d] = jnp.where(hit, below, rows[d])
    return vals


def _ranked_weights(s, k):
    vals = _top_values_keys(s, k)
    nums = [jnp.ones_like(vals[0])]
    for r in range(1, k):
        v = _bf16_value(jnp.exp(jnp.maximum(vals[r] - vals[0], EXP_FLOOR)))
        nums.append(jnp.minimum(v, _bf16_below(nums[-1])))
    placed = jnp.zeros(s.shape, F32)
    for r in range(k):
        placed = jnp.where(s == vals[r], nums[r], placed)
    return vals, nums, placed


def _peer_route_kernel(x_ref, g_ref, wq_ref, sk_ref, ht_ref, ecut_ref, e0_ref, e1_ref, e1_scr):
    h = _rms(x_ref[...], g_ref[...])
    ht_ref[...] = h.T.astype(BF16)
    q = _dot(h.astype(BF16), wq_ref[...]).astype(BF16)
    t = h.shape[0]
    half = PEER_KEYS // 2
    for hd in range(PEER_HEADS):
        base = hd * 2 * PEER_KEYS
        s0_all = _dot_nt(sk_ref[0], q[:, base:base + PEER_KEYS])
        s1_all = _dot_nt(sk_ref[1], q[:, base + PEER_KEYS:base + 2 * PEER_KEYS])
        for tc in range(t // LANES):
            cols = slice(tc * LANES, (tc + 1) * LANES)
            s0 = s0_all[:, cols]
            u0 = _top_values_keys(s0, PEER_TOPK + 1)
            u1, n1, e1 = _ranked_weights(s1_all[:, cols], PEER_TOPK + 1)
            cands = [u0[a] + u1[b] for a in range(PEER_TOPK + 1) for b in range(PEER_TOPK + 1)
                     if (a + 1) * (b + 1) <= PEER_TOPK + 1]
            cands += [jnp.full((1, LANES), NEG_INF, F32)] * (N_CAND - len(cands))
            best = _top_values(jnp.concatenate(cands, axis=0), PEER_TOPK + 1)
            thr = 0.5 * (best[PEER_TOPK - 1] + best[PEER_TOPK])
            z = jnp.ones_like(best[0])
            for b in best[1:PEER_TOPK]:
                z = z + jnp.exp(b - best[0])
            cut = thr - s0
            ecut = jnp.full(cut.shape, 2.0, F32)
            for r in range(PEER_TOPK + 1):
                ecut = jnp.where(u1[r] > cut, n1[r], ecut)
            ecut_ref[hd, :, cols] = _bf16_pair_word(ecut)
            e0_ref[hd, :, cols] = _bf16_pair_word(
                _bf16_value(jnp.exp(jnp.maximum(s0 - u0[0], EXP_FLOOR)) / z))
            e1_scr[hd, tc] = e1
            lo = lax.bitcast_convert_type(e1_scr[hd, tc, pl.ds(0, half, stride=2), :], jnp.uint32) >> 16
            hi = (lax.bitcast_convert_type(e1_scr[hd, tc, pl.ds(1, half, stride=2), :], jnp.uint32)
                  & jnp.uint32(0xFFFF0000))
            e1_ref[hd, :, cols] = lax.bitcast_convert_type(hi | lo, jnp.int32)


def _peer_route(x, g, wq, sk):
    n = x.shape[0]
    t = _row_tile(n, PEER_T)
    full = lambda a: pl.BlockSpec(a.shape, lambda i: (0,) * a.ndim)
    tiles = n // t
    tile_spec = lambda *dims: pl.BlockSpec((None,) + dims, lambda i: (i,) + (0,) * len(dims))
    tile_shape = lambda dtype, *dims: jax.ShapeDtypeStruct((tiles,) + dims, dtype)
    return pl.pallas_call(
        _peer_route_kernel,
        grid=(tiles,),
        in_specs=[pl.BlockSpec((t, D_MODEL), lambda i: (i, 0)), full(g), full(wq), full(sk)],
        out_specs=[tile_spec(D_MODEL, t), tile_spec(PEER_HEADS, PEER_KEYS, t),
                   tile_spec(PEER_HEADS, PEER_KEYS, t), tile_spec(PEER_HEADS, PEER_KEYS // 2, t)],
        out_shape=[tile_shape(BF16, D_MODEL, t), tile_shape(jnp.int32, PEER_HEADS, PEER_KEYS, t),
                   tile_shape(jnp.int32, PEER_HEADS, PEER_KEYS, t),
                   tile_shape(jnp.int32, PEER_HEADS, PEER_KEYS // 2, t)],
        scratch_shapes=[pltpu.VMEM((PEER_HEADS, t // LANES, PEER_KEYS, LANES), F32)],
        compiler_params=_cparams(("parallel",)),
        name="peer_route",
    )(x, g, wq, sk)


def _peer_dense_kernel(ht_ref, ecut_ref, e0_ref, e1_ref, eu_ref, evt_ref, x_ref, *rest, final):
    if final:
        fg_ref, o_ref, acc_scr, at_scr, w_scr = rest
    else:
        o_ref, acc_scr, at_scr, w_scr = rest
    j = pl.program_id(1)
    rows_step = PEER_EC // PEER_KEYS
    rows_sub = PEER_SUB // PEER_KEYS

    @pl.when(j == 0)
    def _():
        acc_scr[...] = jnp.zeros(acc_scr.shape, F32)

    t = ht_ref.shape[1]

    def packed_row(row, cols):
        return pltpu.bitcast(jnp.broadcast_to(row[:, cols], (PEER_KEYS // 2, LANES)), BF16)

    def hidden_and_weights(sc):
        rows = slice(sc * PEER_SUB, (sc + 1) * PEER_SUB)
        for ii in range(rows_sub):
            i0 = j * rows_step + sc * rows_sub + ii
            r0 = sc * PEER_SUB + ii * PEER_KEYS
            ecut_rows = [ecut_ref[hd, pl.ds(i0, 1), :] for hd in range(PEER_HEADS)]
            e0_rows = [e0_ref[hd, pl.ds(i0, 1), :] for hd in range(PEER_HEADS)]
            for tc in range(t // LANES):
                cols = slice(tc * LANES, (tc + 1) * LANES)
                w = None
                for hd in range(PEER_HEADS):
                    e1 = pltpu.bitcast(e1_ref[hd, :, cols], BF16)
                    term = jnp.where(e1 >= packed_row(ecut_rows[hd], cols), e1, jnp.zeros_like(e1))
                    term = term * packed_row(e0_rows[hd], cols)
                    w = term if w is None else w + term
                w_scr[r0:r0 + PEER_KEYS, cols] = w
        at_scr[rows, :] = _dot(eu_ref[rows, :], ht_ref[...]).astype(BF16)

    def expert_output(sc):
        rows = slice(sc * PEER_SUB, (sc + 1) * PEER_SUB)
        return _dot(evt_ref[0, :, rows], w_scr[rows, :] * jax.nn.gelu(at_scr[rows, :]))

    n_sub = PEER_EC // PEER_SUB
    hidden_and_weights(0)
    total = None
    for sc in range(n_sub):
        if sc + 1 < n_sub:
            hidden_and_weights(sc + 1)
        part = expert_output(sc)
        total = part if total is None else total + part
    acc_scr[...] += total

    @pl.when(j == pl.num_programs(1) - 1)
    def _():
        y = x_ref[...] + acc_scr[...].T
        o_ref[...] = _rms(y, fg_ref[...]) if final else y


def _peer_dense(x, ht, ecut, e0, e1, eu, evt, final_g=None):
    n = x.shape[0]
    t = _row_tile(n, PEER_T)
    tile_spec = lambda *dims: pl.BlockSpec((None,) + dims, lambda i, j: (i,) + (0,) * len(dims))
    rspec = tile_spec(PEER_HEADS, PEER_KEYS, t)
    xspec = pl.BlockSpec((t, D_MODEL), lambda i, j: (i, 0))
    final = final_g is not None
    extra = [final_g] if final else []
    return pl.pallas_call(
        functools.partial(_peer_dense_kernel, final=final),
        grid=(n // t, PEER_EXPERTS // PEER_EC),
        in_specs=[tile_spec(D_MODEL, t), rspec, rspec, tile_spec(PEER_HEADS, PEER_KEYS // 2, t),
                  pl.BlockSpec((PEER_EC, D_MODEL), lambda i, j: (j, 0)),
                  pl.BlockSpec((1, D_MODEL, PEER_EC), lambda i, j: (j, 0, 0)), xspec]
        + [pl.BlockSpec(g.shape, lambda i, j: (0, 0)) for g in extra],
        out_specs=xspec,
        out_shape=jax.ShapeDtypeStruct((n, D_MODEL), F32),
        scratch_shapes=[pltpu.VMEM((D_MODEL, t), F32), pltpu.VMEM((PEER_EC, t), BF16),
                        pltpu.VMEM((PEER_EC, t), BF16)],
        compiler_params=_cparams(("parallel", "arbitrary")),
        name="peer_dense",
    )(ht, ecut, e0, e1, eu, evt, x, *extra)


def _cd_in_kernel(x_ref, g_ref, wc_ref, wg_ref, wx_ref, wd_ref, lg_ref, lb_ref,
                  u_ref, v_ref, gate_ref, xbc_ref, dt_ref):
    h = _rms(x_ref[...], g_ref[...]).astype(BF16)
    u_ref[...] = jax.nn.gelu(_dot(h, wc_ref[:, 0:512]))
    v_ref[...] = _layernorm(jax.nn.gelu(_dot(h, wc_ref[:, 512:1024])), lg_ref[...], lb_ref[...])
    gate_ref[...] = _dot(h, wg_ref[...])
    xbc_ref[...] = _dot(h, wx_ref[...])
    dt_ref[...] = _dot(h, wd_ref[...])


def _cd_in(x, g, wc, wg, wx, wd, lg, lb):
    n = x.shape[0]
    tm = _row_tile(n)
    row = lambda c: pl.BlockSpec((tm, c), lambda i: (i, 0))
    full = lambda a: pl.BlockSpec(a.shape, lambda i: (0,) * a.ndim)
    widths = (512, 512, 512, SSM_CONV_CH, LANES)
    return pl.pallas_call(
        _cd_in_kernel,
        grid=(n // tm,),
        in_specs=[row(D_MODEL), full(g), full(wc), full(wg), full(wx), full(wd), full(lg), full(lb)],
        out_specs=[row(c) for c in widths],
        out_shape=[jax.ShapeDtypeStruct((n, c), F32) for c in widths],
        compiler_params=_cparams(("parallel",)),
        name="cd_in",
    )(x, g, wc, wg, wx, wd, lg, lb)


def _gmlp_kernel(u_ref, v_ref, ws_ref, bs_ref, o_ref, *, gm_len, n_chunks):
    r = lax.broadcasted_iota(jnp.int32, (gm_len, gm_len), 0)
    c = lax.broadcasted_iota(jnp.int32, (gm_len, gm_len), 1)
    for g in range(GM_GROUPS):
        w = jnp.where(r >= c, ws_ref[g], 0.0).astype(BF16)
        ch = slice(g * LANES, (g + 1) * LANES)
        for ci in range(n_chunks):
            rows = slice(ci * gm_len, (ci + 1) * gm_len)
            mixed = _dot(w, v_ref[0, rows, ch].astype(BF16)) + bs_ref[:, g:g + 1]
            o_ref[0, rows, ch] = (u_ref[0, rows, ch] * mixed).astype(BF16)


def _gmlp(u, v, ws, bs_t, gm_len):
    bsz, l, _ = u.shape
    tl = _row_tile(l)
    full = lambda a: pl.BlockSpec(a.shape, lambda bi, i: (0,) * a.ndim)
    spec = pl.BlockSpec((1, tl, GM_WIDTH), lambda bi, i: (bi, i, 0))
    return pl.pallas_call(
        functools.partial(_gmlp_kernel, gm_len=gm_len, n_chunks=tl // gm_len),
        grid=(bsz, l // tl),
        in_specs=[spec, spec, full(ws), full(bs_t)],
        out_specs=spec,
        out_shape=jax.ShapeDtypeStruct((bsz, l, GM_WIDTH), BF16),
        compiler_params=_cparams(("parallel", "parallel")),
        name="gmlp",
    )(u, v, ws, bs_t)


SSD_HALO = 8
SSD_PAIRS = SSM_HEADS // 2
SSD_GROUP_W = SSM_INNER // 2


def _ssd_kernel(xbc_ref, halo_ref, ctx_ref, gate_ref, dt_ref, h0_ref, cw_ref, cb_ref, dtb_ref, alog_ref,
                dsk_ref, ng_ref, y_ref, fin_ref, xp_scr, st_scr, *, tl):
    i = pl.program_id(1)
    pad = SSD_HALO - (SSM_CONV_K - 1)
    q = CHUNK

    @pl.when(i == 0)
    def _():
        xp_scr[pad:SSD_HALO, :] = ctx_ref[0]
        for k in range(SSD_PAIRS):
            st_scr[k] = jnp.concatenate([h0_ref[0, 2 * k], h0_ref[0, 2 * k + 1]], axis=0).T

    @pl.when(i > 0)
    def _():
        xp_scr[0:SSD_HALO, :] = halo_ref[0]

    xp_scr[SSD_HALO:SSD_HALO + tl, :] = xbc_ref[0]

    lane128 = lax.broadcasted_iota(jnp.int32, (q, LANES), 1)
    row128 = lax.broadcasted_iota(jnp.int32, (q, LANES), 0)
    er = lax.broadcasted_iota(jnp.int32, (LANES, SSM_INNER), 0)
    ec = lax.broadcasted_iota(jnp.int32, (LANES, SSM_INNER), 1)
    expand = jnp.where((ec >> 6) == er, 1.0, 0.0).astype(F32)
    tr = lax.broadcasted_iota(jnp.int32, (q, q), 0)
    tc = lax.broadcasted_iota(jnp.int32, (q, q), 1)
    ltri = jnp.where(tr >= tc, 1.0, 0.0).astype(F32)
    a_neg = -jnp.exp(alog_ref[...])
    zeros_q = jnp.zeros((q, LANES), F32)

    for ci in range(tl // q):
        r0 = ci * q
        acc = jnp.broadcast_to(cb_ref[...], (q, SSM_CONV_CH))
        for k in range(SSM_CONV_K):
            acc = acc + cw_ref[k:k + 1, :] * xp_scr[r0 + pad + k:r0 + pad + k + q, :]
        xc = _silu(acc)
        xs = xc[:, 0:SSM_INNER]
        dt = jnp.where(lane128 < SSM_HEADS, _softplus(dt_ref[0, r0:r0 + q, :] + dtb_ref[...]), 0.0)
        acs = _dot_f32(ltri, dt * a_neg)
        acs_e = _dot_f32(acs, expand)
        dt_e = _dot_f32(dt, expand)
        tot_e = acs_e[q - 1:q, :]
        xdt = xs * dt_e
        xd = (xdt * jnp.exp(tot_e - acs_e)).astype(BF16)
        eacs = jnp.exp(acs_e)
        cdec = jnp.exp(tot_e)
        ys = []
        for k in range(SSD_PAIRS):
            g = k // 2
            blk = slice(k * LANES, (k + 1) * LANES)
            bm = xc[:, SSM_INNER + g * SSM_STATE:SSM_INNER + (g + 1) * SSM_STATE]
            cm = xc[:, SSM_INNER + 2 * SSM_STATE + g * SSM_STATE:
                    SSM_INNER + 2 * SSM_STATE + (g + 1) * SSM_STATE].astype(BF16)
            cb2 = _dot_nt(cm, jnp.concatenate([bm, bm], axis=0).astype(BF16))
            a_blk = acs_e[:, blk]
            a_row = jnp.sum(jnp.where(row128 == (lane128 & (q - 1)), a_blk, 0.0), axis=0, keepdims=True)
            lmat = jnp.exp(jnp.where((lane128 & (q - 1)) <= row128, a_blk - a_row, NEG_INF))
            sc = (cb2 * lmat).astype(BF16)
            x_blk = xdt[:, blk]
            rhs = jnp.concatenate([jnp.where(lane128 < q, x_blk, 0.0), jnp.where(lane128 >= q, x_blk, 0.0)],
                                  axis=0).astype(BF16)
            prev = st_scr[k]
            y_pair = _dot(sc, rhs) + _dot(cm, prev.astype(BF16)) * eacs[:, blk]
            ys.append(y_pair)
            bt = jnp.concatenate([bm, zeros_q], axis=0).T.astype(BF16)
            xd_pad = jnp.concatenate([xd[:, blk], zeros_q.astype(BF16)], axis=0)
            st_scr[k] = prev * cdec[:, blk] + _dot(bt, xd_pad)
        y = jnp.concatenate(ys, axis=-1) + dsk_ref[...] * xs
        y = y * _silu(gate_ref[0, r0:r0 + q, :])
        outs = []
        for g in range(2):
            seg = y[:, g * SSD_GROUP_W:(g + 1) * SSD_GROUP_W]
            outs.append(seg * lax.rsqrt(jnp.mean(seg * seg, axis=-1, keepdims=True) + 1e-6))
        y_ref[0, r0:r0 + q, :] = (jnp.concatenate(outs, axis=-1) * ng_ref[...]).astype(BF16)

    @pl.when(i == pl.num_programs(1) - 1)
    def _():
        for k in range(SSD_PAIRS):
            st = st_scr[k].T
            fin_ref[0, 2 * k] = st[0:SSM_HEAD_DIM]
            fin_ref[0, 2 * k + 1] = st[SSM_HEAD_DIM:2 * SSM_HEAD_DIM]


def _ssd(xbc, ctx, gate, dt, h0, cw, cb, dtb, alog, dsk, ng):
    bsz, l, _ = xbc.shape
    tl = 256 if l % 256 == 0 else l
    hb = tl // SSD_HALO
    full = lambda a: pl.BlockSpec(a.shape, lambda bi, i: (0,) * a.ndim)
    tile = lambda c: pl.BlockSpec((1, tl, c), lambda bi, i: (bi, i, 0))
    stspec = pl.BlockSpec((1, SSM_HEADS, SSM_HEAD_DIM, SSM_STATE), lambda bi, i: (bi, 0, 0, 0))
    return pl.pallas_call(
        functools.partial(_ssd_kernel, tl=tl),
        grid=(bsz, l // tl),
        in_specs=[tile(SSM_CONV_CH),
                  pl.BlockSpec((1, SSD_HALO, SSM_CONV_CH), lambda bi, i: (bi, jnp.maximum(i * hb - 1, 0), 0)),
                  pl.BlockSpec((1, SSM_CONV_K - 1, SSM_CONV_CH), lambda bi, i: (bi, 0, 0)),
                  tile(SSM_INNER), tile(LANES), stspec,
                  full(cw), full(cb), full(dtb), full(alog), full(dsk), full(ng)],
        out_specs=[tile(SSM_INNER), stspec],
        out_shape=[jax.ShapeDtypeStruct((bsz, l, SSM_INNER), BF16),
                   jax.ShapeDtypeStruct((bsz, SSM_HEADS, SSM_HEAD_DIM, SSM_STATE), F32)],
        scratch_shapes=[pltpu.VMEM((SSD_HALO + tl, SSM_CONV_CH), F32),
                        pltpu.VMEM((SSD_PAIRS, SSM_STATE, LANES), F32)],
        compiler_params=_cparams(("parallel", "arbitrary")),
        name="ssd",
    )(xbc, xbc, ctx, gate, dt, h0, cw, cb, dtb, alog, dsk, ng)


def _blocked_transpose(table):
    depth, _, d = table.shape
    return table.reshape(depth, PEER_EXPERTS // PEER_EC, PEER_EC, d).swapaxes(2, 3)


def _pad_lanes(a, width=LANES):
    return jnp.pad(a, ((0, 0), (0, width - a.shape[-1])))


def _trunk(x, mem_k, mem_v, attn_k, attn_v, conv_a, ssd_st, conv_ssm, gm_len, p):
    bsz, l, _ = x.shape
    n = bsz * l
    row = lambda a: a.reshape(1, -1)
    x2 = x.reshape(n, D_MODEL)

    lam_init = 0.8 - 0.6 * math.exp(-0.3 * 0)
    glu, q16, k32, v32, k16, v16 = _ab_in(x2, row(p["norm_mix_g"][0]), p["w_in_ab"])
    glu3 = glu.reshape(bsz, l, CONV_CH)
    ctx_a = jnp.zeros((bsz, CONV_K - 1, CONV_CH), F32) if conv_a is None else conv_a[0]
    ca = _conv_a(glu3, ctx_a, p["conv_a_w"], row(p["conv_a_b"]), row(p["ln_a_g"]), row(p["ln_a_b"]))
    lam_args = (row(p["lam_q1"]), row(p["lam_k1"]), row(p["lam_q2"]), row(p["lam_k2"]), row(p["subln_g"]))
    shp3 = lambda a: a.reshape(bsz, l, DA_QK)
    if attn_k is None:
        o = _attn_prompt(shp3(q16), shp3(k16), shp3(v16), *lam_args, lam_init)
    else:
        past = attn_k.shape[2]
        o = _attn_sample(shp3(q16), shp3(k16), shp3(v16), attn_k[0].reshape(bsz, past, DA_QK),
                         attn_v[0].reshape(bsz, past, DA_QK), *lam_args, lam_init)
    new_k = k32.reshape(1, bsz, l, DA_HEADS, 2 * DA_HEAD_DIM)
    new_v = v32.reshape(1, bsz, l, DA_HEADS, 2 * DA_HEAD_DIM)
    new_ca = glu3[:, l - (CONV_K - 1):][None]

    def tail(x2, a, b, w_mix, layer, final_g=None):
        x3 = _cross(x2.reshape(bsz, l, D_MODEL), a, b, w_mix, row(p["norm_cross_g"][layer]), p["w_xq"][layer],
                    p["w_xo"][layer], mem_k, mem_v, layer).reshape(n, D_MODEL)
        routed = _peer_route(x3, row(p["norm_ffn_g"][layer]), p["w_pq"][layer], p["sub_keys"][layer])
        return _peer_dense(x3, *routed, p["expert_u"][layer], p["expert_vt"][layer], final_g)

    x2 = tail(x2, ca, o, p["w_out_ab"], 0)

    u, vln, gate, xbc, dt = _cd_in(x2, row(p["norm_mix_g"][1]), p["w_cd_c"], p["w_cd_gate"], p["w_cd_xbc"],
                                   p["w_cd_dt"], row(p["ln_c_g"]), row(p["ln_c_b"]))
    shp = lambda a: a.reshape(bsz, l, a.shape[-1])
    c_out = _gmlp(shp(u), shp(vln), p["gm_w_s"][:, :gm_len, :gm_len], p["gm_b_s"][:, :gm_len].T, gm_len)
    xbc3 = shp(xbc)
    ctx_d = jnp.zeros((bsz, SSM_CONV_K - 1, SSM_CONV_CH), F32) if conv_ssm is None else conv_ssm[0]
    h0 = jnp.zeros((bsz, SSM_HEADS, SSM_HEAD_DIM, SSM_STATE), F32) if ssd_st is None else ssd_st[0]
    y, fin = _ssd(xbc3, ctx_d, shp(gate), shp(dt), h0, p["conv_d_w"], row(p["conv_d_b"]),
                  _pad_lanes(row(p["dt_bias"])), _pad_lanes(row(p["a_log"])),
                  row(jnp.repeat(p["d_skip"], SSM_HEAD_DIM)), row(p["norm_d_g"]))
    y_out = tail(x2, c_out, y, p["w_out_cd"], 1, row(p["norm_final_g"])).reshape(bsz, l, D_MODEL)
    new_gv = vln.reshape(1, bsz, l, GM_GROUPS, GM_WIDTH // GM_GROUPS)
    new_cs = xbc3[:, l - (SSM_CONV_K - 1):][None]
    return y_out, new_k, new_v, new_ca, new_gv, fin[None], new_cs


def kernel(x_prompt, x_sample, cache_attn_k, cache_attn_v, state_conv_a, state_ssd, state_conv_ssm, cache_mem_k, cache_mem_v, mem_prompt, norm_mix_g, norm_cross_g, norm_ffn_g, norm_final_g, w_in_ab, conv_a_w, conv_a_b, ln_a_g, ln_a_b, lam_q1, lam_k1, lam_q2, lam_k2, subln_g, w_out_ab, w_in_cd, ln_c_g, ln_c_b, gm_w_s, gm_b_s, conv_d_w, conv_d_b, dt_bias, a_log, d_skip, norm_d_g, w_out_cd, w_xq, w_xk, w_xv, w_xo, w_pq, sub_keys, expert_u, expert_v):
    bf = lambda a: a.astype(BF16)
    w_cd = w_in_cd[0]
    p = {
        "norm_mix_g": norm_mix_g, "norm_cross_g": norm_cross_g, "norm_ffn_g": norm_ffn_g,
        "norm_final_g": norm_final_g,
        "w_in_ab": bf(w_in_ab[0]), "conv_a_w": conv_a_w[0], "conv_a_b": conv_a_b[0],
        "ln_a_g": ln_a_g[0], "ln_a_b": ln_a_b[0],
        "lam_q1": lam_q1[0], "lam_k1": lam_k1[0], "lam_q2": lam_q2[0], "lam_k2": lam_k2[0],
        "subln_g": subln_g[0], "w_out_ab": bf(w_out_ab[0]),
        "w_cd_c": bf(w_cd[:, 0:1024]), "w_cd_gate": bf(w_cd[:, 1024:1536]), "w_cd_xbc": bf(w_cd[:, 1536:2560]),
        "w_cd_dt": bf(_pad_lanes(w_cd[:, 2560:2568])),
        "ln_c_g": ln_c_g[0], "ln_c_b": ln_c_b[0], "gm_w_s": gm_w_s[0], "gm_b_s": gm_b_s[0],
        "conv_d_w": conv_d_w[0], "conv_d_b": conv_d_b[0], "dt_bias": dt_bias[0], "a_log": a_log[0],
        "d_skip": d_skip[0], "norm_d_g": norm_d_g[0], "w_out_cd": bf(w_out_cd[0]),
        "w_xq": bf(w_xq), "w_xo": bf(w_xo), "w_pq": bf(w_pq), "sub_keys": bf(sub_keys),
        "expert_u": bf(expert_u), "expert_vt": _blocked_transpose(bf(expert_v)),
    }
    bsz, seq, _ = x_prompt.shape
    dec_b, dec_l, _ = x_sample.shape
    depth = w_xk.shape[0]

    mk32, mv32, mk16, mv16 = _mem_kv(mem_prompt.reshape(bsz * N_MEM, D_MODEL), bf(w_xk), bf(w_xv))
    mem_k_p = mk32.reshape(depth, bsz, N_MEM, X_HEADS, X_HEAD_DIM)
    mem_v_p = mv32.reshape(depth, bsz, N_MEM, X_HEADS, X_HEAD_DIM)
    y_prompt, kp, vp, cap, _, ssdp, csp = _trunk(
        x_prompt, mk16.reshape(depth, bsz, N_MEM, D_MODEL), mv16.reshape(depth, bsz, N_MEM, D_MODEL),
        None, None, None, None, None, 2 * CHUNK, p)

    y_sample, ks, vs, cas, gvs, ssds, css = _trunk(
        x_sample, cache_mem_k.reshape(depth, dec_b, N_MEM, D_MODEL), cache_mem_v.reshape(depth, dec_b, N_MEM, D_MODEL),
        cache_attn_k, cache_attn_v, state_conv_a, state_ssd, state_conv_ssm, dec_l, p)
    return (y_prompt, y_sample, kp, vp, cap, ssdp, csp, mem_k_p, mem_v_p, ks, vs, cas, gvs, ssds, css)
```

```python
import functools
import math

import jax
import jax.numpy as jnp
from jax import lax
from jax.experimental import pallas as pl
from jax.experimental.pallas import tpu as pltpu

F32 = jnp.float32
BF16 = jnp.bfloat16
NEG_INF = float("-inf")

D_MODEL = 1024
CHUNK = 64
CONV_CH = 512
CONV_K = 31
DA_HEADS = 4
DA_HEAD_DIM = 64
DA_QK = 512
GM_WIDTH = 512
GM_GROUPS = 4
SSM_INNER = 512
SSM_HEADS = 8
SSM_HEAD_DIM = 64
SSM_STATE = 128
SSM_CONV_K = 4
SSM_CONV_CH = 1024
N_MEM = 256
X_HEADS = 4
X_HEAD_DIM = 256
PEER_HEADS = 8
PEER_KEYS = 128
PEER_EXPERTS = PEER_KEYS * PEER_KEYS
PEER_TOPK = 16
LANES = 128
SUBLANES = 8
BF16_ROWS = 16
VMEM_LIMIT = 56 * 1024 * 1024


def _cparams(sem):
    return pltpu.CompilerParams(dimension_semantics=sem, vmem_limit_bytes=VMEM_LIMIT)


def _dot(a, b):
    return jnp.dot(a, b, preferred_element_type=F32)


def _dot_nt(a, b):
    return lax.dot_general(a, b, (((1,), (1,)), ((), ())), preferred_element_type=F32)


def _dot_f32(a, b):
    return jnp.dot(a, b, preferred_element_type=F32, precision=lax.Precision.HIGHEST)


def _rms(x, g, eps=1e-6):
    return x * lax.rsqrt(jnp.mean(x * x, axis=-1, keepdims=True) + eps) * g


def _layernorm(x, g, b, eps=1e-5):
    xc = x - jnp.mean(x, axis=-1, keepdims=True)
    return xc * lax.rsqrt(jnp.mean(xc * xc, axis=-1, keepdims=True) + eps) * g + b


def _silu(x):
    return x * jax.nn.sigmoid(x)


def _softplus(x):
    return jnp.maximum(x, 0.0) + jnp.log(1.0 + jnp.exp(-jnp.abs(x)))


def _row_tile(n, pref=512):
    return pref if n % pref == 0 else n


def _ab_in_kernel(x_ref, g_ref, w_ref, glu_ref, q_ref, k32_ref, v32_ref, k16_ref, v16_ref):
    h = _rms(x_ref[...], g_ref[...]).astype(BF16)
    a_val = _dot(h, w_ref[:, 0:512])
    a_gate = _dot(h, w_ref[:, 512:1024])
    glu_ref[...] = a_val * jax.nn.sigmoid(a_gate)
    q = _dot(h, w_ref[:, 1024:1536])
    q_ref[...] = (q * (DA_HEAD_DIM ** -0.5 * math.log2(math.e))).astype(BF16)
    k = _dot(h, w_ref[:, 1536:2048])
    k32_ref[...] = k
    k16_ref[...] = k.astype(BF16)
    v = _dot(h, w_ref[:, 2048:2560])
    v32_ref[...] = v
    v16_ref[...] = v.astype(BF16)


def _ab_in(x, g, w):
    n = x.shape[0]
    tm = _row_tile(n)
    row = lambda c: pl.BlockSpec((tm, c), lambda i: (i, 0))
    full = lambda a: pl.BlockSpec(a.shape, lambda i: (0,) * a.ndim)
    return pl.pallas_call(
        _ab_in_kernel,
        grid=(n // tm,),
        in_specs=[row(D_MODEL), full(g), full(w)],
        out_specs=[row(512)] * 6,
        out_shape=[jax.ShapeDtypeStruct((n, 512), F32), jax.ShapeDtypeStruct((n, 512), BF16),
                   jax.ShapeDtypeStruct((n, 512), F32), jax.ShapeDtypeStruct((n, 512), F32),
                   jax.ShapeDtypeStruct((n, 512), BF16), jax.ShapeDtypeStruct((n, 512), BF16)],
        compiler_params=_cparams(("parallel",)),
        name="ab_in",
    )(x, g, w)


CONV_HALO = 32
CONV_RB = 64


def _conv_a_kernel(glu_ref, halo_ref, ctx_ref, w_ref, b_ref, lg_ref, lb_ref, o_ref, xp_scr, sh_scr, *, tl):
    i = pl.program_id(1)
    pad = CONV_HALO - (CONV_K - 1)

    @pl.when(i == 0)
    def _():
        xp_scr[0:pad, :] = jnp.zeros((pad, CONV_CH), F32)
        xp_scr[pad:CONV_HALO, :] = ctx_ref[0]

    @pl.when(i > 0)
    def _():
        xp_scr[0:CONV_HALO, :] = halo_ref[0]

    xp_scr[CONV_HALO:CONV_HALO + tl, :] = glu_ref[0]
    for s in range(1, SUBLANES):
        sh_scr[s - 1, 0:CONV_HALO + tl - SUBLANES, :] = xp_scr[s:s + CONV_HALO + tl - SUBLANES, :]
    rb = min(CONV_RB, tl)
    for r0 in range(0, tl, rb):
        acc = jnp.broadcast_to(b_ref[...], (rb, CONV_CH))
        for k in range(CONV_K):
            phase = (pad + k) % SUBLANES
            base = r0 + pad + k - phase
            rows = xp_scr[base:base + rb, :] if phase == 0 else sh_scr[phase - 1, base:base + rb, :]
            acc = acc + w_ref[k:k + 1, :] * rows
        y = _layernorm(acc, lg_ref[...], lb_ref[...])
        o_ref[0, r0:r0 + rb, :] = _silu(y).astype(BF16)


def _conv_a(glu, ctx, w, b, lg, lb):
    bsz, l, _ = glu.shape
    tl = _row_tile(l)
    hb = tl // CONV_HALO
    full = lambda a: pl.BlockSpec(a.shape, lambda bi, i: (0,) * a.ndim)
    return pl.pallas_call(
        functools.partial(_conv_a_kernel, tl=tl),
        grid=(bsz, l // tl),
        in_specs=[pl.BlockSpec((1, tl, CONV_CH), lambda bi, i: (bi, i, 0)),
                  pl.BlockSpec((1, CONV_HALO, CONV_CH), lambda bi, i: (bi, jnp.maximum(i * hb - 1, 0), 0)),
                  pl.BlockSpec((1, CONV_K - 1, CONV_CH), lambda bi, i: (bi, 0, 0)),
                  full(w), full(b), full(lg), full(lb)],
        out_specs=pl.BlockSpec((1, tl, CONV_CH), lambda bi, i: (bi, i, 0)),
        out_shape=jax.ShapeDtypeStruct((bsz, l, CONV_CH), BF16),
        scratch_shapes=[pltpu.VMEM((CONV_HALO + tl, CONV_CH), F32),
                        pltpu.VMEM((SUBLANES - 1, CONV_HALO + tl, CONV_CH), F32)],
        compiler_params=_cparams(("parallel", "arbitrary")),
        name="conv_a",
    )(glu, glu, ctx, w, b, lg, lb)


def _lambda(lq1, lk1, lq2, lk2, lam_init):
    return (jnp.exp(jnp.sum(lq1[...] * lk1[...], axis=-1, keepdims=True))
            - jnp.exp(jnp.sum(lq2[...] * lk2[...], axis=-1, keepdims=True)) + lam_init)


def _split_q(q):
    lane = lax.broadcasted_iota(jnp.int32, q.shape, 1)
    z = jnp.zeros_like(q)
    return jnp.concatenate([jnp.where(lane < DA_HEAD_DIM, q, z), jnp.where(lane >= DA_HEAD_DIM, q, z)], axis=0)


def _diff_finish(acc, l, tq, lam, sg, lam_init):
    o = acc[0:tq] / l[0:tq] - lam * (acc[tq:2 * tq] / l[tq:2 * tq])
    return _rms(o, sg) * (1.0 - lam_init)


ATTN_TQ = 512
ATTN_TK = 512


def _attn_prompt_kernel(qi_ref, ki_ref, fl_ref, q_ref, k_ref, v_ref, lq1, lk1, lq2, lk2, sg_ref, o_ref,
                        qs_scr, m_scr, acc_scr, *, tq, tk, lam_init):
    p = pl.program_id(1)
    qi = qi_ref[p]
    ki = ki_ref[p]
    flags = fl_ref[p]

    @pl.when(ki == 0)
    def _():
        for h in range(DA_HEADS):
            qs_scr[h] = _split_q(q_ref[0, :, h * LANES:(h + 1) * LANES])
        m_scr[...] = jnp.full(m_scr.shape, NEG_INF, F32)
        acc_scr[...] = jnp.zeros(acc_scr.shape, F32)

    ones_col = jnp.where(lax.broadcasted_iota(jnp.int32, (tk, LANES), 1) == 0, 1.0, 0.0).astype(BF16)

    def step(masked):
        for h in range(DA_HEADS):
            sl = slice(h * LANES, (h + 1) * LANES)
            s = _dot_nt(qs_scr[h], k_ref[0, :, sl])
            if masked:
                r = lax.broadcasted_iota(jnp.int32, s.shape, 0)
                c = lax.broadcasted_iota(jnp.int32, s.shape, 1)
                q_pos = qi * tq + jnp.where(r >= tq, r - tq, r)
                s = jnp.where(((ki * tk + c) >> 6) <= (q_pos >> 6), s, -1e30)
            m_old = m_scr[h]
            m_new = jnp.maximum(m_old, jnp.broadcast_to(jnp.max(s, axis=-1, keepdims=True), m_old.shape))
            alpha = jnp.exp2(m_old - m_new)
            pr = jnp.exp2(s - jnp.concatenate([m_new] * (tk // LANES), axis=1)).astype(BF16)
            v_aug = jnp.concatenate([v_ref[0, :, sl], ones_col], axis=1)
            acc_scr[h] = jnp.concatenate([alpha, alpha], axis=1) * acc_scr[h] + _dot(pr, v_aug)
            m_scr[h] = m_new

    @pl.when((flags & 1) == 0)
    def _():
        step(False)

    @pl.when((flags & 1) == 1)
    def _():
        step(True)

    @pl.when((flags & 2) != 0)
    def _():
        lam = _lambda(lq1, lk1, lq2, lk2, lam_init)
        for h in range(DA_HEADS):
            acc = acc_scr[h]
            o = _diff_finish(acc[:, 0:LANES], acc[:, LANES:LANES + 1], tq, lam, sg_ref[...], lam_init)
            o_ref[0, :, h * LANES:(h + 1) * LANES] = o.astype(BF16)


def _attn_prompt(q, k, v, lq1, lk1, lq2, lk2, sg, lam_init):
    bsz, l, _ = q.shape
    tq = _row_tile(l, ATTN_TQ)
    tk = _row_tile(l, ATTN_TK)
    pairs = []
    for a in range(l // tq):
        last = ((a + 1) * tq - 1) // tk
        for b in range(last + 1):
            crosses = (b + 1) * tk > a * tq + CHUNK
            pairs.append((a, b, int(crosses) + 2 * int(b == last)))
    tabs = [jnp.asarray([pr[i] for pr in pairs], jnp.int32) for i in range(3)]
    small = lambda a: pl.BlockSpec(a.shape, lambda bi, p, qt, kt, fl: (0,) * a.ndim)
    grid_spec = pltpu.PrefetchScalarGridSpec(
        num_scalar_prefetch=3,
        grid=(bsz, len(pairs)),
        in_specs=[pl.BlockSpec((1, tq, DA_QK), lambda bi, p, qt, kt, fl: (bi, qt[p], 0)),
                  pl.BlockSpec((1, tk, DA_QK), lambda bi, p, qt, kt, fl: (bi, kt[p], 0)),
                  pl.BlockSpec((1, tk, DA_QK), lambda bi, p, qt, kt, fl: (bi, kt[p], 0)),
                  small(lq1), small(lk1), small(lq2), small(lk2), small(sg)],
        out_specs=pl.BlockSpec((1, tq, DA_QK), lambda bi, p, qt, kt, fl: (bi, qt[p], 0)),
        scratch_shapes=[pltpu.VMEM((DA_HEADS, 2 * tq, LANES), BF16), pltpu.VMEM((DA_HEADS, 2 * tq, LANES), F32),
                        pltpu.VMEM((DA_HEADS, 2 * tq, 2 * LANES), F32)],
    )
    return pl.pallas_call(
        functools.partial(_attn_prompt_kernel, tq=tq, tk=tk, lam_init=lam_init),
        grid_spec=grid_spec,
        out_shape=jax.ShapeDtypeStruct((bsz, l, DA_QK), BF16),
        compiler_params=_cparams(("parallel", "arbitrary")),
        name="attn_prompt",
    )(*tabs, q, k, v, lq1, lk1, lq2, lk2, sg)


def _attn_sample_kernel(q_ref, kc_ref, vc_ref, kn_ref, vn_ref, lq1, lk1, lq2, lk2, sg_ref, o_ref,
                        *, tq, past_len, lam_init):
    qs = _split_q(q_ref[0])
    s_c = _dot_nt(qs, kc_ref[0].astype(BF16))
    s_n = _dot_nt(qs, kn_ref[0])
    r = lax.broadcasted_iota(jnp.int32, s_n.shape, 0)
    c = lax.broadcasted_iota(jnp.int32, s_n.shape, 1)
    qrow = jnp.where(r >= tq, r - tq, r)
    s_n = jnp.where(((past_len + c) >> 6) <= ((past_len + qrow) >> 6), s_n, -1e30)
    m = jnp.maximum(jnp.max(s_c, axis=-1, keepdims=True), jnp.max(s_n, axis=-1, keepdims=True))
    p_c = jnp.exp2(s_c - m)
    p_n = jnp.exp2(s_n - m)
    l = jnp.sum(p_c, axis=-1, keepdims=True) + jnp.sum(p_n, axis=-1, keepdims=True)
    acc = _dot(p_c.astype(BF16), vc_ref[0].astype(BF16)) + _dot(p_n.astype(BF16), vn_ref[0])
    lam = _lambda(lq1, lk1, lq2, lk2, lam_init)
    o_ref[0] = _diff_finish(acc, l, tq, lam, sg_ref[...], lam_init).astype(BF16)


def _attn_sample(q, k, v, k_past, v_past, lq1, lk1, lq2, lk2, sg, lam_init):
    bsz, l, _ = q.shape
    past_len = k_past.shape[1]
    small = lambda a: pl.BlockSpec(a.shape, lambda bi, h: (0,) * a.ndim)
    new = pl.BlockSpec((1, l, LANES), lambda bi, h: (bi, 0, h))
    old = pl.BlockSpec((1, past_len, LANES), lambda bi, h: (bi, 0, h))
    return pl.pallas_call(
        functools.partial(_attn_sample_kernel, tq=l, past_len=past_len, lam_init=lam_init),
        grid=(bsz, DA_HEADS),
        in_specs=[new, old, old, new, new, small(lq1), small(lk1), small(lq2), small(lk2), small(sg)],
        out_specs=new,
        out_shape=jax.ShapeDtypeStruct((bsz, l, DA_QK), BF16),
        compiler_params=_cparams(("parallel", "parallel")),
        name="attn_sample",
    )(q, k_past, v_past, k, v, lq1, lk1, lq2, lk2, sg)


def _mem_kv_kernel(m_ref, wk_ref, wv_ref, k32_ref, v32_ref, k16_ref, v16_ref):
    m = m_ref[...].astype(BF16)
    k = _dot(m, wk_ref[0])
    v = _dot(m, wv_ref[0])
    k32_ref[0] = k
    v32_ref[0] = v
    k16_ref[0] = k.astype(BF16)
    v16_ref[0] = v.astype(BF16)


def _mem_kv(mem, wk, wv):
    n = mem.shape[0]
    depth = wk.shape[0]
    tm = _row_tile(n)
    wspec = pl.BlockSpec((1, D_MODEL, D_MODEL), lambda l, i: (l, 0, 0))
    ospec = pl.BlockSpec((1, tm, D_MODEL), lambda l, i: (l, i, 0))
    return pl.pallas_call(
        _mem_kv_kernel,
        grid=(depth, n // tm),
        in_specs=[pl.BlockSpec((tm, D_MODEL), lambda l, i: (i, 0)), wspec, wspec],
        out_specs=[ospec] * 4,
        out_shape=[jax.ShapeDtypeStruct((depth, n, D_MODEL), F32)] * 2
        + [jax.ShapeDtypeStruct((depth, n, D_MODEL), BF16)] * 2,
        compiler_params=_cparams(("parallel", "parallel")),
        name="mem_kv",
    )(mem, wk, wv)


def _cross_kernel(x_ref, a_ref, b_ref, wm_ref, g_ref, wq_ref, wo_ref, mk_ref, mv_ref, o_ref):
    x = x_ref[0] + _dot(a_ref[0], wm_ref[0:512, :]) + _dot(b_ref[0], wm_ref[512:1024, :])
    h = _rms(x, g_ref[...]).astype(BF16)
    q = _dot(h, wq_ref[...]).astype(BF16)
    outs = []
    for hd in range(X_HEADS):
        sl = slice(hd * X_HEAD_DIM, (hd + 1) * X_HEAD_DIM)
        s = _dot_nt(q[:, sl], mk_ref[0, 0, :, sl].astype(BF16)) * (X_HEAD_DIM ** -0.5)
        s = s - jnp.max(s, axis=-1, keepdims=True)
        e = jnp.exp(s)
        p = (e / jnp.sum(e, axis=-1, keepdims=True)).astype(BF16)
        outs.append(_dot(p, mv_ref[0, 0, :, sl].astype(BF16)).astype(BF16))
    o = jnp.concatenate(outs, axis=-1)
    o_ref[0] = x + _dot(o, wo_ref[...])


def _cross(x, a, b, wm, g, wq, wo, mk, mv, layer):
    bsz, l, _ = x.shape
    tm = _row_tile(l)
    full = lambda arr: pl.BlockSpec(arr.shape, lambda bi, i: (0,) * arr.ndim)
    xs = pl.BlockSpec((1, tm, D_MODEL), lambda bi, i: (bi, i, 0))
    hs = pl.BlockSpec((1, tm, 512), lambda bi, i: (bi, i, 0))
    ms = pl.BlockSpec((1, 1, N_MEM, D_MODEL), lambda bi, i: (layer, bi, 0, 0))
    return pl.pallas_call(
        _cross_kernel,
        grid=(bsz, l // tm),
        in_specs=[xs, hs, hs, full(wm), full(g), full(wq), full(wo), ms, ms],
        out_specs=xs,
        out_shape=jax.ShapeDtypeStruct((bsz, l, D_MODEL), F32),
        compiler_params=_cparams(("parallel", "parallel")),
        name="cross_attn",
    )(x, a, b, wm, g, wq, wo, mk, mv)


PEER_T = 512
PEER_DENSE_T = 1024
PEER_EC = 1024
PEER_SUB = 512
N_CAND = 56


def _top_values(s, k):
    out = []
    work = s
    for _ in range(k):
        m = jnp.max(work, axis=0, keepdims=True)
        out.append(m)
        work = jnp.where(work >= m, NEG_INF, work)
    return out


def _bf16_value(x):
    return x.astype(BF16).astype(F32)


def _bf16_below(x):
    return lax.bitcast_convert_type(lax.bitcast_convert_type(x, jnp.int32) - 0x10000, F32)


def _bf16_pair_word(x):
    bits = lax.bitcast_convert_type(x, jnp.uint32)
    return lax.bitcast_convert_type((bits & jnp.uint32(0xFFFF0000)) | (bits >> 16), jnp.int32)


EXP_FLOOR = -80.0


def _sorting_network(n):
    def merge(lo, hi, r):
        step = r * 2
        if step < hi - lo:
            yield from merge(lo, hi, step)
            yield from merge(lo + r, hi, step)
            yield from [(i, i + r) for i in range(lo + r, hi - r, step)]
        else:
            yield (lo, lo + r)

    def sort(lo, hi):
        if hi - lo >= 1:
            mid = lo + (hi - lo) // 2
            yield from sort(lo, mid)
            yield from sort(mid + 1, hi)
            yield from merge(lo, hi, 1)

    return list(sort(0, n - 1))


KEY_VREGS = PEER_KEYS // SUBLANES
KEY_SORT = _sorting_network(KEY_VREGS)


def _top_values_keys(s, k):
    rows = [s[SUBLANES * r:SUBLANES * (r + 1)] for r in range(KEY_VREGS)]
    for i, j in KEY_SORT:
        rows[i], rows[j] = jnp.maximum(rows[i], rows[j]), jnp.minimum(rows[i], rows[j])
    vals = []
    for r in range(k):
        m = jnp.max(rows[0], axis=0, keepdims=True)
        vals.append(m)
        still_needed = k - 1 - r
        if still_needed == 0:
            break
        hit = rows[0] >= m
        for d in range(min(still_needed, KEY_VREGS)):
            below = rows[d + 1] if d + 1 < KEY_VREGS else NEG_INF
            rows[d] = jnp.where(hit, below, rows[d])
    return vals


def _ranked_weights(s, k):
    vals = _top_values_keys(s, k)
    nums = [jnp.ones_like(vals[0])]
    for r in range(1, k):
        v = _bf16_value(jnp.exp(jnp.maximum(vals[r] - vals[0], EXP_FLOOR)))
        nums.append(jnp.minimum(v, _bf16_below(nums[-1])))
    placed = jnp.zeros(s.shape, F32)
    for r in range(k):
        placed = jnp.where(s == vals[r], nums[r], placed)
    return vals, nums, placed


def _peer_route_kernel(x_ref, g_ref, wq_ref, sk_ref, ht_ref, ecut_ref, e0_ref, e1_ref, e1_scr):
    h = _rms(x_ref[...], g_ref[...])
    ht_ref[...] = h.T.astype(BF16)
    q = _dot(h.astype(BF16), wq_ref[...]).astype(BF16)
    t = h.shape[0]
    half = PEER_KEYS // 2
    for hd in range(PEER_HEADS):
        base = hd * 2 * PEER_KEYS
        s0_all = _dot_nt(sk_ref[0], q[:, base:base + PEER_KEYS])
        s1_all = _dot_nt(sk_ref[1], q[:, base + PEER_KEYS:base + 2 * PEER_KEYS])
        for tc in range(t // LANES):
            cols = slice(tc * LANES, (tc + 1) * LANES)
            s0 = s0_all[:, cols]
            u0 = _top_values_keys(s0, PEER_TOPK + 1)
            u1, n1, e1 = _ranked_weights(s1_all[:, cols], PEER_TOPK + 1)
            cands = [u0[a] + u1[b] for a in range(PEER_TOPK + 1) for b in range(PEER_TOPK + 1)
                     if (a + 1) * (b + 1) <= PEER_TOPK + 1]
            cands += [jnp.full((1, LANES), NEG_INF, F32)] * (N_CAND - len(cands))
            best = _top_values(jnp.concatenate(cands, axis=0), PEER_TOPK + 1)
            thr = 0.5 * (best[PEER_TOPK - 1] + best[PEER_TOPK])
            z = jnp.ones_like(best[0])
            for b in best[1:PEER_TOPK]:
                z = z + jnp.exp(b - best[0])
            cut = thr - s0
            ecut = jnp.full(cut.shape, 2.0, F32)
            for r in range(PEER_TOPK + 1):
                ecut = jnp.where(u1[r] > cut, n1[r], ecut)
            ecut_ref[hd, :, cols] = _bf16_pair_word(ecut)
            e0_ref[hd, :, cols] = _bf16_pair_word(
                _bf16_value(jnp.exp(jnp.maximum(s0 - u0[0], EXP_FLOOR)) / z))
            e1_scr[hd, tc] = e1
            lo = lax.bitcast_convert_type(e1_scr[hd, tc, pl.ds(0, half, stride=2), :], jnp.uint32) >> 16
            hi = (lax.bitcast_convert_type(e1_scr[hd, tc, pl.ds(1, half, stride=2), :], jnp.uint32)
                  & jnp.uint32(0xFFFF0000))
            e1_ref[hd, :, cols] = lax.bitcast_convert_type(hi | lo, jnp.int32)


def _peer_route(x, g, wq, sk):
    n = x.shape[0]
    t = _row_tile(n, PEER_T)
    full = lambda a: pl.BlockSpec(a.shape, lambda i: (0,) * a.ndim)
    rspec = pl.BlockSpec((PEER_HEADS, PEER_KEYS, t), lambda i: (0, 0, i))
    rshape = jax.ShapeDtypeStruct((PEER_HEADS, PEER_KEYS, n), jnp.int32)
    e1spec = pl.BlockSpec((PEER_HEADS, PEER_KEYS // 2, t), lambda i: (0, 0, i))
    e1shape = jax.ShapeDtypeStruct((PEER_HEADS, PEER_KEYS // 2, n), jnp.int32)
    return pl.pallas_call(
        _peer_route_kernel,
        grid=(n // t,),
        in_specs=[pl.BlockSpec((t, D_MODEL), lambda i: (i, 0)), full(g), full(wq), full(sk)],
        out_specs=[pl.BlockSpec((D_MODEL, t), lambda i: (0, i)), rspec, rspec, e1spec],
        out_shape=[jax.ShapeDtypeStruct((D_MODEL, n), BF16), rshape, rshape, e1shape],
        scratch_shapes=[pltpu.VMEM((PEER_HEADS, t // LANES, PEER_KEYS, LANES), F32)],
        compiler_params=_cparams(("parallel",)),
        name="peer_route",
    )(x, g, wq, sk)


def _peer_dense_kernel(ht_ref, ecut_ref, e0_ref, e1_ref, eu_ref, evt_ref, x_ref, *rest, final):
    if final:
        fg_ref, o_ref, acc_scr, at_scr, w_scr = rest
    else:
        o_ref, acc_scr, at_scr, w_scr = rest
    j = pl.program_id(1)
    rows_step = PEER_EC // PEER_KEYS
    rows_sub = PEER_SUB // PEER_KEYS

    @pl.when(j == 0)
    def _():
        acc_scr[...] = jnp.zeros(acc_scr.shape, F32)

    t = ht_ref.shape[1]

    def packed_row(row, cols):
        return pltpu.bitcast(jnp.broadcast_to(row[:, cols], (PEER_KEYS // 2, LANES)), BF16)

    def hidden_and_weights(sc):
        rows = slice(sc * PEER_SUB, (sc + 1) * PEER_SUB)
        for ii in range(rows_sub):
            i0 = j * rows_step + sc * rows_sub + ii
            r0 = sc * PEER_SUB + ii * PEER_KEYS
            ecut_rows = [ecut_ref[hd, pl.ds(i0, 1), :] for hd in range(PEER_HEADS)]
            e0_rows = [e0_ref[hd, pl.ds(i0, 1), :] for hd in range(PEER_HEADS)]
            for tc in range(t // LANES):
                cols = slice(tc * LANES, (tc + 1) * LANES)
                w = None
                for hd in range(PEER_HEADS):
                    e1 = pltpu.bitcast(e1_ref[hd, :, cols], BF16)
                    term = jnp.where(e1 >= packed_row(ecut_rows[hd], cols), e1, jnp.zeros_like(e1))
                    term = term * packed_row(e0_rows[hd], cols)
                    w = term if w is None else w + term
                w_scr[r0:r0 + PEER_KEYS, cols] = w
        at_scr[rows, :] = _dot(eu_ref[rows, :], ht_ref[...]).astype(BF16)

    def expert_output(sc):
        rows = slice(sc * PEER_SUB, (sc + 1) * PEER_SUB)
        return _dot(evt_ref[0, :, rows], w_scr[rows, :] * jax.nn.gelu(at_scr[rows, :]))

    n_sub = PEER_EC // PEER_SUB
    hidden_and_weights(0)
    total = None
    for sc in range(n_sub):
        if sc + 1 < n_sub:
            hidden_and_weights(sc + 1)
        part = expert_output(sc)
        total = part if total is None else total + part
    acc_scr[...] += total

    @pl.when(j == pl.num_programs(1) - 1)
    def _():
        y = x_ref[...] + acc_scr[...].T
        o_ref[...] = _rms(y, fg_ref[...]) if final else y


def _peer_dense(x, ht, ecut, e0, e1, eu, evt, final_g=None):
    n = x.shape[0]
    t = _row_tile(n, PEER_DENSE_T)
    once = pl.Buffered(1)
    rspec = pl.BlockSpec((PEER_HEADS, PEER_KEYS, t), lambda i, j: (0, 0, i))
    e1spec = pl.BlockSpec((PEER_HEADS, PEER_KEYS // 2, t), lambda i, j: (0, 0, i), pipeline_mode=once)
    xspec = pl.BlockSpec((t, D_MODEL), lambda i, j: (i, 0))
    final = final_g is not None
    extra = [final_g] if final else []
    return pl.pallas_call(
        functools.partial(_peer_dense_kernel, final=final),
        grid=(n // t, PEER_EXPERTS // PEER_EC),
        in_specs=[pl.BlockSpec((D_MODEL, t), lambda i, j: (0, i), pipeline_mode=once), rspec, rspec, e1spec,
                  pl.BlockSpec((PEER_EC, D_MODEL), lambda i, j: (j, 0)),
                  pl.BlockSpec((1, D_MODEL, PEER_EC), lambda i, j: (j, 0, 0)),
                  pl.BlockSpec((t, D_MODEL), lambda i, j: (i, 0), pipeline_mode=once)]
        + [pl.BlockSpec(g.shape, lambda i, j: (0, 0)) for g in extra],
        out_specs=xspec,
        out_shape=jax.ShapeDtypeStruct((n, D_MODEL), F32),
        scratch_shapes=[pltpu.VMEM((D_MODEL, t), F32), pltpu.VMEM((PEER_EC, t), BF16),
                        pltpu.VMEM((PEER_EC, t), BF16)],
        compiler_params=_cparams(("parallel", "arbitrary")),
        name="peer_dense",
    )(ht, ecut, e0, e1, eu, evt, x, *extra)


def _cd_in_kernel(x_ref, g_ref, wc_ref, wg_ref, wx_ref, wd_ref, lg_ref, lb_ref,
                  u_ref, v_ref, gate_ref, xbc_ref, dt_ref):
    h = _rms(x_ref[...], g_ref[...]).astype(BF16)
    u_ref[...] = jax.nn.gelu(_dot(h, wc_ref[:, 0:512]))
    v_ref[...] = _layernorm(jax.nn.gelu(_dot(h, wc_ref[:, 512:1024])), lg_ref[...], lb_ref[...])
    gate_ref[...] = _dot(h, wg_ref[...])
    xbc_ref[...] = _dot(h, wx_ref[...])
    dt_ref[...] = _dot(h, wd_ref[...])


def _cd_in(x, g, wc, wg, wx, wd, lg, lb):
    n = x.shape[0]
    tm = _row_tile(n)
    row = lambda c: pl.BlockSpec((tm, c), lambda i: (i, 0))
    full = lambda a: pl.BlockSpec(a.shape, lambda i: (0,) * a.ndim)
    widths = (512, 512, 512, SSM_CONV_CH, LANES)
    return pl.pallas_call(
        _cd_in_kernel,
        grid=(n // tm,),
        in_specs=[row(D_MODEL), full(g), full(wc), full(wg), full(wx), full(wd), full(lg), full(lb)],
        out_specs=[row(c) for c in widths],
        out_shape=[jax.ShapeDtypeStruct((n, c), F32) for c in widths],
        compiler_params=_cparams(("parallel",)),
        name="cd_in",
    )(x, g, wc, wg, wx, wd, lg, lb)


def _gmlp_kernel(u_ref, v_ref, ws_ref, bs_ref, o_ref, *, gm_len, n_chunks):
    r = lax.broadcasted_iota(jnp.int32, (gm_len, gm_len), 0)
    c = lax.broadcasted_iota(jnp.int32, (gm_len, gm_len), 1)
    for g in range(GM_GROUPS):
        w = jnp.where(r >= c, ws_ref[g], 0.0).astype(BF16)
        ch = slice(g * LANES, (g + 1) * LANES)
        for ci in range(n_chunks):
            rows = slice(ci * gm_len, (ci + 1) * gm_len)
            mixed = _dot(w, v_ref[0, rows, ch].astype(BF16)) + bs_ref[:, g:g + 1]
            o_ref[0, rows, ch] = (u_ref[0, rows, ch] * mixed).astype(BF16)


def _gmlp(u, v, ws, bs_t, gm_len):
    bsz, l, _ = u.shape
    tl = _row_tile(l)
    full = lambda a: pl.BlockSpec(a.shape, lambda bi, i: (0,) * a.ndim)
    spec = pl.BlockSpec((1, tl, GM_WIDTH), lambda bi, i: (bi, i, 0))
    return pl.pallas_call(
        functools.partial(_gmlp_kernel, gm_len=gm_len, n_chunks=tl // gm_len),
        grid=(bsz, l // tl),
        in_specs=[spec, spec, full(ws), full(bs_t)],
        out_specs=spec,
        out_shape=jax.ShapeDtypeStruct((bsz, l, GM_WIDTH), BF16),
        compiler_params=_cparams(("parallel", "parallel")),
        name="gmlp",
    )(u, v, ws, bs_t)


SSD_HALO = 8
SSD_PAIRS = SSM_HEADS // 2
SSD_GROUP_W = SSM_INNER // 2


def _ssd_kernel(xbc_ref, halo_ref, ctx_ref, gate_ref, dt_ref, h0_ref, cw_ref, cb_ref, dtb_ref, alog_ref,
                dsk_ref, ng_ref, y_ref, fin_ref, xp_scr, st_scr, *, tl):
    i = pl.program_id(1)
    pad = SSD_HALO - (SSM_CONV_K - 1)
    q = CHUNK

    @pl.when(i == 0)
    def _():
        xp_scr[pad:SSD_HALO, :] = ctx_ref[0]
        for k in range(SSD_PAIRS):
            st_scr[k] = jnp.concatenate([h0_ref[0, 2 * k], h0_ref[0, 2 * k + 1]], axis=0).T

    @pl.when(i > 0)
    def _():
        xp_scr[0:SSD_HALO, :] = halo_ref[0]

    xp_scr[SSD_HALO:SSD_HALO + tl, :] = xbc_ref[0]

    lane128 = lax.broadcasted_iota(jnp.int32, (q, LANES), 1)
    row128 = lax.broadcasted_iota(jnp.int32, (q, LANES), 0)
    er = lax.broadcasted_iota(jnp.int32, (LANES, SSM_INNER), 0)
    ec = lax.broadcasted_iota(jnp.int32, (LANES, SSM_INNER), 1)
    expand = jnp.where((ec >> 6) == er, 1.0, 0.0).astype(F32)
    tr = lax.broadcasted_iota(jnp.int32, (q, q), 0)
    tc = lax.broadcasted_iota(jnp.int32, (q, q), 1)
    ltri = jnp.where(tr >= tc, 1.0, 0.0).astype(F32)
    a_neg = -jnp.exp(alog_ref[...])
    zeros_q = jnp.zeros((q, LANES), F32)

    for ci in range(tl // q):
        r0 = ci * q
        acc = jnp.broadcast_to(cb_ref[...], (q, SSM_CONV_CH))
        for k in range(SSM_CONV_K):
            acc = acc + cw_ref[k:k + 1, :] * xp_scr[r0 + pad + k:r0 + pad + k + q, :]
        xc = _silu(acc)
        xs = xc[:, 0:SSM_INNER]
        dt = jnp.where(lane128 < SSM_HEADS, _softplus(dt_ref[0, r0:r0 + q, :] + dtb_ref[...]), 0.0)
        acs = _dot_f32(ltri, dt * a_neg)
        acs_e = _dot_f32(acs, expand)
        dt_e = _dot_f32(dt, expand)
        tot_e = acs_e[q - 1:q, :]
        xdt = xs * dt_e
        xd = (xdt * jnp.exp(tot_e - acs_e)).astype(BF16)
        eacs = jnp.exp(acs_e)
        cdec = jnp.exp(tot_e)
        ys = []
        for k in range(SSD_PAIRS):
            g = k // 2
            blk = slice(k * LANES, (k + 1) * LANES)
            bm = xc[:, SSM_INNER + g * SSM_STATE:SSM_INNER + (g + 1) * SSM_STATE]
            cm = xc[:, SSM_INNER + 2 * SSM_STATE + g * SSM_STATE:
                    SSM_INNER + 2 * SSM_STATE + (g + 1) * SSM_STATE].astype(BF16)
            cb2 = _dot_nt(cm, jnp.concatenate([bm, bm], axis=0).astype(BF16))
            a_blk = acs_e[:, blk]
            a_row = jnp.sum(jnp.where(row128 == (lane128 & (q - 1)), a_blk, 0.0), axis=0, keepdims=True)
            lmat = jnp.exp(jnp.where((lane128 & (q - 1)) <= row128, a_blk - a_row, NEG_INF))
            sc = (cb2 * lmat).astype(BF16)
            x_blk = xdt[:, blk]
            rhs = jnp.concatenate([jnp.where(lane128 < q, x_blk, 0.0), jnp.where(lane128 >= q, x_blk, 0.0)],
                                  axis=0).astype(BF16)
            prev = st_scr[k]
            y_pair = _dot(sc, rhs) + _dot(cm, prev.astype(BF16)) * eacs[:, blk]
            ys.append(y_pair)
            bt = jnp.concatenate([bm, zeros_q], axis=0).T.astype(BF16)
            xd_pad = jnp.concatenate([xd[:, blk], zeros_q.astype(BF16)], axis=0)
            st_scr[k] = prev * cdec[:, blk] + _dot(bt, xd_pad)
        y = jnp.concatenate(ys, axis=-1) + dsk_ref[...] * xs
        y = y * _silu(gate_ref[0, r0:r0 + q, :])
        outs = []
        for g in range(2):
            seg = y[:, g * SSD_GROUP_W:(g + 1) * SSD_GROUP_W]
            outs.append(seg * lax.rsqrt(jnp.mean(seg * seg, axis=-1, keepdims=True) + 1e-6))
        y_ref[0, r0:r0 + q, :] = (jnp.concatenate(outs, axis=-1) * ng_ref[...]).astype(BF16)

    @pl.when(i == pl.num_programs(1) - 1)
    def _():
        for k in range(SSD_PAIRS):
            st = st_scr[k].T
            fin_ref[0, 2 * k] = st[0:SSM_HEAD_DIM]
            fin_ref[0, 2 * k + 1] = st[SSM_HEAD_DIM:2 * SSM_HEAD_DIM]


def _ssd(xbc, ctx, gate, dt, h0, cw, cb, dtb, alog, dsk, ng):
    bsz, l, _ = xbc.shape
    tl = 256 if l % 256 == 0 else l
    hb = tl // SSD_HALO
    full = lambda a: pl.BlockSpec(a.shape, lambda bi, i: (0,) * a.ndim)
    tile = lambda c: pl.BlockSpec((1, tl, c), lambda bi, i: (bi, i, 0))
    stspec = pl.BlockSpec((1, SSM_HEADS, SSM_HEAD_DIM, SSM_STATE), lambda bi, i: (bi, 0, 0, 0))
    return pl.pallas_call(
        functools.partial(_ssd_kernel, tl=tl),
        grid=(bsz, l // tl),
        in_specs=[tile(SSM_CONV_CH),
                  pl.BlockSpec((1, SSD_HALO, SSM_CONV_CH), lambda bi, i: (bi, jnp.maximum(i * hb - 1, 0), 0)),
                  pl.BlockSpec((1, SSM_CONV_K - 1, SSM_CONV_CH), lambda bi, i: (bi, 0, 0)),
                  tile(SSM_INNER), tile(LANES), stspec,
                  full(cw), full(cb), full(dtb), full(alog), full(dsk), full(ng)],
        out_specs=[tile(SSM_INNER), stspec],
        out_shape=[jax.ShapeDtypeStruct((bsz, l, SSM_INNER), BF16),
                   jax.ShapeDtypeStruct((bsz, SSM_HEADS, SSM_HEAD_DIM, SSM_STATE), F32)],
        scratch_shapes=[pltpu.VMEM((SSD_HALO + tl, SSM_CONV_CH), F32),
                        pltpu.VMEM((SSD_PAIRS, SSM_STATE, LANES), F32)],
        compiler_params=_cparams(("parallel", "arbitrary")),
        name="ssd",
    )(xbc, xbc, ctx, gate, dt, h0, cw, cb, dtb, alog, dsk, ng)


def _blocked_transpose(table):
    depth, _, d = table.shape
    return table.reshape(depth, PEER_EXPERTS // PEER_EC, PEER_EC, d).swapaxes(2, 3)


def _pad_lanes(a, width=LANES):
    return jnp.pad(a, ((0, 0), (0, width - a.shape[-1])))


def _trunk(x, mem_k, mem_v, attn_k, attn_v, conv_a, ssd_st, conv_ssm, gm_len, p):
    bsz, l, _ = x.shape
    n = bsz * l
    row = lambda a: a.reshape(1, -1)
    x2 = x.reshape(n, D_MODEL)

    lam_init = 0.8 - 0.6 * math.exp(-0.3 * 0)
    glu, q16, k32, v32, k16, v16 = _ab_in(x2, row(p["norm_mix_g"][0]), p["w_in_ab"])
    glu3 = glu.reshape(bsz, l, CONV_CH)
    ctx_a = jnp.zeros((bsz, CONV_K - 1, CONV_CH), F32) if conv_a is None else conv_a[0]
    ca = _conv_a(glu3, ctx_a, p["conv_a_w"], row(p["conv_a_b"]), row(p["ln_a_g"]), row(p["ln_a_b"]))
    lam_args = (row(p["lam_q1"]), row(p["lam_k1"]), row(p["lam_q2"]), row(p["lam_k2"]), row(p["subln_g"]))
    shp3 = lambda a: a.reshape(bsz, l, DA_QK)
    if attn_k is None:
        o = _attn_prompt(shp3(q16), shp3(k16), shp3(v16), *lam_args, lam_init)
    else:
        past = attn_k.shape[2]
        o = _attn_sample(shp3(q16), shp3(k16), shp3(v16), attn_k[0].reshape(bsz, past, DA_QK),
                         attn_v[0].reshape(bsz, past, DA_QK), *lam_args, lam_init)
    new_k = k32.reshape(1, bsz, l, DA_HEADS, 2 * DA_HEAD_DIM)
    new_v = v32.reshape(1, bsz, l, DA_HEADS, 2 * DA_HEAD_DIM)
    new_ca = glu3[:, l - (CONV_K - 1):][None]

    def tail(x2, a, b, w_mix, layer, final_g=None):
        x3 = _cross(x2.reshape(bsz, l, D_MODEL), a, b, w_mix, row(p["norm_cross_g"][layer]), p["w_xq"][layer],
                    p["w_xo"][layer], mem_k, mem_v, layer).reshape(n, D_MODEL)
        routed = _peer_route(x3, row(p["norm_ffn_g"][layer]), p["w_pq"][layer], p["sub_keys"][layer])
        return _peer_dense(x3, *routed, p["expert_u"][layer], p["expert_vt"][layer], final_g)

    x2 = tail(x2, ca, o, p["w_out_ab"], 0)

    u, vln, gate, xbc, dt = _cd_in(x2, row(p["norm_mix_g"][1]), p["w_cd_c"], p["w_cd_gate"], p["w_cd_xbc"],
                                   p["w_cd_dt"], row(p["ln_c_g"]), row(p["ln_c_b"]))
    shp = lambda a: a.reshape(bsz, l, a.shape[-1])
    c_out = _gmlp(shp(u), shp(vln), p["gm_w_s"][:, :gm_len, :gm_len], p["gm_b_s"][:, :gm_len].T, gm_len)
    xbc3 = shp(xbc)
    ctx_d = jnp.zeros((bsz, SSM_CONV_K - 1, SSM_CONV_CH), F32) if conv_ssm is None else conv_ssm[0]
    h0 = jnp.zeros((bsz, SSM_HEADS, SSM_HEAD_DIM, SSM_STATE), F32) if ssd_st is None else ssd_st[0]
    y, fin = _ssd(xbc3, ctx_d, shp(gate), shp(dt), h0, p["conv_d_w"], row(p["conv_d_b"]),
                  _pad_lanes(row(p["dt_bias"])), _pad_lanes(row(p["a_log"])),
                  row(jnp.repeat(p["d_skip"], SSM_HEAD_DIM)), row(p["norm_d_g"]))
    y_out = tail(x2, c_out, y, p["w_out_cd"], 1, row(p["norm_final_g"])).reshape(bsz, l, D_MODEL)
    new_gv = vln.reshape(1, bsz, l, GM_GROUPS, GM_WIDTH // GM_GROUPS)
    new_cs = xbc3[:, l - (SSM_CONV_K - 1):][None]
    return y_out, new_k, new_v, new_ca, new_gv, fin[None], new_cs


def kernel(x_prompt, x_sample, cache_attn_k, cache_attn_v, state_conv_a, state_ssd, state_conv_ssm, cache_mem_k, cache_mem_v, mem_prompt, norm_mix_g, norm_cross_g, norm_ffn_g, norm_final_g, w_in_ab, conv_a_w, conv_a_b, ln_a_g, ln_a_b, lam_q1, lam_k1, lam_q2, lam_k2, subln_g, w_out_ab, w_in_cd, ln_c_g, ln_c_b, gm_w_s, gm_b_s, conv_d_w, conv_d_b, dt_bias, a_log, d_skip, norm_d_g, w_out_cd, w_xq, w_xk, w_xv, w_xo, w_pq, sub_keys, expert_u, expert_v):
    bf = lambda a: a.astype(BF16)
    w_cd = w_in_cd[0]
    p = {
        "norm_mix_g": norm_mix_g, "norm_cross_g": norm_cross_g, "norm_ffn_g": norm_ffn_g,
        "norm_final_g": norm_final_g,
        "w_in_ab": bf(w_in_ab[0]), "conv_a_w": conv_a_w[0], "conv_a_b": conv_a_b[0],
        "ln_a_g": ln_a_g[0], "ln_a_b": ln_a_b[0],
        "lam_q1": lam_q1[0], "lam_k1": lam_k1[0], "lam_q2": lam_q2[0], "lam_k2": lam_k2[0],
        "subln_g": subln_g[0], "w_out_ab": bf(w_out_ab[0]),
        "w_cd_c": bf(w_cd[:, 0:1024]), "w_cd_gate": bf(w_cd[:, 1024:1536]), "w_cd_xbc": bf(w_cd[:, 1536:2560]),
        "w_cd_dt": bf(_pad_lanes(w_cd[:, 2560:2568])),
        "ln_c_g": ln_c_g[0], "ln_c_b": ln_c_b[0], "gm_w_s": gm_w_s[0], "gm_b_s": gm_b_s[0],
        "conv_d_w": conv_d_w[0], "conv_d_b": conv_d_b[0], "dt_bias": dt_bias[0], "a_log": a_log[0],
        "d_skip": d_skip[0], "norm_d_g": norm_d_g[0], "w_out_cd": bf(w_out_cd[0]),
        "w_xq": bf(w_xq), "w_xo": bf(w_xo), "w_pq": bf(w_pq), "sub_keys": bf(sub_keys),
        "expert_u": bf(expert_u), "expert_vt": _blocked_transpose(bf(expert_v)),
    }
    bsz, seq, _ = x_prompt.shape
    dec_b, dec_l, _ = x_sample.shape
    depth = w_xk.shape[0]

    mk32, mv32, mk16, mv16 = _mem_kv(mem_prompt.reshape(bsz * N_MEM, D_MODEL), bf(w_xk), bf(w_xv))
    mem_k_p = mk32.reshape(depth, bsz, N_MEM, X_HEADS, X_HEAD_DIM)
    mem_v_p = mv32.reshape(depth, bsz, N_MEM, X_HEADS, X_HEAD_DIM)
    y_prompt, kp, vp, cap, _, ssdp, csp = _trunk(
        x_prompt, mk16.reshape(depth, bsz, N_MEM, D_MODEL), mv16.reshape(depth, bsz, N_MEM, D_MODEL),
        None, None, None, None, None, 2 * CHUNK, p)

    y_sample, ks, vs, cas, gvs, ssds, css = _trunk(
        x_sample, cache_mem_k.reshape(depth, dec_b, N_MEM, D_MODEL), cache_mem_v.reshape(depth, dec_b, N_MEM, D_MODEL),
        cache_attn_k, cache_attn_v, state_conv_a, state_ssd, state_conv_ssm, dec_l, p)
    return (y_prompt, y_sample, kp, vp, cap, ssdp, csp, mem_k_p, mem_v_p, ks, vs, cas, gvs, ssds, css)
```

```python
import functools
import math

import jax
import jax.numpy as jnp
from jax import lax
from jax.experimental import pallas as pl
from jax.experimental.pallas import tpu as pltpu

F32 = jnp.float32
BF16 = jnp.bfloat16
NEG_INF = float("-inf")

D_MODEL = 1024
CHUNK = 64
CONV_CH = 512
CONV_K = 31
DA_HEADS = 4
DA_HEAD_DIM = 64
DA_QK = 512
GM_WIDTH = 512
GM_GROUPS = 4
SSM_INNER = 512
SSM_HEADS = 8
SSM_HEAD_DIM = 64
SSM_STATE = 128
SSM_CONV_K = 4
SSM_CONV_CH = 1024
N_MEM = 256
X_HEADS = 4
X_HEAD_DIM = 256
PEER_HEADS = 8
PEER_KEYS = 128
PEER_EXPERTS = PEER_KEYS * PEER_KEYS
PEER_TOPK = 16
LANES = 128
SUBLANES = 8
BF16_ROWS = 16
VMEM_LIMIT = 56 * 1024 * 1024


def _cparams(sem):
    return pltpu.CompilerParams(dimension_semantics=sem, vmem_limit_bytes=VMEM_LIMIT)


def _dot(a, b):
    return jnp.dot(a, b, preferred_element_type=F32)


def _dot_nt(a, b):
    return lax.dot_general(a, b, (((1,), (1,)), ((), ())), preferred_element_type=F32)


def _dot_f32(a, b):
    return jnp.dot(a, b, preferred_element_type=F32, precision=lax.Precision.HIGHEST)


def _rms(x, g, eps=1e-6):
    return x * lax.rsqrt(jnp.mean(x * x, axis=-1, keepdims=True) + eps) * g


def _layernorm(x, g, b, eps=1e-5):
    xc = x - jnp.mean(x, axis=-1, keepdims=True)
    return xc * lax.rsqrt(jnp.mean(xc * xc, axis=-1, keepdims=True) + eps) * g + b


def _silu(x):
    return x * jax.nn.sigmoid(x)


def _softplus(x):
    return jnp.maximum(x, 0.0) + jnp.log(1.0 + jnp.exp(-jnp.abs(x)))


def _row_tile(n, pref=512):
    return pref if n % pref == 0 else n


def _ab_in_kernel(x_ref, g_ref, w_ref, glu_ref, q_ref, k32_ref, v32_ref, k16_ref, v16_ref):
    h = _rms(x_ref[...], g_ref[...]).astype(BF16)
    a_val = _dot(h, w_ref[:, 0:512])
    a_gate = _dot(h, w_ref[:, 512:1024])
    glu_ref[...] = a_val * jax.nn.sigmoid(a_gate)
    q = _dot(h, w_ref[:, 1024:1536])
    q_ref[...] = (q * (DA_HEAD_DIM ** -0.5 * math.log2(math.e))).astype(BF16)
    k = _dot(h, w_ref[:, 1536:2048])
    k32_ref[...] = k
    k16_ref[...] = k.astype(BF16)
    v = _dot(h, w_ref[:, 2048:2560])
    v32_ref[...] = v
    v16_ref[...] = v.astype(BF16)


def _ab_in(x, g, w):
    n = x.shape[0]
    tm = _row_tile(n)
    row = lambda c: pl.BlockSpec((tm, c), lambda i: (i, 0))
    full = lambda a: pl.BlockSpec(a.shape, lambda i: (0,) * a.ndim)
    return pl.pallas_call(
        _ab_in_kernel,
        grid=(n // tm,),
        in_specs=[row(D_MODEL), full(g), full(w)],
        out_specs=[row(512)] * 6,
        out_shape=[jax.ShapeDtypeStruct((n, 512), F32), jax.ShapeDtypeStruct((n, 512), BF16),
                   jax.ShapeDtypeStruct((n, 512), F32), jax.ShapeDtypeStruct((n, 512), F32),
                   jax.ShapeDtypeStruct((n, 512), BF16), jax.ShapeDtypeStruct((n, 512), BF16)],
        compiler_params=_cparams(("parallel",)),
        name="ab_in",
    )(x, g, w)


CONV_HALO = 32
CONV_RB = 64


def _conv_a_kernel(glu_ref, halo_ref, ctx_ref, w_ref, b_ref, lg_ref, lb_ref, o_ref, xp_scr, sh_scr, *, tl):
    i = pl.program_id(1)
    pad = CONV_HALO - (CONV_K - 1)

    @pl.when(i == 0)
    def _():
        xp_scr[0:pad, :] = jnp.zeros((pad, CONV_CH), F32)
        xp_scr[pad:CONV_HALO, :] = ctx_ref[0]

    @pl.when(i > 0)
    def _():
        xp_scr[0:CONV_HALO, :] = halo_ref[0]

    xp_scr[CONV_HALO:CONV_HALO + tl, :] = glu_ref[0]
    for s in range(1, SUBLANES):
        sh_scr[s - 1, 0:CONV_HALO + tl - SUBLANES, :] = xp_scr[s:s + CONV_HALO + tl - SUBLANES, :]
    rb = min(CONV_RB, tl)
    for r0 in range(0, tl, rb):
        acc = jnp.broadcast_to(b_ref[...], (rb, CONV_CH))
        for k in range(CONV_K):
            phase = (pad + k) % SUBLANES
            base = r0 + pad + k - phase
            rows = xp_scr[base:base + rb, :] if phase == 0 else sh_scr[phase - 1, base:base + rb, :]
            acc = acc + w_ref[k:k + 1, :] * rows
        y = _layernorm(acc, lg_ref[...], lb_ref[...])
        o_ref[0, r0:r0 + rb, :] = _silu(y).astype(BF16)


def _conv_a(glu, ctx, w, b, lg, lb):
    bsz, l, _ = glu.shape
    tl = _row_tile(l)
    hb = tl // CONV_HALO
    full = lambda a: pl.BlockSpec(a.shape, lambda bi, i: (0,) * a.ndim)
    return pl.pallas_call(
        functools.partial(_conv_a_kernel, tl=tl),
        grid=(bsz, l // tl),
        in_specs=[pl.BlockSpec((1, tl, CONV_CH), lambda bi, i: (bi, i, 0)),
                  pl.BlockSpec((1, CONV_HALO, CONV_CH), lambda bi, i: (bi, jnp.maximum(i * hb - 1, 0), 0)),
                  pl.BlockSpec((1, CONV_K - 1, CONV_CH), lambda bi, i: (bi, 0, 0)),
                  full(w), full(b), full(lg), full(lb)],
        out_specs=pl.BlockSpec((1, tl, CONV_CH), lambda bi, i: (bi, i, 0)),
        out_shape=jax.ShapeDtypeStruct((bsz, l, CONV_CH), BF16),
        scratch_shapes=[pltpu.VMEM((CONV_HALO + tl, CONV_CH), F32),
                        pltpu.VMEM((SUBLANES - 1, CONV_HALO + tl, CONV_CH), F32)],
        compiler_params=_cparams(("parallel", "arbitrary")),
        name="conv_a",
    )(glu, glu, ctx, w, b, lg, lb)


def _lambda(lq1, lk1, lq2, lk2, lam_init):
    return (jnp.exp(jnp.sum(lq1[...] * lk1[...], axis=-1, keepdims=True))
            - jnp.exp(jnp.sum(lq2[...] * lk2[...], axis=-1, keepdims=True)) + lam_init)


def _split_q(q):
    lane = lax.broadcasted_iota(jnp.int32, q.shape, 1)
    z = jnp.zeros_like(q)
    return jnp.concatenate([jnp.where(lane < DA_HEAD_DIM, q, z), jnp.where(lane >= DA_HEAD_DIM, q, z)], axis=0)


def _diff_finish(acc, l, tq, lam, sg, lam_init):
    o = acc[0:tq] / l[0:tq] - lam * (acc[tq:2 * tq] / l[tq:2 * tq])
    return _rms(o, sg) * (1.0 - lam_init)


ATTN_TQ = 512
ATTN_TK = 512


def _attn_prompt_kernel(qi_ref, ki_ref, fl_ref, q_ref, k_ref, v_ref, lq1, lk1, lq2, lk2, sg_ref, o_ref,
                        qs_scr, m_scr, acc_scr, *, tq, tk, lam_init):
    p = pl.program_id(1)
    qi = qi_ref[p]
    ki = ki_ref[p]
    flags = fl_ref[p]

    @pl.when(ki == 0)
    def _():
        for h in range(DA_HEADS):
            qs_scr[h] = _split_q(q_ref[0, :, h * LANES:(h + 1) * LANES])
        m_scr[...] = jnp.full(m_scr.shape, NEG_INF, F32)
        acc_scr[...] = jnp.zeros(acc_scr.shape, F32)

    ones_col = jnp.where(lax.broadcasted_iota(jnp.int32, (tk, LANES), 1) == 0, 1.0, 0.0).astype(BF16)

    def step(masked):
        for h in range(DA_HEADS):
            sl = slice(h * LANES, (h + 1) * LANES)
            s = _dot_nt(qs_scr[h], k_ref[0, :, sl])
            if masked:
                r = lax.broadcasted_iota(jnp.int32, s.shape, 0)
                c = lax.broadcasted_iota(jnp.int32, s.shape, 1)
                q_pos = qi * tq + jnp.where(r >= tq, r - tq, r)
                s = jnp.where(((ki * tk + c) >> 6) <= (q_pos >> 6), s, -1e30)
            m_old = m_scr[h]
            m_new = jnp.maximum(m_old, jnp.broadcast_to(jnp.max(s, axis=-1, keepdims=True), m_old.shape))
            alpha = jnp.exp2(m_old - m_new)
            pr = jnp.exp2(s - jnp.concatenate([m_new] * (tk // LANES), axis=1)).astype(BF16)
            v_aug = jnp.concatenate([v_ref[0, :, sl], ones_col], axis=1)
            acc_scr[h] = jnp.concatenate([alpha, alpha], axis=1) * acc_scr[h] + _dot(pr, v_aug)
            m_scr[h] = m_new

    @pl.when((flags & 1) == 0)
    def _():
        step(False)

    @pl.when((flags & 1) == 1)
    def _():
        step(True)

    @pl.when((flags & 2) != 0)
    def _():
        lam = _lambda(lq1, lk1, lq2, lk2, lam_init)
        for h in range(DA_HEADS):
            acc = acc_scr[h]
            o = _diff_finish(acc[:, 0:LANES], acc[:, LANES:LANES + 1], tq, lam, sg_ref[...], lam_init)
            o_ref[0, :, h * LANES:(h + 1) * LANES] = o.astype(BF16)


def _attn_prompt(q, k, v, lq1, lk1, lq2, lk2, sg, lam_init):
    bsz, l, _ = q.shape
    tq = _row_tile(l, ATTN_TQ)
    tk = _row_tile(l, ATTN_TK)
    pairs = []
    for a in range(l // tq):
        last = ((a + 1) * tq - 1) // tk
        for b in range(last + 1):
            crosses = (b + 1) * tk > a * tq + CHUNK
            pairs.append((a, b, int(crosses) + 2 * int(b == last)))
    tabs = [jnp.asarray([pr[i] for pr in pairs], jnp.int32) for i in range(3)]
    small = lambda a: pl.BlockSpec(a.shape, lambda bi, p, qt, kt, fl: (0,) * a.ndim)
    grid_spec = pltpu.PrefetchScalarGridSpec(
        num_scalar_prefetch=3,
        grid=(bsz, len(pairs)),
        in_specs=[pl.BlockSpec((1, tq, DA_QK), lambda bi, p, qt, kt, fl: (bi, qt[p], 0)),
                  pl.BlockSpec((1, tk, DA_QK), lambda bi, p, qt, kt, fl: (bi, kt[p], 0)),
                  pl.BlockSpec((1, tk, DA_QK), lambda bi, p, qt, kt, fl: (bi, kt[p], 0)),
                  small(lq1), small(lk1), small(lq2), small(lk2), small(sg)],
        out_specs=pl.BlockSpec((1, tq, DA_QK), lambda bi, p, qt, kt, fl: (bi, qt[p], 0)),
        scratch_shapes=[pltpu.VMEM((DA_HEADS, 2 * tq, LANES), BF16), pltpu.VMEM((DA_HEADS, 2 * tq, LANES), F32),
                        pltpu.VMEM((DA_HEADS, 2 * tq, 2 * LANES), F32)],
    )
    return pl.pallas_call(
        functools.partial(_attn_prompt_kernel, tq=tq, tk=tk, lam_init=lam_init),
        grid_spec=grid_spec,
        out_shape=jax.ShapeDtypeStruct((bsz, l, DA_QK), BF16),
        compiler_params=_cparams(("parallel", "arbitrary")),
        name="attn_prompt",
    )(*tabs, q, k, v, lq1, lk1, lq2, lk2, sg)


def _attn_sample_kernel(q_ref, kc_ref, vc_ref, kn_ref, vn_ref, lq1, lk1, lq2, lk2, sg_ref, o_ref,
                        *, tq, past_len, lam_init):
    lam = _lambda(lq1, lk1, lq2, lk2, lam_init)
    for h in range(DA_HEADS):
        sl = slice(h * LANES, (h + 1) * LANES)
        qs = _split_q(q_ref[0, :, sl])
        s_c = _dot_nt(qs, kc_ref[0, :, sl].astype(BF16))
        s_n = _dot_nt(qs, kn_ref[0, :, sl])
        r = lax.broadcasted_iota(jnp.int32, s_n.shape, 0)
        c = lax.broadcasted_iota(jnp.int32, s_n.shape, 1)
        qrow = jnp.where(r >= tq, r - tq, r)
        s_n = jnp.where(((past_len + c) >> 6) <= ((past_len + qrow) >> 6), s_n, -1e30)
        m = jnp.maximum(jnp.max(s_c, axis=-1, keepdims=True), jnp.max(s_n, axis=-1, keepdims=True))
        p_c = jnp.exp2(s_c - m)
        p_n = jnp.exp2(s_n - m)
        l = jnp.sum(p_c, axis=-1, keepdims=True) + jnp.sum(p_n, axis=-1, keepdims=True)
        acc = (_dot(p_c.astype(BF16), vc_ref[0, :, sl].astype(BF16))
               + _dot(p_n.astype(BF16), vn_ref[0, :, sl]))
        o_ref[0, :, sl] = _diff_finish(acc, l, tq, lam, sg_ref[...], lam_init).astype(BF16)


def _attn_sample(q, k, v, k_past, v_past, lq1, lk1, lq2, lk2, sg, lam_init):
    bsz, l, _ = q.shape
    past_len = k_past.shape[1]
    small = lambda a: pl.BlockSpec(a.shape, lambda bi: (0,) * a.ndim)
    new = pl.BlockSpec((1, l, DA_QK), lambda bi: (bi, 0, 0))
    old = pl.BlockSpec((1, past_len, DA_QK), lambda bi: (bi, 0, 0))
    return pl.pallas_call(
        functools.partial(_attn_sample_kernel, tq=l, past_len=past_len, lam_init=lam_init),
        grid=(bsz,),
        in_specs=[new, old, old, new, new, small(lq1), small(lk1), small(lq2), small(lk2), small(sg)],
        out_specs=new,
        out_shape=jax.ShapeDtypeStruct((bsz, l, DA_QK), BF16),
        compiler_params=_cparams(("parallel",)),
        name="attn_sample",
    )(q, k_past, v_past, k, v, lq1, lk1, lq2, lk2, sg)


def _mem_kv_kernel(m_ref, wk_ref, wv_ref, k32_ref, v32_ref, k16_ref, v16_ref):
    m = m_ref[...].astype(BF16)
    k = _dot(m, wk_ref[0])
    v = _dot(m, wv_ref[0])
    k32_ref[0] = k
    v32_ref[0] = v
    k16_ref[0] = k.astype(BF16)
    v16_ref[0] = v.astype(BF16)


def _mem_kv(mem, wk, wv):
    n = mem.shape[0]
    depth = wk.shape[0]
    tm = _row_tile(n)
    wspec = pl.BlockSpec((1, D_MODEL, D_MODEL), lambda l, i: (l, 0, 0))
    ospec = pl.BlockSpec((1, tm, D_MODEL), lambda l, i: (l, i, 0))
    return pl.pallas_call(
        _mem_kv_kernel,
        grid=(depth, n // tm),
        in_specs=[pl.BlockSpec((tm, D_MODEL), lambda l, i: (i, 0)), wspec, wspec],
        out_specs=[ospec] * 4,
        out_shape=[jax.ShapeDtypeStruct((depth, n, D_MODEL), F32)] * 2
        + [jax.ShapeDtypeStruct((depth, n, D_MODEL), BF16)] * 2,
        compiler_params=_cparams(("parallel", "parallel")),
        name="mem_kv",
    )(mem, wk, wv)


def _cross_kernel(x_ref, a_ref, b_ref, wm_ref, g_ref, wq_ref, wo_ref, mk_ref, mv_ref, o_ref):
    x = x_ref[0] + _dot(a_ref[0], wm_ref[0:512, :]) + _dot(b_ref[0], wm_ref[512:1024, :])
    h = _rms(x, g_ref[...]).astype(BF16)
    q = _dot(h, wq_ref[...]).astype(BF16)
    outs = []
    for hd in range(X_HEADS):
        sl = slice(hd * X_HEAD_DIM, (hd + 1) * X_HEAD_DIM)
        s = _dot_nt(q[:, sl], mk_ref[0, 0, :, sl].astype(BF16)) * (X_HEAD_DIM ** -0.5)
        s = s - jnp.max(s, axis=-1, keepdims=True)
        e = jnp.exp(s)
        p = (e / jnp.sum(e, axis=-1, keepdims=True)).astype(BF16)
        outs.append(_dot(p, mv_ref[0, 0, :, sl].astype(BF16)).astype(BF16))
    o = jnp.concatenate(outs, axis=-1)
    o_ref[0] = x + _dot(o, wo_ref[...])


def _cross(x, a, b, wm, g, wq, wo, mk, mv, layer):
    bsz, l, _ = x.shape
    tm = _row_tile(l)
    full = lambda arr: pl.BlockSpec(arr.shape, lambda bi, i: (0,) * arr.ndim)
    xs = pl.BlockSpec((1, tm, D_MODEL), lambda bi, i: (bi, i, 0))
    hs = pl.BlockSpec((1, tm, 512), lambda bi, i: (bi, i, 0))
    ms = pl.BlockSpec((1, 1, N_MEM, D_MODEL), lambda bi, i: (layer, bi, 0, 0))
    return pl.pallas_call(
        _cross_kernel,
        grid=(bsz, l // tm),
        in_specs=[xs, hs, hs, full(wm), full(g), full(wq), full(wo), ms, ms],
        out_specs=xs,
        out_shape=jax.ShapeDtypeStruct((bsz, l, D_MODEL), F32),
        compiler_params=_cparams(("parallel", "parallel")),
        name="cross_attn",
    )(x, a, b, wm, g, wq, wo, mk, mv)


PEER_T = 512
PEER_EC = 2048
PEER_SUB = 512
N_CAND = 56


def _top_values(s, k):
    out = []
    work = s
    for _ in range(k):
        m = jnp.max(work, axis=0, keepdims=True)
        out.append(m)
        work = jnp.where(work >= m, NEG_INF, work)
    return out


def _bf16_value(x):
    return x.astype(BF16).astype(F32)


def _bf16_below(x):
    return lax.bitcast_convert_type(lax.bitcast_convert_type(x, jnp.int32) - 0x10000, F32)


def _bf16_pair_word(x):
    bits = lax.bitcast_convert_type(x, jnp.uint32)
    return lax.bitcast_convert_type((bits & jnp.uint32(0xFFFF0000)) | (bits >> 16), jnp.int32)


EXP_FLOOR = -80.0


def _sorting_network(n):
    def merge(lo, hi, r):
        step = r * 2
        if step < hi - lo:
            yield from merge(lo, hi, step)
            yield from merge(lo + r, hi, step)
            yield from [(i, i + r) for i in range(lo + r, hi - r, step)]
        else:
            yield (lo, lo + r)

    def sort(lo, hi):
        if hi - lo >= 1:
            mid = lo + (hi - lo) // 2
            yield from sort(lo, mid)
            yield from sort(mid + 1, hi)
            yield from merge(lo, hi, 1)

    return list(sort(0, n - 1))


KEY_VREGS = PEER_KEYS // SUBLANES
KEY_SORT = _sorting_network(KEY_VREGS)


def _top_values_keys(s, k):
    rows = [s[SUBLANES * r:SUBLANES * (r + 1)] for r in range(KEY_VREGS)]
    for i, j in KEY_SORT:
        rows[i], rows[j] = jnp.maximum(rows[i], rows[j]), jnp.minimum(rows[i], rows[j])
    vals = []
    for r in range(k):
        m = jnp.max(rows[0], axis=0, keepdims=True)
        vals.append(m)
        still_needed = k - 1 - r
        if still_needed == 0:
            break
        hit = rows[0] >= m
        for d in range(min(still_needed, KEY_VREGS)):
            below = rows[d + 1] if d + 1 < KEY_VREGS else NEG_INF
            rows[d] = jnp.where(hit, below, rows[d])
    return vals


def _ranked_weights(s, k):
    vals = _top_values_keys(s, k)
    nums = [jnp.ones_like(vals[0])]
    for r in range(1, k):
        v = _bf16_value(jnp.exp(jnp.maximum(vals[r] - vals[0], EXP_FLOOR)))
        nums.append(jnp.minimum(v, _bf16_below(nums[-1])))
    placed = jnp.zeros(s.shape, F32)
    for r in range(k):
        placed = jnp.where(s == vals[r], nums[r], placed)
    return vals, nums, placed


def _peer_route_kernel(x_ref, g_ref, wq_ref, sk_ref, ht_ref, ecut_ref, e0_ref, e1_ref, e1_scr):
    h = _rms(x_ref[...], g_ref[...])
    ht_ref[...] = h.T.astype(BF16)
    q = _dot(h.astype(BF16), wq_ref[...]).astype(BF16)
    t = h.shape[0]
    half = PEER_KEYS // 2
    for hd in range(PEER_HEADS):
        base = hd * 2 * PEER_KEYS
        s0_all = _dot_nt(sk_ref[0], q[:, base:base + PEER_KEYS])
        s1_all = _dot_nt(sk_ref[1], q[:, base + PEER_KEYS:base + 2 * PEER_KEYS])
        for tc in range(t // LANES):
            cols = slice(tc * LANES, (tc + 1) * LANES)
            s0 = s0_all[:, cols]
            u0 = _top_values_keys(s0, PEER_TOPK + 1)
            u1, n1, e1 = _ranked_weights(s1_all[:, cols], PEER_TOPK + 1)
            cands = [u0[a] + u1[b] for a in range(PEER_TOPK + 1) for b in range(PEER_TOPK + 1)
                     if (a + 1) * (b + 1) <= PEER_TOPK + 1]
            cands += [jnp.full((1, LANES), NEG_INF, F32)] * (N_CAND - len(cands))
            best = _top_values(jnp.concatenate(cands, axis=0), PEER_TOPK + 1)
            thr = 0.5 * (best[PEER_TOPK - 1] + best[PEER_TOPK])
            z = jnp.ones_like(best[0])
            for b in best[1:PEER_TOPK]:
                z = z + jnp.exp(b - best[0])
            cut = thr - s0
            ecut = jnp.full(cut.shape, 2.0, F32)
            for r in range(PEER_TOPK + 1):
                ecut = jnp.where(u1[r] > cut, n1[r], ecut)
            ecut_ref[hd, :, cols] = _bf16_pair_word(ecut)
            e0_ref[hd, :, cols] = _bf16_pair_word(
                _bf16_value(jnp.exp(jnp.maximum(s0 - u0[0], EXP_FLOOR)) / z))
            e1_scr[hd, tc] = e1
            lo = lax.bitcast_convert_type(e1_scr[hd, tc, pl.ds(0, half, stride=2), :], jnp.uint32) >> 16
            hi = (lax.bitcast_convert_type(e1_scr[hd, tc, pl.ds(1, half, stride=2), :], jnp.uint32)
                  & jnp.uint32(0xFFFF0000))
            e1_ref[hd, :, cols] = lax.bitcast_convert_type(hi | lo, jnp.int32)


def _peer_route(x, g, wq, sk):
    n = x.shape[0]
    t = _row_tile(n, PEER_T)
    full = lambda a: pl.BlockSpec(a.shape, lambda i: (0,) * a.ndim)
    rspec = pl.BlockSpec((PEER_HEADS, PEER_KEYS, t), lambda i: (0, 0, i))
    rshape = jax.ShapeDtypeStruct((PEER_HEADS, PEER_KEYS, n), jnp.int32)
    e1spec = pl.BlockSpec((PEER_HEADS, PEER_KEYS // 2, t), lambda i: (0, 0, i))
    e1shape = jax.ShapeDtypeStruct((PEER_HEADS, PEER_KEYS // 2, n), jnp.int32)
    return pl.pallas_call(
        _peer_route_kernel,
        grid=(n // t,),
        in_specs=[pl.BlockSpec((t, D_MODEL), lambda i: (i, 0)), full(g), full(wq), full(sk)],
        out_specs=[pl.BlockSpec((D_MODEL, t), lambda i: (0, i)), rspec, rspec, e1spec],
        out_shape=[jax.ShapeDtypeStruct((D_MODEL, n), BF16), rshape, rshape, e1shape],
        scratch_shapes=[pltpu.VMEM((PEER_HEADS, t // LANES, PEER_KEYS, LANES), F32)],
        compiler_params=_cparams(("parallel",)),
        name="peer_route",
    )(x, g, wq, sk)


def _peer_dense_kernel(ht_ref, ecut_ref, e0_ref, e1_ref, eu_ref, evt_ref, x_ref, *rest, final):
    if final:
        fg_ref, o_ref, acc_scr, at_scr, w_scr = rest
    else:
        o_ref, acc_scr, at_scr, w_scr = rest
    j = pl.program_id(1)
    rows_step = PEER_EC // PEER_KEYS
    rows_sub = PEER_SUB // PEER_KEYS

    @pl.when(j == 0)
    def _():
        acc_scr[...] = jnp.zeros(acc_scr.shape, F32)

    t = ht_ref.shape[1]

    def packed_row(row, cols):
        return pltpu.bitcast(jnp.broadcast_to(row[:, cols], (PEER_KEYS // 2, LANES)), BF16)

    def hidden_and_weights(sc):
        rows = slice(sc * PEER_SUB, (sc + 1) * PEER_SUB)
        for ii in range(rows_sub):
            i0 = j * rows_step + sc * rows_sub + ii
            r0 = sc * PEER_SUB + ii * PEER_KEYS
            ecut_rows = [ecut_ref[hd, pl.ds(i0, 1), :] for hd in range(PEER_HEADS)]
            e0_rows = [e0_ref[hd, pl.ds(i0, 1), :] for hd in range(PEER_HEADS)]
            for tc in range(t // LANES):
                cols = slice(tc * LANES, (tc + 1) * LANES)
                w = None
                for hd in range(PEER_HEADS):
                    e1 = pltpu.bitcast(e1_ref[hd, :, cols], BF16)
                    term = jnp.where(e1 >= packed_row(ecut_rows[hd], cols), e1, jnp.zeros_like(e1))
                    term = term * packed_row(e0_rows[hd], cols)
                    w = term if w is None else w + term
                w_scr[r0:r0 + PEER_KEYS, cols] = w
        at_scr[rows, :] = _dot(eu_ref[rows, :], ht_ref[...]).astype(BF16)

    def expert_output(sc):
        rows = slice(sc * PEER_SUB, (sc + 1) * PEER_SUB)
        return _dot(evt_ref[0, :, rows], w_scr[rows, :] * jax.nn.gelu(at_scr[rows, :]))

    n_sub = PEER_EC // PEER_SUB
    hidden_and_weights(0)
    total = None
    for sc in range(n_sub):
        if sc + 1 < n_sub:
            hidden_and_weights(sc + 1)
        part = expert_output(sc)
        total = part if total is None else total + part
    acc_scr[...] += total

    @pl.when(j == pl.num_programs(1) - 1)
    def _():
        y = x_ref[...] + acc_scr[...].T
        o_ref[...] = _rms(y, fg_ref[...]) if final else y


def _peer_dense(x, ht, ecut, e0, e1, eu, evt, final_g=None):
    n = x.shape[0]
    t = _row_tile(n, PEER_T)
    rspec = pl.BlockSpec((PEER_HEADS, PEER_KEYS, t), lambda i, j: (0, 0, i))
    e1spec = pl.BlockSpec((PEER_HEADS, PEER_KEYS // 2, t), lambda i, j: (0, 0, i))
    xspec = pl.BlockSpec((t, D_MODEL), lambda i, j: (i, 0))
    final = final_g is not None
    extra = [final_g] if final else []
    return pl.pallas_call(
        functools.partial(_peer_dense_kernel, final=final),
        grid=(n // t, PEER_EXPERTS // PEER_EC),
        in_specs=[pl.BlockSpec((D_MODEL, t), lambda i, j: (0, i)), rspec, rspec, e1spec,
                  pl.BlockSpec((PEER_EC, D_MODEL), lambda i, j: (j, 0)),
                  pl.BlockSpec((1, D_MODEL, PEER_EC), lambda i, j: (j, 0, 0)), xspec]
        + [pl.BlockSpec(g.shape, lambda i, j: (0, 0)) for g in extra],
        out_specs=xspec,
        out_shape=jax.ShapeDtypeStruct((n, D_MODEL), F32),
        scratch_shapes=[pltpu.VMEM((D_MODEL, t), F32), pltpu.VMEM((PEER_EC, t), BF16),
                        pltpu.VMEM((PEER_EC, t), BF16)],
        compiler_params=_cparams(("parallel", "arbitrary")),
        name="peer_dense",
    )(ht, ecut, e0, e1, eu, evt, x, *extra)


def _cd_in_kernel(x_ref, g_ref, wc_ref, wg_ref, wx_ref, wd_ref, lg_ref, lb_ref,
                  u_ref, v_ref, gate_ref, xbc_ref, dt_ref):
    h = _rms(x_ref[...], g_ref[...]).astype(BF16)
    u_ref[...] = jax.nn.gelu(_dot(h, wc_ref[:, 0:512]))
    v_ref[...] = _layernorm(jax.nn.gelu(_dot(h, wc_ref[:, 512:1024])), lg_ref[...], lb_ref[...])
    gate_ref[...] = _dot(h, wg_ref[...])
    xbc_ref[...] = _dot(h, wx_ref[...])
    dt_ref[...] = _dot(h, wd_ref[...])


def _cd_in(x, g, wc, wg, wx, wd, lg, lb):
    n = x.shape[0]
    tm = _row_tile(n)
    row = lambda c: pl.BlockSpec((tm, c), lambda i: (i, 0))
    full = lambda a: pl.BlockSpec(a.shape, lambda i: (0,) * a.ndim)
    widths = (512, 512, 512, SSM_CONV_CH, LANES)
    return pl.pallas_call(
        _cd_in_kernel,
        grid=(n // tm,),
        in_specs=[row(D_MODEL), full(g), full(wc), full(wg), full(wx), full(wd), full(lg), full(lb)],
        out_specs=[row(c) for c in widths],
        out_shape=[jax.ShapeDtypeStruct((n, c), F32) for c in widths],
        compiler_params=_cparams(("parallel",)),
        name="cd_in",
    )(x, g, wc, wg, wx, wd, lg, lb)


def _gmlp_kernel(u_ref, v_ref, ws_ref, bs_ref, o_ref, *, gm_len, n_chunks):
    r = lax.broadcasted_iota(jnp.int32, (gm_len, gm_len), 0)
    c = lax.broadcasted_iota(jnp.int32, (gm_len, gm_len), 1)
    for g in range(GM_GROUPS):
        w = jnp.where(r >= c, ws_ref[g], 0.0).astype(BF16)
        ch = slice(g * LANES, (g + 1) * LANES)
        for ci in range(n_chunks):
            rows = slice(ci * gm_len, (ci + 1) * gm_len)
            mixed = _dot(w, v_ref[0, rows, ch].astype(BF16)) + bs_ref[:, g:g + 1]
            o_ref[0, rows, ch] = (u_ref[0, rows, ch] * mixed).astype(BF16)


def _gmlp(u, v, ws, bs_t, gm_len):
    bsz, l, _ = u.shape
    tl = _row_tile(l)
    full = lambda a: pl.BlockSpec(a.shape, lambda bi, i: (0,) * a.ndim)
    spec = pl.BlockSpec((1, tl, GM_WIDTH), lambda bi, i: (bi, i, 0))
    return pl.pallas_call(
        functools.partial(_gmlp_kernel, gm_len=gm_len, n_chunks=tl // gm_len),
        grid=(bsz, l // tl),
        in_specs=[spec, spec, full(ws), full(bs_t)],
        out_specs=spec,
        out_shape=jax.ShapeDtypeStruct((bsz, l, GM_WIDTH), BF16),
        compiler_params=_cparams(("parallel", "parallel")),
        name="gmlp",
    )(u, v, ws, bs_t)


SSD_HALO = 8
SSD_PAIRS = SSM_HEADS // 2
SSD_GROUP_W = SSM_INNER // 2


def _ssd_kernel(xbc_ref, halo_ref, ctx_ref, gate_ref, dt_ref, h0_ref, cw_ref, cb_ref, dtb_ref, alog_ref,
                dsk_ref, ng_ref, y_ref, fin_ref, xp_scr, st_scr, *, tl):
    i = pl.program_id(1)
    pad = SSD_HALO - (SSM_CONV_K - 1)
    q = CHUNK

    @pl.when(i == 0)
    def _():
        xp_scr[pad:SSD_HALO, :] = ctx_ref[0]
        for k in range(SSD_PAIRS):
            st_scr[k] = jnp.concatenate([h0_ref[0, 2 * k], h0_ref[0, 2 * k + 1]], axis=0).T

    @pl.when(i > 0)
    def _():
        xp_scr[0:SSD_HALO, :] = halo_ref[0]

    xp_scr[SSD_HALO:SSD_HALO + tl, :] = xbc_ref[0]

    lane128 = lax.broadcasted_iota(jnp.int32, (q, LANES), 1)
    row128 = lax.broadcasted_iota(jnp.int32, (q, LANES), 0)
    er = lax.broadcasted_iota(jnp.int32, (LANES, SSM_INNER), 0)
    ec = lax.broadcasted_iota(jnp.int32, (LANES, SSM_INNER), 1)
    expand = jnp.where((ec >> 6) == er, 1.0, 0.0).astype(F32)
    tr = lax.broadcasted_iota(jnp.int32, (q, q), 0)
    tc = lax.broadcasted_iota(jnp.int32, (q, q), 1)
    ltri = jnp.where(tr >= tc, 1.0, 0.0).astype(F32)
    a_neg = -jnp.exp(alog_ref[...])
    zeros_q = jnp.zeros((q, LANES), F32)

    for ci in range(tl // q):
        r0 = ci * q
        acc = jnp.broadcast_to(cb_ref[...], (q, SSM_CONV_CH))
        for k in range(SSM_CONV_K):
            acc = acc + cw_ref[k:k + 1, :] * xp_scr[r0 + pad + k:r0 + pad + k + q, :]
        xc = _silu(acc)
        xs = xc[:, 0:SSM_INNER]
        dt = jnp.where(lane128 < SSM_HEADS, _softplus(dt_ref[0, r0:r0 + q, :] + dtb_ref[...]), 0.0)
        acs = _dot_f32(ltri, dt * a_neg)
        acs_e = _dot_f32(acs, expand)
        dt_e = _dot_f32(dt, expand)
        tot_e = acs_e[q - 1:q, :]
        xdt = xs * dt_e
        xd = (xdt * jnp.exp(tot_e - acs_e)).astype(BF16)
        eacs = jnp.exp(acs_e)
        cdec = jnp.exp(tot_e)
        ys = []
        for k in range(SSD_PAIRS):
            g = k // 2
            blk = slice(k * LANES, (k + 1) * LANES)
            bm = xc[:, SSM_INNER + g * SSM_STATE:SSM_INNER + (g + 1) * SSM_STATE]
            cm = xc[:, SSM_INNER + 2 * SSM_STATE + g * SSM_STATE:
                    SSM_INNER + 2 * SSM_STATE + (g + 1) * SSM_STATE].astype(BF16)
            cb2 = _dot_nt(cm, jnp.concatenate([bm, bm], axis=0).astype(BF16))
            a_blk = acs_e[:, blk]
            a_row = jnp.sum(jnp.where(row128 == (lane128 & (q - 1)), a_blk, 0.0), axis=0, keepdims=True)
            lmat = jnp.exp(jnp.where((lane128 & (q - 1)) <= row128, a_blk - a_row, NEG_INF))
            sc = (cb2 * lmat).astype(BF16)
            x_blk = xdt[:, blk]
            rhs = jnp.concatenate([jnp.where(lane128 < q, x_blk, 0.0), jnp.where(lane128 >= q, x_blk, 0.0)],
                                  axis=0).astype(BF16)
            prev = st_scr[k]
            y_pair = _dot(sc, rhs) + _dot(cm, prev.astype(BF16)) * eacs[:, blk]
            ys.append(y_pair)
            bt = jnp.concatenate([bm, zeros_q], axis=0).T.astype(BF16)
            xd_pad = jnp.concatenate([xd[:, blk], zeros_q.astype(BF16)], axis=0)
            st_scr[k] = prev * cdec[:, blk] + _dot(bt, xd_pad)
        y = jnp.concatenate(ys, axis=-1) + dsk_ref[...] * xs
        y = y * _silu(gate_ref[0, r0:r0 + q, :])
        outs = []
        for g in range(2):
            seg = y[:, g * SSD_GROUP_W:(g + 1) * SSD_GROUP_W]
            outs.append(seg * lax.rsqrt(jnp.mean(seg * seg, axis=-1, keepdims=True) + 1e-6))
        y_ref[0, r0:r0 + q, :] = (jnp.concatenate(outs, axis=-1) * ng_ref[...]).astype(BF16)

    @pl.when(i == pl.num_programs(1) - 1)
    def _():
        for k in range(SSD_PAIRS):
            st = st_scr[k].T
            fin_ref[0, 2 * k] = st[0:SSM_HEAD_DIM]
            fin_ref[0, 2 * k + 1] = st[SSM_HEAD_DIM:2 * SSM_HEAD_DIM]


def _ssd(xbc, ctx, gate, dt, h0, cw, cb, dtb, alog, dsk, ng):
    bsz, l, _ = xbc.shape
    tl = 256 if l % 256 == 0 else l
    hb = tl // SSD_HALO
    full = lambda a: pl.BlockSpec(a.shape, lambda bi, i: (0,) * a.ndim)
    tile = lambda c: pl.BlockSpec((1, tl, c), lambda bi, i: (bi, i, 0))
    stspec = pl.BlockSpec((1, SSM_HEADS, SSM_HEAD_DIM, SSM_STATE), lambda bi, i: (bi, 0, 0, 0))
    return pl.pallas_call(
        functools.partial(_ssd_kernel, tl=tl),
        grid=(bsz, l // tl),
        in_specs=[tile(SSM_CONV_CH),
                  pl.BlockSpec((1, SSD_HALO, SSM_CONV_CH), lambda bi, i: (bi, jnp.maximum(i * hb - 1, 0), 0)),
                  pl.BlockSpec((1, SSM_CONV_K - 1, SSM_CONV_CH), lambda bi, i: (bi, 0, 0)),
                  tile(SSM_INNER), tile(LANES), stspec,
                  full(cw), full(cb), full(dtb), full(alog), full(dsk), full(ng)],
        out_specs=[tile(SSM_INNER), stspec],
        out_shape=[jax.ShapeDtypeStruct((bsz, l, SSM_INNER), BF16),
                   jax.ShapeDtypeStruct((bsz, SSM_HEADS, SSM_HEAD_DIM, SSM_STATE), F32)],
        scratch_shapes=[pltpu.VMEM((SSD_HALO + tl, SSM_CONV_CH), F32),
                        pltpu.VMEM((SSD_PAIRS, SSM_STATE, LANES), F32)],
        compiler_params=_cparams(("parallel", "arbitrary")),
        name="ssd",
    )(xbc, xbc, ctx, gate, dt, h0, cw, cb, dtb, alog, dsk, ng)


def _blocked_transpose(table):
    depth, _, d = table.shape
    return table.reshape(depth, PEER_EXPERTS // PEER_EC, PEER_EC, d).swapaxes(2, 3)


def _pad_lanes(a, width=LANES):
    return jnp.pad(a, ((0, 0), (0, width - a.shape[-1])))


def _trunk(x, mem_k, mem_v, attn_k, attn_v, conv_a, ssd_st, conv_ssm, gm_len, p):
    bsz, l, _ = x.shape
    n = bsz * l
    row = lambda a: a.reshape(1, -1)
    x2 = x.reshape(n, D_MODEL)

    lam_init = 0.8 - 0.6 * math.exp(-0.3 * 0)
    glu, q16, k32, v32, k16, v16 = _ab_in(x2, row(p["norm_mix_g"][0]), p["w_in_ab"])
    glu3 = glu.reshape(bsz, l, CONV_CH)
    ctx_a = jnp.zeros((bsz, CONV_K - 1, CONV_CH), F32) if conv_a is None else conv_a[0]
    ca = _conv_a(glu3, ctx_a, p["conv_a_w"], row(p["conv_a_b"]), row(p["ln_a_g"]), row(p["ln_a_b"]))
    lam_args = (row(p["lam_q1"]), row(p["lam_k1"]), row(p["lam_q2"]), row(p["lam_k2"]), row(p["subln_g"]))
    shp3 = lambda a: a.reshape(bsz, l, DA_QK)
    if attn_k is None:
        o = _attn_prompt(shp3(q16), shp3(k16), shp3(v16), *lam_args, lam_init)
    else:
        past = attn_k.shape[2]
        o = _attn_sample(shp3(q16), shp3(k16), shp3(v16), attn_k[0].reshape(bsz, past, DA_QK),
                         attn_v[0].reshape(bsz, past, DA_QK), *lam_args, lam_init)
    new_k = k32.reshape(1, bsz, l, DA_HEADS, 2 * DA_HEAD_DIM)
    new_v = v32.reshape(1, bsz, l, DA_HEADS, 2 * DA_HEAD_DIM)
    new_ca = glu3[:, l - (CONV_K - 1):][None]

    def tail(x2, a, b, w_mix, layer, final_g=None):
        x3 = _cross(x2.reshape(bsz, l, D_MODEL), a, b, w_mix, row(p["norm_cross_g"][layer]), p["w_xq"][layer],
                    p["w_xo"][layer], mem_k, mem_v, layer).reshape(n, D_MODEL)
        routed = _peer_route(x3, row(p["norm_ffn_g"][layer]), p["w_pq"][layer], p["sub_keys"][layer])
        return _peer_dense(x3, *routed, p["expert_u"][layer], p["expert_vt"][layer], final_g)

    x2 = tail(x2, ca, o, p["w_out_ab"], 0)

    u, vln, gate, xbc, dt = _cd_in(x2, row(p["norm_mix_g"][1]), p["w_cd_c"], p["w_cd_gate"], p["w_cd_xbc"],
                                   p["w_cd_dt"], row(p["ln_c_g"]), row(p["ln_c_b"]))
    shp = lambda a: a.reshape(bsz, l, a.shape[-1])
    c_out = _gmlp(shp(u), shp(vln), p["gm_w_s"][:, :gm_len, :gm_len], p["gm_b_s"][:, :gm_len].T, gm_len)
    xbc3 = shp(xbc)
    ctx_d = jnp.zeros((bsz, SSM_CONV_K - 1, SSM_CONV_CH), F32) if conv_ssm is None else conv_ssm[0]
    h0 = jnp.zeros((bsz, SSM_HEADS, SSM_HEAD_DIM, SSM_STATE), F32) if ssd_st is None else ssd_st[0]
    y, fin = _ssd(xbc3, ctx_d, shp(gate), shp(dt), h0, p["conv_d_w"], row(p["conv_d_b"]),
                  _pad_lanes(row(p["dt_bias"])), _pad_lanes(row(p["a_log"])),
                  row(jnp.repeat(p["d_skip"], SSM_HEAD_DIM)), row(p["norm_d_g"]))
    y_out = tail(x2, c_out, y, p["w_out_cd"], 1, row(p["norm_final_g"])).reshape(bsz, l, D_MODEL)
    new_gv = vln.reshape(1, bsz, l, GM_GROUPS, GM_WIDTH // GM_GROUPS)
    new_cs = xbc3[:, l - (SSM_CONV_K - 1):][None]
    return y_out, new_k, new_v, new_ca, new_gv, fin[None], new_cs


def kernel(x_prompt, x_sample, cache_attn_k, cache_attn_v, state_conv_a, state_ssd, state_conv_ssm, cache_mem_k, cache_mem_v, mem_prompt, norm_mix_g, norm_cross_g, norm_ffn_g, norm_final_g, w_in_ab, conv_a_w, conv_a_b, ln_a_g, ln_a_b, lam_q1, lam_k1, lam_q2, lam_k2, subln_g, w_out_ab, w_in_cd, ln_c_g, ln_c_b, gm_w_s, gm_b_s, conv_d_w, conv_d_b, dt_bias, a_log, d_skip, norm_d_g, w_out_cd, w_xq, w_xk, w_xv, w_xo, w_pq, sub_keys, expert_u, expert_v):
    bf = lambda a: a.astype(BF16)
    w_cd = w_in_cd[0]
    p = {
        "norm_mix_g": norm_mix_g, "norm_cross_g": norm_cross_g, "norm_ffn_g": norm_ffn_g,
        "norm_final_g": norm_final_g,
        "w_in_ab": bf(w_in_ab[0]), "conv_a_w": conv_a_w[0], "conv_a_b": conv_a_b[0],
        "ln_a_g": ln_a_g[0], "ln_a_b": ln_a_b[0],
        "lam_q1": lam_q1[0], "lam_k1": lam_k1[0], "lam_q2": lam_q2[0], "lam_k2": lam_k2[0],
        "subln_g": subln_g[0], "w_out_ab": bf(w_out_ab[0]),
        "w_cd_c": bf(w_cd[:, 0:1024]), "w_cd_gate": bf(w_cd[:, 1024:1536]), "w_cd_xbc": bf(w_cd[:, 1536:2560]),
        "w_cd_dt": bf(_pad_lanes(w_cd[:, 2560:2568])),
        "ln_c_g": ln_c_g[0], "ln_c_b": ln_c_b[0], "gm_w_s": gm_w_s[0], "gm_b_s": gm_b_s[0],
        "conv_d_w": conv_d_w[0], "conv_d_b": conv_d_b[0], "dt_bias": dt_bias[0], "a_log": a_log[0],
        "d_skip": d_skip[0], "norm_d_g": norm_d_g[0], "w_out_cd": bf(w_out_cd[0]),
        "w_xq": bf(w_xq), "w_xo": bf(w_xo), "w_pq": bf(w_pq), "sub_keys": bf(sub_keys),
        "expert_u": bf(expert_u), "expert_vt": _blocked_transpose(bf(expert_v)),
    }
    bsz, seq, _ = x_prompt.shape
    dec_b, dec_l, _ = x_sample.shape
    depth = w_xk.shape[0]

    mk32, mv32, mk16, mv16 = _mem_kv(mem_prompt.reshape(bsz * N_MEM, D_MODEL), bf(w_xk), bf(w_xv))
    mem_k_p = mk32.reshape(depth, bsz, N_MEM, X_HEADS, X_HEAD_DIM)
    mem_v_p = mv32.reshape(depth, bsz, N_MEM, X_HEADS, X_HEAD_DIM)
    y_prompt, kp, vp, cap, _, ssdp, csp = _trunk(
        x_prompt, mk16.reshape(depth, bsz, N_MEM, D_MODEL), mv16.reshape(depth, bsz, N_MEM, D_MODEL),
        None, None, None, None, None, 2 * CHUNK, p)

    y_sample, ks, vs, cas, gvs, ssds, css = _trunk(
        x_sample, cache_mem_k.reshape(depth, dec_b, N_MEM, D_MODEL), cache_mem_v.reshape(depth, dec_b, N_MEM, D_MODEL),
        cache_attn_k, cache_attn_v, state_conv_a, state_ssd, state_conv_ssm, dec_l, p)
    return (y_prompt, y_sample, kp, vp, cap, ssdp, csp, mem_k_p, mem_v_p, ks, vs, cas, gvs, ssds, css)
```
